```python
import math
import jax, jax.numpy as jnp
from jax import lax
import numpy as np

D_MODEL = 2048
BATCH = 4
SEQ = 4096
DEPTH = 4

GRID_W = 64
CTX_LEN = 256
HEAD_DIM = 64
V_DIM = 2 * HEAD_DIM
ATTN_WIDTH = D_MODEL // 2
N_HEADS = ATTN_WIDTH // (2 * HEAD_DIM)
CONV_WIDTH = D_MODEL // 2
CONV_K = 3
ROPE_THETA = 10000.0
ROPE_AXIS_DIM = HEAD_DIM // 2
Q_BLOCK = 128
N_EXPERTS = 16
N_GROUPS = 4
TOPK_GROUPS = 1
TOP_K = 2
D_EXPERT = D_MODEL // 2
EPS = 1e-6

K0 = ATTN_WIDTH
V0 = 2 * ATTN_WIDTH
CB0 = 3 * ATTN_WIDTH
CC0 = CB0 + CONV_WIDTH
CX0 = CC0 + CONV_WIDTH
GA0 = CX0 + CONV_WIDTH
GC0 = GA0 + D_MODEL
IN_COLS = GC0 + D_MODEL
SPLITS = [K0, V0, CB0, CC0, CX0, GA0, GC0]

kernel_name = "hybrid_diffattn_shortconv_grouped_moe_prefix"


def rmsnorm(x, g):
    xf = x.astype(jnp.float32)
    y = xf * lax.rsqrt(jnp.mean(xf * xf, axis=-1, keepdims=True) + EPS)
    return (y * g.astype(jnp.float32)).astype(x.dtype)


def modulate(h, shift, scale):
    return h * (1 + scale) + shift


def rope_1d(x, ang):
    x1, x2 = jnp.split(x, 2, axis=-1)
    cos = jnp.cos(ang).astype(x.dtype)
    sin = jnp.sin(ang).astype(x.dtype)
    return jnp.concatenate([x1 * cos - x2 * sin, x2 * cos + x1 * sin], axis=-1)


def rope_2d(x, ang_r, ang_c):
    xr, xc = jnp.split(x, 2, axis=-1)
    return jnp.concatenate([rope_1d(xr, ang_r), rope_1d(xc, ang_c)], axis=-1)


def to_qk_heads(t):
    b, l, _ = t.shape
    return t.reshape(b, l, N_HEADS, 2, HEAD_DIM).transpose(0, 2, 3, 1, 4)


def to_v_heads(t):
    b, l, _ = t.shape
    return t.reshape(b, l, N_HEADS, V_DIM).transpose(0, 2, 1, 3)


def diff_attn(q, k, v, lam):
    s = jnp.einsum('bhmqd,bhmkd->bhmqk', q, k).astype(jnp.float32) * (HEAD_DIM ** -0.5)
    p = jax.nn.softmax(s, axis=-1)
    a = p[:, :, 0] - lam * p[:, :, 1]
    return jnp.einsum('bhqk,bhkd->bhqd', a.astype(v.dtype), v)


def latent_diff_attention(q, k_all, v_all, lam):
    b, h, m, l, d = q.shape
    nb = l // Q_BLOCK
    qb = jnp.moveaxis(q.reshape(b, h, m, nb, Q_BLOCK, d), 3, 0)
    ob = lax.map(lambda qq: diff_attn(qq, k_all, v_all, lam), qb)
    return jnp.moveaxis(ob, 0, 2).reshape(b, h, l, V_DIM)


def short_conv(u, w, bias):
    up = jnp.pad(u, ((0, 0), (1, 1), (0, 0)))
    return up[:, :-2] * w[0] + up[:, 1:-1] * w[1] + up[:, 2:] * w[2] + bias


def mixer_merge(attn, cb, cc, cx, ga, gc, subln_g, lam_init, conv_w, conv_b, w_ab, w_cb, w_o):
    b, h, l, _ = attn.shape
    a = rmsnorm(attn, subln_g) * (1.0 - lam_init)
    a = a.transpose(0, 2, 1, 3).reshape(b, l, ATTN_WIDTH)
    y = cb * short_conv(cc * cx, conv_w, conv_b)
    merged = jax.nn.sigmoid(ga) * (a @ w_ab) + jax.nn.sigmoid(gc) * (y @ w_cb)
    return merged @ w_o


def moe(h, w_router, router_bias, wg, wu, wd):
    n = h.shape[0]
    s = jax.nn.sigmoid((h @ w_router).astype(jnp.float32))
    sb = s + router_bias.astype(jnp.float32)
    grp = sb.reshape(n, N_GROUPS, N_EXPERTS // N_GROUPS)
    gscore = lax.top_k(grp, 2)[0].sum(-1)
    _, gidx = lax.top_k(gscore, TOPK_GROUPS)
    gmask = jnp.sum(jax.nn.one_hot(gidx, N_GROUPS, dtype=jnp.float32), axis=1) > 0
    emask = jnp.repeat(gmask, N_EXPERTS // N_GROUPS, axis=1)
    _, eidx = lax.top_k(jnp.where(emask, sb, -jnp.inf), TOP_K)
    w = jnp.take_along_axis(s, eidx, axis=1)
    w = w / jnp.sum(w, axis=-1, keepdims=True)
    combine = jnp.sum(jax.nn.one_hot(eidx, N_EXPERTS, dtype=jnp.float32) * w[..., None], axis=1)
    combine = combine.astype(h.dtype)
    out = jnp.zeros_like(h)
    for e in range(N_EXPERTS):
        he = jax.nn.silu(h @ wg[e]) * (h @ wu[e])
        out = out + combine[:, e:e + 1] * (he @ wd[e])
    return out


def setup_inputs(seed: int = 0) -> dict:
    key = jax.random.key(seed)
    ks = jax.random.split(key, 32)
    f32 = jnp.float32

    def nrm(k, shape, scale):
        return jax.random.normal(k, shape, f32) * scale

    d = D_MODEL
    return {
        "x": nrm(ks[0], (BATCH, SEQ, d), 1.0),
        "c": nrm(ks[1], (BATCH, d), 1.0),
        "ctx": nrm(ks[2], (BATCH, CTX_LEN, d), 1.0),
        "c_ctx": nrm(ks[3], (d,), 1.0),
        "w_mod": nrm(ks[4], (DEPTH, d, 6 * d), 0.5 * d ** -0.5),
        "b_mod": nrm(ks[5], (DEPTH, 6 * d), 0.02),
        "norm1_g": 1.0 + nrm(ks[6], (DEPTH, d), 0.05),
        "norm2_g": 1.0 + nrm(ks[7], (DEPTH, d), 0.05),
        "w_in": nrm(ks[8], (DEPTH, d, IN_COLS), d ** -0.5),
        "lambda_q1": nrm(ks[9], (DEPTH, HEAD_DIM), 0.1),
        "lambda_k1": nrm(ks[10], (DEPTH, HEAD_DIM), 0.1),
        "lambda_q2": nrm(ks[11], (DEPTH, HEAD_DIM), 0.1),
        "lambda_k2": nrm(ks[12], (DEPTH, HEAD_DIM), 0.1),
        "subln_g": 1.0 + nrm(ks[13], (DEPTH, V_DIM), 0.05),
        "conv_w": nrm(ks[14], (DEPTH, CONV_K, CONV_WIDTH), CONV_K ** -0.5),
        "conv_b": nrm(ks[15], (DEPTH, CONV_WIDTH), 0.02),
        "w_attn_branch": nrm(ks[16], (DEPTH, ATTN_WIDTH, d), ATTN_WIDTH ** -0.5),
        "w_conv_branch": nrm(ks[17], (DEPTH, CONV_WIDTH, d), CONV_WIDTH ** -0.5),
        "w_out": nrm(ks[18], (DEPTH, d, d), d ** -0.5),
        "w_router": nrm(ks[19], (d, N_EXPERTS), d ** -0.5),
        "router_bias": nrm(ks[20], (N_EXPERTS,), 0.01),
        "w_exp_gate": nrm(ks[21], (DEPTH, N_EXPERTS, d, D_EXPERT), d ** -0.5),
        "w_exp_up": nrm(ks[22], (DEPTH, N_EXPERTS, d, D_EXPERT), d ** -0.5),
        "w_exp_down": nrm(ks[23], (DEPTH, N_EXPERTS, D_EXPERT, d), D_EXPERT ** -0.5),
        "final_g": 1.0 + nrm(ks[24], (d,), 0.05),
    }


def reference(x, c, ctx, c_ctx, w_mod, b_mod, norm1_g, norm2_g, w_in,
              lambda_q1, lambda_k1, lambda_q2, lambda_k2, subln_g, conv_w, conv_b,
              w_attn_branch, w_conv_branch, w_out, w_router, router_bias,
              w_exp_gate, w_exp_up, w_exp_down, final_g):
    b, seq, d = x.shape
    n_ctx = ctx.shape[1]
    ROWS = seq // GRID_W
    row = jnp.repeat(jnp.arange(ROWS), GRID_W).astype(jnp.float32)
    col = jnp.tile(jnp.arange(GRID_W), ROWS).astype(jnp.float32)
    inv = ROPE_THETA ** (-jnp.arange(0, ROPE_AXIS_DIM, 2, dtype=jnp.float32) / ROPE_AXIS_DIM)
    ang_r = row[:, None] * inv[None, :]
    ang_c = col[:, None] * inv[None, :]

    silu_c = jax.nn.silu(c)
    silu_cc = jax.nn.silu(c_ctx)

    for l in range(DEPTH):
        last = l == DEPTH - 1
        lam_init = 0.8 - 0.6 * math.exp(-0.3 * l)
        lam = (jnp.exp(jnp.sum(lambda_q1[l].astype(jnp.float32) * lambda_k1[l].astype(jnp.float32)))
               - jnp.exp(jnp.sum(lambda_q2[l].astype(jnp.float32) * lambda_k2[l].astype(jnp.float32)))
               + lam_init)

        mod_x = (silu_c @ w_mod[l] + b_mod[l])[:, None, :]
        mod_c = silu_cc @ w_mod[l] + b_mod[l]
        sh1, sc1, g1, sh2, sc2, g2 = jnp.split(mod_x, 6, axis=-1)
        csh1, csc1, cg1, csh2, csc2, cg2 = jnp.split(mod_c, 6, axis=-1)

        h = modulate(rmsnorm(x, norm1_g[l]), sh1, sc1)
        hc = modulate(rmsnorm(ctx, norm1_g[l]), csh1, csc1)
        q, k, v, cb, cc, cx, ga, gc = jnp.split(h @ w_in[l], SPLITS, axis=-1)
        q = rope_2d(to_qk_heads(q), ang_r, ang_c)
        k = rope_2d(to_qk_heads(k), ang_r, ang_c)
        v = to_v_heads(v)
        if last:
            kc, vc = jnp.split(hc @ w_in[l][:, K0:CB0], 2, axis=-1)
        else:
            qc, kc, vc, cbc, ccc, cxc, gac, gcc = jnp.split(hc @ w_in[l], SPLITS, axis=-1)
            qc = to_qk_heads(qc)
        kc = to_qk_heads(kc)
        vc = to_v_heads(vc)
        k_all = jnp.concatenate([k, kc], axis=3)
        v_all = jnp.concatenate([v, vc], axis=2)

        a_lat = latent_diff_attention(q, k_all, v_all, lam)
        x = x + g1 * mixer_merge(a_lat, cb, cc, cx, ga, gc, subln_g[l], lam_init,
                                 conv_w[l], conv_b[l], w_attn_branch[l], w_conv_branch[l], w_out[l])
        if not last:
            a_ctx = diff_attn(qc, kc, vc, lam)
            ctx = ctx + cg1 * mixer_merge(a_ctx, cbc, ccc, cxc, gac, gcc, subln_g[l], lam_init,
                                          conv_w[l], conv_b[l], w_attn_branch[l], w_conv_branch[l], w_out[l])

        h2 = modulate(rmsnorm(x, norm2_g[l]), sh2, sc2).reshape(b * seq, d)
        if last:
            y = moe(h2, w_router, router_bias, w_exp_gate[l], w_exp_up[l], w_exp_down[l])
            x = x + g2 * y.reshape(b, seq, d)
        else:
            h2c = modulate(rmsnorm(ctx, norm2_g[l]), csh2, csc2).reshape(b * n_ctx, d)
            y = moe(jnp.concatenate([h2, h2c], axis=0), w_router, router_bias,
                    w_exp_gate[l], w_exp_up[l], w_exp_down[l])
            x = x + g2 * y[:b * seq].reshape(b, seq, d)
            ctx = ctx + cg2 * y[b * seq:].reshape(b, n_ctx, d)

    return rmsnorm(x, final_g)
```

```python
import functools
import math

import jax
import jax.numpy as jnp
from jax import lax
from jax.experimental import pallas as pl
from jax.experimental.pallas import tpu as pltpu

HEAD_DIM = 64
V_DIM = 2 * HEAD_DIM
GRID_W = 64
ROPE_THETA = 10000.0
ROPE_AXIS_DIM = HEAD_DIM // 2
N_GROUPS = 4
EPS = 1e-6
LANES = 128
SUBLANES = 8
VMEM_LIMIT_BYTES = 56 * 1024 * 1024
MOD_ROWS = 8

F32 = jnp.float32
BF16 = jnp.bfloat16
HIGHEST = lax.Precision.HIGHEST
NT_DIMS = (((1,), (1,)), ((), ()))


def _params(*sem):
    return pltpu.CompilerParams(dimension_semantics=sem, vmem_limit_bytes=VMEM_LIMIT_BYTES)


def _rms(x):
    return x * lax.rsqrt(jnp.mean(x * x, axis=-1, keepdims=True) + EPS)


def _mod_kernel(c_ref, w_ref, b_ref, o_ref):
    c = c_ref[...]
    sc = c * jax.nn.sigmoid(c)
    o_ref[0] = jnp.dot(sc, w_ref[0], precision=HIGHEST, preferred_element_type=F32) + b_ref[0]


def _modulation(cvec, w_mod, b_mod):
    depth, d, n6 = w_mod.shape
    tn = _tile(1024, n6)
    return pl.pallas_call(
        _mod_kernel,
        grid=(depth, n6 // tn),
        in_specs=[
            pl.BlockSpec((MOD_ROWS, d), lambda l, j: (0, 0)),
            pl.BlockSpec((1, d, tn), lambda l, j: (l, 0, j)),
            pl.BlockSpec((1, 1, tn), lambda l, j: (l, 0, j)),
        ],
        out_specs=pl.BlockSpec((1, MOD_ROWS, tn), lambda l, j: (l, 0, j)),
        out_shape=jax.ShapeDtypeStruct((depth, MOD_ROWS, n6), F32),
        compiler_params=_params("arbitrary", "arbitrary"),
        name="modulation",
    )(cvec, w_mod, b_mod.reshape(depth, 1, n6))


def _in_kernel(x_ref, mod_ref, g_ref, w_ref, rc_ref, ra_ref, rb_ref, qkv_ref, rest_ref, h_scr,
               *, tm, seq, n_lat, nb, d, aw):
    i = pl.program_id(0)
    j = pl.program_id(1)

    @pl.when(j == 0)
    def _():
        seg = jnp.where(i * tm < n_lat, (i * tm) // seq, nb)
        m = mod_ref[pl.ds(seg, 1), :]
        y = _rms(x_ref[...]) * g_ref[...]
        h_scr[...] = (y * (1.0 + m[:, d:2 * d]) + m[:, 0:d]).astype(BF16)

    acc = jnp.dot(h_scr[...], w_ref[...], preferred_element_type=F32)

    @pl.when(j < 2)
    def _():
        scale = jnp.where(j == 0, HEAD_DIM ** -0.5, 1.0).astype(F32)
        rc, ra, rb = rc_ref[...], ra_ref[...], rb_ref[...]
        for c in range(aw // LANES):
            a = acc[:, c * LANES:(c + 1) * LANES]
            r = (a * rc + pltpu.roll(a, LANES - ROPE_AXIS_DIM // 2, 1) * ra
                 + pltpu.roll(a, ROPE_AXIS_DIM // 2, 1) * rb)
            qkv_ref[:, c * LANES:(c + 1) * LANES] = (r * scale).astype(BF16)

    @pl.when(j == 2)
    def _():
        qkv_ref[...] = acc.astype(BF16)

    @pl.when(j >= 3)
    def _():
        rest_ref[...] = acc


def _rope_tables(seq, tm):
    pos = jnp.arange(seq)
    row = (pos // GRID_W).astype(F32)
    col = (pos % GRID_W).astype(F32)
    inv = ROPE_THETA ** (-jnp.arange(0, ROPE_AXIS_DIM, 2, dtype=F32) / ROPE_AXIS_DIM)
    lane = jnp.arange(LANES)
    jj = lane % HEAD_DIM
    axis = jj // ROPE_AXIS_DIM
    r = jj % ROPE_AXIS_DIM
    f = r % (ROPE_AXIS_DIM // 2)
    half = r // (ROPE_AXIS_DIM // 2)
    posv = jnp.where(axis[None, :] == 0, row[:, None], col[:, None])
    ang = posv * inv[f][None, :]
    cos, sin = jnp.cos(ang), jnp.sin(ang)
    rc = jnp.concatenate([cos, jnp.ones((tm, LANES), F32)], axis=0)
    ra = jnp.concatenate([jnp.where(half[None, :] == 0, -sin, 0.0), jnp.zeros((tm, LANES), F32)], axis=0)
    rb = jnp.concatenate([jnp.where(half[None, :] == 1, sin, 0.0), jnp.zeros((tm, LANES), F32)], axis=0)
    return rc, ra, rb


def _in_proj(xa, mod_l, g1, w_in_l, rope, *, tm, seq, n_lat, nb):
    nt, d = xa.shape
    in_cols = w_in_l.shape[1]
    aw = d // 2
    n_rest = in_cols - 3 * aw
    n_lat_tiles = n_lat // tm
    seq_tiles = seq // tm

    def rope_idx(i, j):
        return (jnp.where(i < n_lat_tiles, i % seq_tiles, seq_tiles), 0)

    kern = functools.partial(_in_kernel, tm=tm, seq=seq, n_lat=n_lat, nb=nb, d=d, aw=aw)
    return pl.pallas_call(
        kern,
        grid=(nt // tm, in_cols // aw),
        in_specs=[
            pl.BlockSpec((tm, d), lambda i, j: (i, 0)),
            pl.BlockSpec((MOD_ROWS, 6 * d), lambda i, j: (0, 0)),
            pl.BlockSpec((1, d), lambda i, j: (0, 0)),
            pl.BlockSpec((d, aw), lambda i, j: (0, j)),
            pl.BlockSpec((tm, LANES), rope_idx),
            pl.BlockSpec((tm, LANES), rope_idx),
            pl.BlockSpec((tm, LANES), rope_idx),
        ],
        out_specs=[
            pl.BlockSpec((tm, aw), lambda i, j: (i, jnp.minimum(j, 2))),
            pl.BlockSpec((tm, aw), lambda i, j: (i, jnp.maximum(j - 3, 0))),
        ],
        out_shape=[
            jax.ShapeDtypeStruct((nt, 3 * aw), BF16),
            jax.ShapeDtypeStruct((nt, n_rest), F32),
        ],
        scratch_shapes=[pltpu.VMEM((tm, d), BF16)],
        compiler_params=_params("arbitrary", "arbitrary"),
        name="in_proj",
    )(xa, mod_l, g1.reshape(1, d), w_in_l, *rope)


def _attn_body(lam_ref, q_ref, k_refs, v_refs, g_ref, o_ref):
    lam = lam_ref[0]
    post = lam_ref[1]
    q = q_ref[...]
    lane = lax.broadcasted_iota(jnp.int32, q.shape, 1)
    zero = jnp.zeros_like(q)
    q_maps = [jnp.where(lane < HEAD_DIM, q, zero), jnp.where(lane >= HEAD_DIM, q, zero)]
    probs = []
    for qm in q_maps:
        ss = [lax.dot_general(qm, k[...], NT_DIMS, preferred_element_type=F32) for k in k_refs]
        mx = functools.reduce(jnp.maximum, [jnp.max(s, axis=-1, keepdims=True) for s in ss])
        es = [jnp.exp(s - mx) for s in ss]
        den = functools.reduce(jnp.add, [jnp.sum(e, axis=-1, keepdims=True) for e in es])
        probs.append((es, den))
    r1 = 1.0 / probs[0][1]
    r2 = lam / probs[1][1]
    o = None
    for si in range(len(k_refs)):
        a = (probs[0][0][si] * r1 - probs[1][0][si] * r2).astype(BF16)
        pv = jnp.dot(a, v_refs[si][...], preferred_element_type=F32)
        o = pv if o is None else o + pv
    o = _rms(o) * g_ref[...] * post
    o_ref[...] = o.astype(BF16)


def _attn_kernel(lam_ref, q_ref, kl_ref, kc_ref, vl_ref, vc_ref, g_ref, o_ref, *, qt):
    t = pl.program_id(2)

    @pl.when(t < qt)
    def _():
        _attn_body(lam_ref, q_ref, (kl_ref, kc_ref), (vl_ref, vc_ref), g_ref, o_ref)

    @pl.when(t >= qt)
    def _():
        _attn_body(lam_ref, q_ref, (kc_ref,), (vc_ref,), g_ref, o_ref)


def _attention(qkv, lam_vec, subln_g, *, tq, seq, ctx, n_lat, nb, aw):
    nt = qkv.shape[0]
    nh = aw // V_DIM
    kcol = aw // V_DIM
    vcol = 2 * aw // V_DIM
    ctx0 = n_lat // ctx
    qt = seq // tq
    ct = ctx // tq

    def q_idx(b, h, t):
        return (jnp.where(t < qt, b * qt + t, n_lat // tq + b * ct + (t - qt)), h)

    return pl.pallas_call(
        functools.partial(_attn_kernel, qt=qt),
        grid=(nb, nh, qt + ct),
        in_specs=[
            pl.BlockSpec(memory_space=pltpu.SMEM),
            pl.BlockSpec((tq, V_DIM), q_idx),
            pl.BlockSpec((seq, V_DIM), lambda b, h, t: (b, kcol + h)),
            pl.BlockSpec((ctx, V_DIM), lambda b, h, t: (ctx0 + b, kcol + h)),
            pl.BlockSpec((seq, V_DIM), lambda b, h, t: (b, vcol + h)),
            pl.BlockSpec((ctx, V_DIM), lambda b, h, t: (ctx0 + b, vcol + h)),
            pl.BlockSpec((1, V_DIM), lambda b, h, t: (0, 0)),
        ],
        out_specs=pl.BlockSpec((tq, V_DIM), q_idx),
        out_shape=jax.ShapeDtypeStruct((nt, aw), BF16),
        compiler_params=_params("arbitrary", "arbitrary", "arbitrary"),
        name="attention",
    )(lam_vec, qkv, qkv, qkv, qkv, qkv, subln_g.reshape(1, V_DIM))


def _top2_of4(a, b, c, d):
    m01, n01 = jnp.maximum(a, b), jnp.minimum(a, b)
    m23, n23 = jnp.maximum(c, d), jnp.minimum(c, d)
    return jnp.maximum(m01, m23) + jnp.maximum(jnp.minimum(m01, m23), jnp.maximum(n01, n23))


def _route(logits_t, bias_ref):
    n_exp = logits_t.shape[0]
    per = n_exp // N_GROUPS
    s = [jax.nn.sigmoid(logits_t[e:e + 1, :]) for e in range(n_exp)]
    sb = [s[e] + bias_ref[e:e + 1, :] for e in range(n_exp)]
    gscore = [_top2_of4(*sb[g * per:(g + 1) * per]) for g in range(N_GROUPS)]
    best, gidx = gscore[0], jnp.zeros_like(gscore[0], dtype=jnp.int32)
    for g in range(1, N_GROUPS):
        better = gscore[g] > best
        gidx = jnp.where(better, g, gidx)
        best = jnp.where(better, gscore[g], best)
    cand_b, cand_s = [], []
    for jx in range(per):
        vb, vs = sb[jx], s[jx]
        for g in range(1, N_GROUPS):
            sel = gidx == g
            vb = jnp.where(sel, sb[g * per + jx], vb)
            vs = jnp.where(sel, s[g * per + jx], vs)
        cand_b.append(vb)
        cand_s.append(vs)

    def argmax_first(vals, exclude):
        bv = bi = bs = None
        for jx in range(per):
            v = vals[jx] if exclude is None else jnp.where(exclude == jx, -jnp.inf, vals[jx])
            if bv is None:
                bv, bi, bs = v, jnp.zeros_like(gidx), cand_s[jx]
            else:
                better = v > bv
                bi = jnp.where(better, jx, bi)
                bs = jnp.where(better, cand_s[jx], bs)
                bv = jnp.where(better, v, bv)
        return bi, bs

    j1, w1 = argmax_first(cand_b, None)
    j2, w2 = argmax_first(cand_b, j1)
    tot = w1 + w2
    return gidx * per + j1, gidx * per + j2, w1 / tot, w2 / tot


def _merge_kernel(x_ref, attn_ref, rest_ref, ccp_ref, cxp_ref, ccn_ref, cxn_ref, mod_ref,
                  convw_ref, convb_ref, wab_ref, wcb_ref, wo_ref, g2_ref, wrt_ref, rbias_ref,
                  xo_ref, h2_ref, eidx_ref, ew_ref, *, tm, seq, ctx, n_lat, nb, d, cw):
    i = pl.program_id(0)
    r0 = i * tm
    is_lat = r0 < n_lat
    seg = jnp.where(is_lat, r0 // seq, nb)
    pos = jnp.where(is_lat, r0 % seq, (r0 - n_lat) % ctx)
    slen = jnp.where(is_lat, seq, ctx)
    has_prev = (pos > 0).astype(F32)
    has_next = (pos + tm < slen).astype(F32)

    cb = rest_ref[:, 0:cw]
    u = rest_ref[:, cw:2 * cw] * rest_ref[:, 2 * cw:3 * cw]
    ga = rest_ref[:, 3 * cw:3 * cw + d]
    gc = rest_ref[:, 3 * cw + d:3 * cw + 2 * d]
    halo_prev = ccp_ref[SUBLANES - 1:SUBLANES, :] * cxp_ref[SUBLANES - 1:SUBLANES, :] * has_prev
    halo_next = ccn_ref[0:1, :] * cxn_ref[0:1, :] * has_next
    rid = lax.broadcasted_iota(jnp.int32, u.shape, 0)
    u_prev = jnp.where(rid == 0, halo_prev, pltpu.roll(u, 1, 0))
    u_next = jnp.where(rid == tm - 1, halo_next, pltpu.roll(u, tm - 1, 0))
    y = cb * (u_prev * convw_ref[0:1, :] + u * convw_ref[1:2, :] + u_next * convw_ref[2:3, :]
              + convb_ref[...])

    ma = jnp.dot(attn_ref[...], wab_ref[...], preferred_element_type=F32)
    mc = jnp.dot(y.astype(BF16), wcb_ref[...], preferred_element_type=F32)
    merged = jax.nn.sigmoid(ga) * ma + jax.nn.sigmoid(gc) * mc
    out = jnp.dot(merged.astype(BF16), wo_ref[...], preferred_element_type=F32)

    m = mod_ref[pl.ds(seg, 1), :]
    xn = x_ref[...] + m[:, 2 * d:3 * d] * out
    xo_ref[...] = xn
    h2 = _rms(xn) * g2_ref[...] * (1.0 + m[:, 4 * d:5 * d]) + m[:, 3 * d:4 * d]
    h2_ref[...] = h2

    logits_t = lax.dot_general(wrt_ref[...], h2, NT_DIMS, precision=HIGHEST,
                               preferred_element_type=F32)
    e1, e2, w1, w2 = _route(logits_t, rbias_ref)
    eidx_ref[...] = jnp.zeros(eidx_ref.shape, jnp.int32)
    ew_ref[...] = jnp.zeros(ew_ref.shape, F32)
    eidx_ref[0:1, :] = e1
    eidx_ref[1:2, :] = e2
    ew_ref[0:1, :] = w1
    ew_ref[1:2, :] = w2


def _merge(xa, attn, rest, mod_l, conv_w, conv_b, w_ab, w_cb, w_o, g2, w_router, router_bias,
           *, n_rows, tm, seq, ctx, n_lat, nb):
    nt, d = xa.shape
    cw = conv_w.shape[1]
    n_exp = w_router.shape[1]
    n_rest = rest.shape[1]
    hb = tm // SUBLANES
    last_hblk = nt // SUBLANES - 1

    def prev_idx(col):
        return lambda i: (jnp.maximum(i * hb - 1, 0), col)

    def next_idx(col):
        return lambda i: (jnp.minimum((i + 1) * hb, last_hblk), col)

    const = lambda i: (0, 0)
    kern = functools.partial(_merge_kernel, tm=tm, seq=seq, ctx=ctx, n_lat=n_lat, nb=nb, d=d, cw=cw)
    return pl.pallas_call(
        kern,
        grid=(n_rows // tm,),
        in_specs=[
            pl.BlockSpec((tm, d), lambda i: (i, 0)),
            pl.BlockSpec((tm, attn.shape[1]), lambda i: (i, 0)),
            pl.BlockSpec((tm, n_rest), lambda i: (i, 0)),
            pl.BlockSpec((SUBLANES, cw), prev_idx(1)),
            pl.BlockSpec((SUBLANES, cw), prev_idx(2)),
            pl.BlockSpec((SUBLANES, cw), next_idx(1)),
            pl.BlockSpec((SUBLANES, cw), next_idx(2)),
            pl.BlockSpec((MOD_ROWS, 6 * d), const),
            pl.BlockSpec((3, cw), const),
            pl.BlockSpec((1, cw), const),
            pl.BlockSpec(w_ab.shape, const),
            pl.BlockSpec(w_cb.shape, const),
            pl.BlockSpec(w_o.shape, const),
            pl.BlockSpec((1, d), const),
            pl.BlockSpec((n_exp, d), const),
            pl.BlockSpec((n_exp, 1), const),
        ],
        out_specs=[
            pl.BlockSpec((tm, d), lambda i: (i, 0)),
            pl.BlockSpec((tm, d), lambda i: (i, 0)),
            pl.BlockSpec((SUBLANES, tm), lambda i: (0, i)),
            pl.BlockSpec((SUBLANES, tm), lambda i: (0, i)),
        ],
        out_shape=[
            jax.ShapeDtypeStruct((n_rows, d), F32),
            jax.ShapeDtypeStruct((n_rows, d), F32),
            jax.ShapeDtypeStruct((SUBLANES, n_rows), jnp.int32),
            jax.ShapeDtypeStruct((SUBLANES, n_rows), F32),
        ],
        compiler_params=_params("arbitrary"),
        name="merge_router",
    )(xa, attn, rest, rest, rest, rest, rest, mod_l, conv_w, conv_b.reshape(1, cw), w_ab, w_cb, w_o,
      g2.reshape(1, d), w_router.T, router_bias.reshape(n_exp, 1))


def _dispatch_plan(eidx, ew, n_tok, tg, n_exp):
    ef = eidx[:2].reshape(-1)
    wf = ew[:2].reshape(-1)
    n2 = 2 * n_tok
    onehot = (ef[:, None] == jnp.arange(n_exp)[None, :]).astype(jnp.int32)
    csum = jnp.cumsum(onehot, axis=0)
    rank = jnp.take_along_axis(csum, ef[:, None], axis=1)[:, 0] - 1
    counts = csum[-1]
    padded = ((counts + tg - 1) // tg) * tg
    gend = jnp.cumsum(padded)
    gstart = gend - padded
    dest = gstart[ef] + rank
    n_slots = -(-n2 // tg) * tg + n_exp * tg
    n_tiles = n_slots // tg
    tok_of_slot = jnp.zeros((n_slots,), jnp.int32).at[dest].set(jnp.arange(n2, dtype=jnp.int32) % n_tok)
    w_of_slot = jnp.zeros((n_slots,), F32).at[dest].set(wf)
    tile_expert = jnp.searchsorted(gend, jnp.arange(n_tiles, dtype=jnp.int32) * tg, side="right")
    tile_expert = jnp.minimum(tile_expert, n_exp - 1).astype(jnp.int32)
    n_active = (gend[-1] // tg).astype(jnp.int32).reshape(1)
    slots = dest.reshape(2, n_tok).astype(jnp.int32)
    return tok_of_slot, w_of_slot.reshape(n_slots, 1), tile_expert, n_active, slots


def _expert_kernel(te_ref, tok_ref, na_ref, h2_hbm, wg_ref, wu_ref, wd_ref, ws_ref, y_ref, xbuf, sem,
                   *, tg):
    del te_ref
    t = pl.program_id(0)

    @pl.when(t < na_ref[0])
    def _():
        base = t * tg

        def issue(r, carry):
            tok = tok_ref[base + r]
            pltpu.make_async_copy(h2_hbm.at[pl.ds(tok, 1), :], xbuf.at[pl.ds(r, 1), :], sem).start()
            return carry

        lax.fori_loop(0, tg, issue, 0)
        pltpu.make_async_copy(h2_hbm.at[pl.ds(0, tg), :], xbuf, sem).wait()
        hx = xbuf[...].astype(BF16)
        gte = jnp.dot(hx, wg_ref[0], preferred_element_type=F32)
        up = jnp.dot(hx, wu_ref[0], preferred_element_type=F32)
        he = (gte * jax.nn.sigmoid(gte) * up).astype(BF16)
        y = jnp.dot(he, wd_ref[0], preferred_element_type=F32)
        y_ref[...] = y * ws_ref[...]

    @pl.when(t >= na_ref[0])
    def _():
        y_ref[...] = jnp.zeros(y_ref.shape, F32)


def _experts(h2, plan, wg, wu, wd, *, tg):
    tok_of_slot, w_of_slot, tile_expert, n_active, _ = plan
    n_slots = tok_of_slot.shape[0]
    d = h2.shape[1]
    f = wg.shape[2]
    grid_spec = pltpu.PrefetchScalarGridSpec(
        num_scalar_prefetch=3,
        grid=(n_slots // tg,),
        in_specs=[
            pl.BlockSpec(memory_space=pl.ANY),
            pl.BlockSpec((1, d, f), lambda t, te, tok, na: (te[t], 0, 0)),
            pl.BlockSpec((1, d, f), lambda t, te, tok, na: (te[t], 0, 0)),
            pl.BlockSpec((1, f, d), lambda t, te, tok, na: (te[t], 0, 0)),
            pl.BlockSpec((tg, 1), lambda t, te, tok, na: (t, 0)),
        ],
        out_specs=pl.BlockSpec((tg, d), lambda t, te, tok, na: (t, 0)),
        scratch_shapes=[pltpu.VMEM((tg, d), F32), pltpu.SemaphoreType.DMA(())],
    )
    return pl.pallas_call(
        functools.partial(_expert_kernel, tg=tg),
        grid_spec=grid_spec,
        out_shape=jax.ShapeDtypeStruct((n_slots, d), F32),
        compiler_params=_params("arbitrary"),
        name="experts",
    )(tile_expert, tok_of_slot, n_active, h2, wg, wu, wd, w_of_slot)


def _combine_kernel(s1_ref, s2_ref, x_ref, y_hbm, mod_ref, fg_ref, xo_ref, buf, sem,
                    *, tm, seq, n_lat, nb, d, final):
    i = pl.program_id(0)
    base = i * tm

    def issue(r, carry):
        pltpu.make_async_copy(y_hbm.at[pl.ds(s1_ref[base + r], 1), :], buf.at[0, pl.ds(r, 1), :], sem).start()
        pltpu.make_async_copy(y_hbm.at[pl.ds(s2_ref[base + r], 1), :], buf.at[1, pl.ds(r, 1), :], sem).start()
        return carry

    lax.fori_loop(0, tm, issue, 0)
    pltpu.make_async_copy(y_hbm.at[pl.ds(0, tm), :], buf.at[0], sem).wait()
    pltpu.make_async_copy(y_hbm.at[pl.ds(0, tm), :], buf.at[1], sem).wait()
    seg = jnp.where(base < n_lat, base // seq, nb)
    m = mod_ref[pl.ds(seg, 1), :]
    xn = x_ref[...] + m[:, 5 * d:6 * d] * (buf[0] + buf[1])
    if final:
        xn = _rms(xn) * fg_ref[...]
    xo_ref[...] = xn


def _combine(xa, y_sorted, slots, mod_l, final_g, *, tm, seq, n_lat, nb, final):
    n_rows, d = xa.shape
    grid_spec = pltpu.PrefetchScalarGridSpec(
        num_scalar_prefetch=2,
        grid=(n_rows // tm,),
        in_specs=[
            pl.BlockSpec((tm, d), lambda i, s1, s2: (i, 0)),
            pl.BlockSpec(memory_space=pl.ANY),
            pl.BlockSpec((MOD_ROWS, 6 * d), lambda i, s1, s2: (0, 0)),
            pl.BlockSpec((1, d), lambda i, s1, s2: (0, 0)),
        ],
        out_specs=pl.BlockSpec((tm, d), lambda i, s1, s2: (i, 0)),
        scratch_shapes=[pltpu.VMEM((2, tm, d), F32), pltpu.SemaphoreType.DMA(())],
    )
    kern = functools.partial(_combine_kernel, tm=tm, seq=seq, n_lat=n_lat, nb=nb, d=d, final=final)
    return pl.pallas_call(
        kern,
        grid_spec=grid_spec,
        out_shape=jax.ShapeDtypeStruct((n_rows, d), F32),
        compiler_params=_params("arbitrary"),
        name="combine",
    )(slots[0], slots[1], xa, y_sorted, mod_l, final_g.reshape(1, d))


def _tile(limit, *sizes):
    t = limit
    while any(s % t for s in sizes):
        t //= 2
    return t


def kernel(x, c, ctx, c_ctx, w_mod, b_mod, norm1_g, norm2_g, w_in, lambda_q1, lambda_k1, lambda_q2,
           lambda_k2, subln_g, conv_w, conv_b, w_attn_branch, w_conv_branch, w_out, w_router,
           router_bias, w_exp_gate, w_exp_up, w_exp_down, final_g):
    nb, seq, d = x.shape
    n_ctx_tok = ctx.shape[1]
    depth = w_mod.shape[0]
    n_exp = w_router.shape[1]
    aw = d // 2
    n_lat = nb * seq
    nt = n_lat + nb * n_ctx_tok
    assert nb < MOD_ROWS and seq % GRID_W == 0 and aw % V_DIM == 0 and n_exp % N_GROUPS == 0

    tm_in = _tile(512, seq, nb * n_ctx_tok)
    tm_mg = _tile(256, seq, n_ctx_tok)
    tq = _tile(256, seq, n_ctx_tok)
    tg = _tile(256, tm_mg)

    xa = jnp.concatenate([x.reshape(n_lat, d), ctx.reshape(nb * n_ctx_tok, d)], axis=0)
    cvec = jnp.zeros((MOD_ROWS, d), F32).at[:nb].set(c).at[nb].set(c_ctx)
    mod = _modulation(cvec, w_mod, b_mod)
    rope = _rope_tables(seq, tm_in)

    for l in range(depth):
        last = l == depth - 1
        lam_init = 0.8 - 0.6 * math.exp(-0.3 * l)
        lam = (jnp.exp(jnp.sum(lambda_q1[l] * lambda_k1[l])) - jnp.exp(jnp.sum(lambda_q2[l] * lambda_k2[l]))
               + lam_init)
        lam_vec = jnp.stack([lam, jnp.asarray(1.0 - lam_init, F32)]).astype(F32)

        qkv, rest = _in_proj(xa, mod[l], norm1_g[l], w_in[l].astype(BF16), rope,
                             tm=tm_in, seq=seq, n_lat=n_lat, nb=nb)
        attn = _attention(qkv, lam_vec, subln_g[l], tq=tq, seq=seq, ctx=n_ctx_tok, n_lat=n_lat, nb=nb, aw=aw)
        n_rows = n_lat if last else nt
        xa, h2, eidx, ew = _merge(
            xa, attn, rest, mod[l], conv_w[l], conv_b[l], w_attn_branch[l].astype(BF16),
            w_conv_branch[l].astype(BF16), w_out[l].astype(BF16), norm2_g[l], w_router, router_bias,
            n_rows=n_rows, tm=tm_mg, seq=seq, ctx=n_ctx_tok, n_lat=n_lat, nb=nb)
        plan = _dispatch_plan(eidx, ew, n_rows, tg, n_exp)
        y_sorted = _experts(h2, plan, w_exp_gate[l].astype(BF16), w_exp_up[l].astype(BF16),
                            w_exp_down[l].astype(BF16), tg=tg)
        xa = _combine(xa, y_sorted, plan[4], mod[l], final_g, tm=tm_mg, seq=seq, n_lat=n_lat, nb=nb,
                      final=last)

    return xa.reshape(nb, seq, d)
```

```python
import functools
import math

import jax
import jax.numpy as jnp
from jax import lax
from jax.experimental import pallas as pl
from jax.experimental.pallas import tpu as pltpu

HEAD_DIM = 64
V_DIM = 2 * HEAD_DIM
GRID_W = 64
ROPE_THETA = 10000.0
ROPE_AXIS_DIM = HEAD_DIM // 2
N_GROUPS = 4
EPS = 1e-6
LANES = 128
SUBLANES = 8
VMEM_LIMIT_BYTES = 56 * 1024 * 1024
MOD_ROWS = 8
LOG2E = math.log2(math.e)

F32 = jnp.float32
BF16 = jnp.bfloat16
HIGHEST = lax.Precision.HIGHEST
NT_DIMS = (((1,), (1,)), ((), ()))


def _params(*sem):
    return pltpu.CompilerParams(dimension_semantics=sem, vmem_limit_bytes=VMEM_LIMIT_BYTES)


def _rms(x):
    return x * lax.rsqrt(jnp.mean(x * x, axis=-1, keepdims=True) + EPS)


def _segment(row0, seq, n_lat, nb):
    return jnp.where(row0 < n_lat, row0 // seq, nb)


def _adaln(x, g_ref, mod_ref, seg, shift_col, d):
    m = mod_ref[pl.ds(seg, 1), :]
    return _rms(x) * g_ref[...] * (1.0 + m[:, (shift_col + 1) * d:(shift_col + 2) * d]) + m[:, shift_col * d:(shift_col + 1) * d]


def _mod_kernel(c_ref, w_ref, b_ref, o_ref):
    c = c_ref[...]
    sc = c * jax.nn.sigmoid(c)
    o_ref[0] = jnp.dot(sc, w_ref[0], precision=HIGHEST, preferred_element_type=F32) + b_ref[0]


def _modulation(cvec, w_mod, b_mod):
    depth, d, n6 = w_mod.shape
    tn = _tile(1024, n6)
    return pl.pallas_call(
        _mod_kernel,
        grid=(depth, n6 // tn),
        in_specs=[
            pl.BlockSpec((MOD_ROWS, d), lambda l, j: (0, 0)),
            pl.BlockSpec((1, d, tn), lambda l, j: (l, 0, j)),
            pl.BlockSpec((1, 1, tn), lambda l, j: (l, 0, j)),
        ],
        out_specs=pl.BlockSpec((1, MOD_ROWS, tn), lambda l, j: (l, 0, j)),
        out_shape=jax.ShapeDtypeStruct((depth, MOD_ROWS, n6), F32),
        compiler_params=_params("arbitrary", "arbitrary"),
        name="modulation",
    )(cvec, w_mod, b_mod.reshape(depth, 1, n6))


def _prenorm_kernel(x_ref, mod_ref, g_ref, h_ref, *, tm, seq, n_lat, nb, d):
    seg = _segment(pl.program_id(0) * tm, seq, n_lat, nb)
    h_ref[...] = _adaln(x_ref[...], g_ref, mod_ref, seg, 0, d).astype(BF16)


def _prenorm(xa, mod_l, g1, *, tm, seq, n_lat, nb):
    nt, d = xa.shape
    return pl.pallas_call(
        functools.partial(_prenorm_kernel, tm=tm, seq=seq, n_lat=n_lat, nb=nb, d=d),
        grid=(nt // tm,),
        in_specs=[
            pl.BlockSpec((tm, d), lambda i: (i, 0)),
            pl.BlockSpec((MOD_ROWS, 6 * d), lambda i: (0, 0)),
            pl.BlockSpec((1, d), lambda i: (0, 0)),
        ],
        out_specs=pl.BlockSpec((tm, d), lambda i: (i, 0)),
        out_shape=jax.ShapeDtypeStruct((nt, d), BF16),
        compiler_params=_params("arbitrary"),
        name="prenorm",
    )(xa, mod_l, g1.reshape(1, d))


def _in_kernel(h_ref, w_ref, *refs, rope, aw):
    o_ref, w_scr = refs[-2], refs[-1]
    j = pl.program_id(0)

    @pl.when(pl.program_id(1) == 0)
    def _():
        w_scr[...] = w_ref[0].astype(BF16)

    acc = jnp.dot(h_ref[...], w_scr[...], preferred_element_type=F32)
    if not rope:
        o_ref[...] = acc
        return
    rc_ref, ra_ref, rb_ref = refs[:3]

    @pl.when(j < 2)
    def _():
        scale = jnp.where(j == 0, HEAD_DIM ** -0.5 * LOG2E, 1.0).astype(F32)
        rc, ra, rb = rc_ref[...], ra_ref[...], rb_ref[...]
        for c in range(aw // LANES):
            a = acc[:, c * LANES:(c + 1) * LANES]
            r = (a * rc + pltpu.roll(a, LANES - ROPE_AXIS_DIM // 2, 1) * ra
                 + pltpu.roll(a, ROPE_AXIS_DIM // 2, 1) * rb)
            o_ref[:, c * LANES:(c + 1) * LANES] = (r * scale).astype(BF16)

    @pl.when(j == 2)
    def _():
        o_ref[...] = acc.astype(BF16)


def _rope_tables(seq, tm):
    pos = jnp.arange(seq)
    row = (pos // GRID_W).astype(F32)
    col = (pos % GRID_W).astype(F32)
    inv = ROPE_THETA ** (-jnp.arange(0, ROPE_AXIS_DIM, 2, dtype=F32) / ROPE_AXIS_DIM)
    lane = jnp.arange(LANES)
    jj = lane % HEAD_DIM
    axis = jj // ROPE_AXIS_DIM
    r = jj % ROPE_AXIS_DIM
    f = r % (ROPE_AXIS_DIM // 2)
    half = r // (ROPE_AXIS_DIM // 2)
    posv = jnp.where(axis[None, :] == 0, row[:, None], col[:, None])
    ang = posv * inv[f][None, :]
    cos, sin = jnp.cos(ang), jnp.sin(ang)
    rc = jnp.concatenate([cos, jnp.ones((tm, LANES), F32)], axis=0)
    ra = jnp.concatenate([jnp.where(half[None, :] == 0, -sin, 0.0), jnp.zeros((tm, LANES), F32)], axis=0)
    rb = jnp.concatenate([jnp.where(half[None, :] == 1, sin, 0.0), jnp.zeros((tm, LANES), F32)], axis=0)
    return rc, ra, rb


def _in_proj(h, w_in, layer, rope, *, tm, seq, n_lat):
    nt, d = h.shape
    in_cols = w_in.shape[2]
    aw = d // 2
    n_qkv = 3
    n_rest = in_cols // aw - n_qkv
    n_lat_tiles = n_lat // tm
    seq_tiles = seq // tm

    def rope_idx(j, i):
        return (jnp.where(i < n_lat_tiles, i % seq_tiles, seq_tiles), 0)

    def call(col0, n_col, out_dtype, with_rope):
        in_specs = [
            pl.BlockSpec((tm, d), lambda j, i: (i, 0)),
            pl.BlockSpec((1, d, aw), lambda j, i: (layer, 0, col0 + j)),
        ]
        args = [h, w_in]
        if with_rope:
            in_specs += [pl.BlockSpec((tm, LANES), rope_idx)] * 3
            args += list(rope)
        return pl.pallas_call(
            functools.partial(_in_kernel, rope=with_rope, aw=aw),
            grid=(n_col, nt // tm),
            in_specs=in_specs,
            out_specs=pl.BlockSpec((tm, aw), lambda j, i: (i, j)),
            out_shape=jax.ShapeDtypeStruct((nt, n_col * aw), out_dtype),
            scratch_shapes=[pltpu.VMEM((d, aw), BF16)],
            compiler_params=_params("arbitrary", "arbitrary"),
            name="in_proj_qkv" if with_rope else "in_proj_rest",
        )(*args)

    return call(0, n_qkv, BF16, True), call(n_qkv, n_rest, F32, False)


EXP_CHUNK = 128
ATTN_HALF = 256


def _diff_attention(lam_ref, q_ref, k_ref, v_ref, g_ref, o_ref, s_refs, e_refs, k0, n, half):
    n_half = q_ref.shape[0] // half
    q_maps = []
    for hh in range(n_half):
        q = q_ref[hh * half:(hh + 1) * half, :]
        lane = lax.broadcasted_iota(jnp.int32, q.shape, 1)
        zero = jnp.zeros_like(q)
        q_maps += [jnp.where(lane < HEAD_DIM, q, zero), jnp.where(lane >= HEAD_DIM, q, zero)]
    n_chain = len(q_maps)
    outs = [None] * n_chain

    def scores(i):
        s_refs[i][:, 0:n] = lax.dot_general(q_maps[i], k_ref[k0:k0 + n, :], NT_DIMS,
                                            preferred_element_type=F32)

    def exps(i):
        mx = jnp.max(s_refs[i][:, 0:n], axis=-1, keepdims=True)
        for c in range(n // EXP_CHUNK):
            sl = slice(c * EXP_CHUNK, (c + 1) * EXP_CHUNK)
            e_refs[i][:, sl] = jnp.exp2(s_refs[i][:, sl] - mx).astype(BF16)

    def values(i):
        outs[i] = jnp.dot(e_refs[i][:, 0:n], v_ref[k0:k0 + n, :], preferred_element_type=F32)

    scores(0)
    for i in range(n_chain):
        if i + 1 < n_chain:
            scores(i + 1)
        exps(i)
        if i >= 1:
            values(i - 1)
    values(n_chain - 1)

    for hh in range(n_half):
        a, b = outs[2 * hh], outs[2 * hh + 1]
        o1 = a[:, 0:V_DIM] / a[:, V_DIM:V_DIM + 1]
        o2 = b[:, 0:V_DIM] * (lam_ref[0] / b[:, V_DIM:V_DIM + 1])
        o = _rms(o1 - o2) * g_ref[...] * lam_ref[1]
        o_ref[hh * half:(hh + 1) * half, :] = o.astype(BF16)


def _attn_lat_kernel(lam_ref, q_ref, kl_ref, kc_ref, vl_ref, vc_ref, g_ref, o_ref, k_scr, v_scr, *chain_scr,
                     seq, ctx, half):
    nk = seq + ctx

    @pl.when(pl.program_id(2) == 0)
    def _():
        k_scr[0:seq, :] = kl_ref[...]
        k_scr[seq:nk, :] = kc_ref[...]
        v_scr[0:seq, 0:V_DIM] = vl_ref[...]
        v_scr[seq:nk, 0:V_DIM] = vc_ref[...]
        v_scr[:, V_DIM:2 * V_DIM] = jnp.ones((nk, V_DIM), BF16)

    n_chain = len(chain_scr) // 2
    _diff_attention(lam_ref, q_ref, k_scr, v_scr, g_ref, o_ref, chain_scr[:n_chain], chain_scr[n_chain:],
                    0, nk, half)


def _attn_ctx_kernel(lam_ref, q_ref, k_ref, v_ref, g_ref, o_ref, v_scr, *chain_scr, ctx, half):
    v_scr[:, 0:V_DIM] = v_ref[...]
    v_scr[:, V_DIM:2 * V_DIM] = jnp.ones((ctx, V_DIM), BF16)
    n_chain = len(chain_scr) // 2
    _diff_attention(lam_ref, q_ref, k_ref, v_scr, g_ref, o_ref, chain_scr[:n_chain], chain_scr[n_chain:],
                    0, ctx, half)


def _attention(qkv, lam_vec, subln_g, *, tq, seq, ctx, n_lat, nb, aw):
    nh = aw // V_DIM
    kcol = aw // V_DIM
    vcol = 2 * aw // V_DIM
    ctx0 = n_lat // ctx
    qt = seq // tq
    nk = seq + ctx
    g = subln_g.reshape(1, V_DIM)
    smem = pl.BlockSpec(memory_space=pltpu.SMEM)

    def chain_scratch(rows, half, n):
        n_chain = 2 * (rows // half)
        return ([pltpu.VMEM((half, n), F32)] * n_chain) + ([pltpu.VMEM((half, n), BF16)] * n_chain)

    half = _tile(ATTN_HALF, tq)
    lat = pl.pallas_call(
        functools.partial(_attn_lat_kernel, seq=seq, ctx=ctx, half=half),
        grid=(nb, nh, qt),
        in_specs=[
            smem,
            pl.BlockSpec((tq, V_DIM), lambda b, h, t: (b * qt + t, h)),
            pl.BlockSpec((seq, V_DIM), lambda b, h, t: (b, kcol + h)),
            pl.BlockSpec((ctx, V_DIM), lambda b, h, t: (ctx0 + b, kcol + h)),
            pl.BlockSpec((seq, V_DIM), lambda b, h, t: (b, vcol + h)),
            pl.BlockSpec((ctx, V_DIM), lambda b, h, t: (ctx0 + b, vcol + h)),
            pl.BlockSpec((1, V_DIM), lambda b, h, t: (0, 0)),
        ],
        out_specs=pl.BlockSpec((tq, V_DIM), lambda b, h, t: (b * qt + t, h)),
        out_shape=jax.ShapeDtypeStruct((n_lat, aw), BF16),
        scratch_shapes=[pltpu.VMEM((nk, V_DIM), BF16), pltpu.VMEM((nk, 2 * V_DIM), BF16)]
        + chain_scratch(tq, half, nk),
        compiler_params=_params("arbitrary", "arbitrary", "arbitrary"),
        name="attn_latent",
    )(lam_vec, qkv, qkv, qkv, qkv, qkv, g)

    half_c = _tile(ATTN_HALF, ctx)
    cx = pl.pallas_call(
        functools.partial(_attn_ctx_kernel, ctx=ctx, half=half_c),
        grid=(nb, nh),
        in_specs=[
            smem,
            pl.BlockSpec((ctx, V_DIM), lambda b, h: (ctx0 + b, h)),
            pl.BlockSpec((ctx, V_DIM), lambda b, h: (ctx0 + b, kcol + h)),
            pl.BlockSpec((ctx, V_DIM), lambda b, h: (ctx0 + b, vcol + h)),
            pl.BlockSpec((1, V_DIM), lambda b, h: (0, 0)),
        ],
        out_specs=pl.BlockSpec((ctx, V_DIM), lambda b, h: (b, h)),
        out_shape=jax.ShapeDtypeStruct((nb * ctx, aw), BF16),
        scratch_shapes=[pltpu.VMEM((ctx, 2 * V_DIM), BF16)] + chain_scratch(ctx, half_c, ctx),
        compiler_params=_params("arbitrary", "arbitrary"),
        name="attn_context",
    )(lam_vec, qkv, qkv, qkv, g)
    return lat, cx


def _top2_of4(a, b, c, d):
    m01, n01 = jnp.maximum(a, b), jnp.minimum(a, b)
    m23, n23 = jnp.maximum(c, d), jnp.minimum(c, d)
    return jnp.maximum(m01, m23) + jnp.maximum(jnp.minimum(m01, m23), jnp.maximum(n01, n23))


def _route(logits_t, bias_ref):
    n_exp = logits_t.shape[0]
    per = n_exp // N_GROUPS
    s = [jax.nn.sigmoid(logits_t[e:e + 1, :]) for e in range(n_exp)]
    sb = [s[e] + bias_ref[e:e + 1, :] for e in range(n_exp)]
    gscore = [_top2_of4(*sb[g * per:(g + 1) * per]) for g in range(N_GROUPS)]
    best, gidx = gscore[0], jnp.zeros_like(gscore[0], dtype=jnp.int32)
    for g in range(1, N_GROUPS):
        better = gscore[g] > best
        gidx = jnp.where(better, g, gidx)
        best = jnp.where(better, gscore[g], best)
    cand_b, cand_s = [], []
    for jx in range(per):
        vb, vs = sb[jx], s[jx]
        for g in range(1, N_GROUPS):
            sel = gidx == g
            vb = jnp.where(sel, sb[g * per + jx], vb)
            vs = jnp.where(sel, s[g * per + jx], vs)
        cand_b.append(vb)
        cand_s.append(vs)

    def argmax_first(vals, exclude):
        bv = bi = bs = None
        for jx in range(per):
            v = vals[jx] if exclude is None else jnp.where(exclude == jx, -jnp.inf, vals[jx])
            if bv is None:
                bv, bi, bs = v, jnp.zeros_like(gidx), cand_s[jx]
            else:
                better = v > bv
                bi = jnp.where(better, jx, bi)
                bs = jnp.where(better, cand_s[jx], bs)
                bv = jnp.where(better, v, bv)
        return bi, bs

    j1, w1 = argmax_first(cand_b, None)
    j2, w2 = argmax_first(cand_b, j1)
    tot = w1 + w2
    return gidx * per + j1, gidx * per + j2, w1 / tot, w2 / tot


def _merge_kernel(x_ref, attn_lat_ref, attn_ctx_ref, rest_ref, ccp_ref, cxp_ref, ccn_ref, cxn_ref, mod_ref,
                  convw_ref, convb_ref, wab_ref, wcb_ref, wo_ref, g2_ref, wrt_ref, rbias_ref,
                  xo_ref, h2_ref, eidx_ref, ew_ref, *, tm, seq, ctx, n_lat, nb, d, cw):
    i = pl.program_id(0)
    r0 = i * tm
    is_lat = r0 < n_lat
    seg = _segment(r0, seq, n_lat, nb)
    pos = jnp.where(is_lat, r0 % seq, (r0 - n_lat) % ctx)
    slen = jnp.where(is_lat, seq, ctx)
    has_prev = (pos > 0).astype(F32)
    has_next = (pos + tm < slen).astype(F32)

    cb = rest_ref[:, 0:cw]
    u = rest_ref[:, cw:2 * cw] * rest_ref[:, 2 * cw:3 * cw]
    ga = rest_ref[:, 3 * cw:3 * cw + d]
    gc = rest_ref[:, 3 * cw + d:3 * cw + 2 * d]
    halo_prev = ccp_ref[SUBLANES - 1:SUBLANES, :] * cxp_ref[SUBLANES - 1:SUBLANES, :] * has_prev
    halo_next = ccn_ref[0:1, :] * cxn_ref[0:1, :] * has_next
    rid = lax.broadcasted_iota(jnp.int32, u.shape, 0)
    u_prev = jnp.where(rid == 0, halo_prev, pltpu.roll(u, 1, 0))
    u_next = jnp.where(rid == tm - 1, halo_next, pltpu.roll(u, tm - 1, 0))
    y = cb * (u_prev * convw_ref[0:1, :] + u * convw_ref[1:2, :] + u_next * convw_ref[2:3, :]
              + convb_ref[...])

    attn = jnp.where(is_lat, attn_lat_ref[...], attn_ctx_ref[...])
    ma = jnp.dot(attn, wab_ref[...], preferred_element_type=F32)
    mc = jnp.dot(y.astype(BF16), wcb_ref[...], preferred_element_type=F32)
    merged = jax.nn.sigmoid(ga) * ma + jax.nn.sigmoid(gc) * mc
    out = jnp.dot(merged.astype(BF16), wo_ref[...], preferred_element_type=F32)

    m = mod_ref[pl.ds(seg, 1), :]
    xn = x_ref[...] + m[:, 2 * d:3 * d] * out
    xo_ref[...] = xn
    h2 = _adaln(xn, g2_ref, mod_ref, seg, 3, d)
    h2_ref[...] = h2

    logits_t = lax.dot_general(wrt_ref[...], h2, NT_DIMS, precision=HIGHEST,
                               preferred_element_type=F32)
    e1, e2, w1, w2 = _route(logits_t, rbias_ref)
    eidx_ref[...] = jnp.zeros(eidx_ref.shape, jnp.int32)
    ew_ref[...] = jnp.zeros(ew_ref.shape, F32)
    eidx_ref[0:1, :] = e1
    eidx_ref[1:2, :] = e2
    ew_ref[0:1, :] = w1
    ew_ref[1:2, :] = w2


def _merge(xa, attn_lat, attn_ctx, rest, mod_l, conv_w, conv_b, w_ab, w_cb, w_o, g2, w_router, router_bias,
           *, n_rows, tm, seq, ctx, n_lat, nb):
    nt, d = xa.shape
    aw = attn_lat.shape[1]
    lat_tiles = n_lat // tm
    ctx_tiles = attn_ctx.shape[0] // tm
    cw = conv_w.shape[1]
    n_exp = w_router.shape[1]
    n_rest = rest.shape[1]
    hb = tm // SUBLANES
    last_hblk = nt // SUBLANES - 1

    def prev_idx(col):
        return lambda i: (jnp.maximum(i * hb - 1, 0), col)

    def next_idx(col):
        return lambda i: (jnp.minimum((i + 1) * hb, last_hblk), col)

    const = lambda i: (0, 0)
    kern = functools.partial(_merge_kernel, tm=tm, seq=seq, ctx=ctx, n_lat=n_lat, nb=nb, d=d, cw=cw)
    return pl.pallas_call(
        kern,
        grid=(n_rows // tm,),
        in_specs=[
            pl.BlockSpec((tm, d), lambda i: (i, 0)),
            pl.BlockSpec((tm, aw), lambda i: (jnp.minimum(i, lat_tiles - 1), 0)),
            pl.BlockSpec((tm, aw), lambda i: (jnp.clip(i - lat_tiles, 0, ctx_tiles - 1), 0)),
            pl.BlockSpec((tm, n_rest), lambda i: (i, 0)),
            pl.BlockSpec((SUBLANES, cw), prev_idx(1)),
            pl.BlockSpec((SUBLANES, cw), prev_idx(2)),
            pl.BlockSpec((SUBLANES, cw), next_idx(1)),
            pl.BlockSpec((SUBLANES, cw), next_idx(2)),
            pl.BlockSpec((MOD_ROWS, 6 * d), const),
            pl.BlockSpec((3, cw), const),
            pl.BlockSpec((1, cw), const),
            pl.BlockSpec(w_ab.shape, const),
            pl.BlockSpec(w_cb.shape, const),
            pl.BlockSpec(w_o.shape, const),
            pl.BlockSpec((1, d), const),
            pl.BlockSpec((n_exp, d), const),
            pl.BlockSpec((n_exp, 1), const),
        ],
        out_specs=[
            pl.BlockSpec((tm, d), lambda i: (i, 0)),
            pl.BlockSpec((tm, d), lambda i: (i, 0)),
            pl.BlockSpec((SUBLANES, tm), lambda i: (0, i)),
            pl.BlockSpec((SUBLANES, tm), lambda i: (0, i)),
        ],
        out_shape=[
            jax.ShapeDtypeStruct((n_rows, d), F32),
            jax.ShapeDtypeStruct((n_rows, d), F32),
            jax.ShapeDtypeStruct((SUBLANES, n_rows), jnp.int32),
            jax.ShapeDtypeStruct((SUBLANES, n_rows), F32),
        ],
        compiler_params=_params("arbitrary"),
        name="merge_router",
    )(xa, attn_lat, attn_ctx, rest, rest, rest, rest, rest, mod_l, conv_w, conv_b.reshape(1, cw), w_ab, w_cb, w_o,
      g2.reshape(1, d), w_router.T, router_bias.reshape(n_exp, 1))


def _dispatch_plan(eidx, ew, n_tok, tg, n_exp):
    ef = eidx[:2].reshape(-1)
    wf = ew[:2].reshape(-1)
    n2 = 2 * n_tok
    onehot = (ef[:, None] == jnp.arange(n_exp)[None, :]).astype(jnp.int32)
    csum = jnp.cumsum(onehot, axis=0)
    rank = jnp.take_along_axis(csum, ef[:, None], axis=1)[:, 0] - 1
    counts = csum[-1]
    padded = ((counts + tg - 1) // tg) * tg
    gend = jnp.cumsum(padded)
    gstart = gend - padded
    dest = gstart[ef] + rank
    n_slots = -(-n2 // tg) * tg + n_exp * tg
    n_tiles = n_slots // tg
    src_of_slot = jnp.full((n_slots,), -1, jnp.int32).at[dest].set(jnp.arange(n2, dtype=jnp.int32))
    filled = src_of_slot >= 0
    src = jnp.maximum(src_of_slot, 0)
    tok_of_slot = jnp.where(filled, src % n_tok, 0)
    w_of_slot = jnp.where(filled, wf[src], 0.0)
    tile_start = jnp.arange(n_tiles, dtype=jnp.int32) * tg
    tile_expert = jnp.sum((gend[None, :] <= tile_start[:, None]).astype(jnp.int32), axis=1)
    tile_expert = jnp.minimum(tile_expert, n_exp - 1).astype(jnp.int32)
    n_active = (gend[-1] // tg).astype(jnp.int32).reshape(1)
    slots = dest.reshape(2, n_tok).astype(jnp.int32)
    return tok_of_slot, w_of_slot.reshape(n_slots, 1), tile_expert, n_active, slots


def _expert_kernel(te_ref, tok_ref, na_ref, h2_hbm, wg_ref, wu_ref, wd_ref, ws_ref, y_ref, xbuf, sem,
                   *, tg):
    del te_ref
    t = pl.program_id(0)
    n_act = na_ref[0]
    slot = t % 2

    def issue(tile, dst_slot):
        base = tile * tg
        for r in range(tg):
            pltpu.make_async_copy(h2_hbm.at[pl.ds(tok_ref[base + r], 1), :],
                                  xbuf.at[dst_slot, pl.ds(r, 1), :], sem.at[dst_slot]).start()

    def wait(dst_slot):
        pltpu.make_async_copy(h2_hbm.at[pl.ds(0, tg), :], xbuf.at[dst_slot], sem.at[dst_slot]).wait()

    @pl.when(t == 0)
    def _():
        issue(0, 0)

    @pl.when(t < n_act)
    def _():
        wait(slot)
        issue(jnp.minimum(t + 1, n_act - 1), 1 - slot)
        hx = xbuf[slot].astype(BF16)
        gte = jnp.dot(hx, wg_ref[0], preferred_element_type=F32)
        up = jnp.dot(hx, wu_ref[0], preferred_element_type=F32)
        he = (gte * jax.nn.sigmoid(gte) * up).astype(BF16)
        y = jnp.dot(he, wd_ref[0], preferred_element_type=F32)
        y_ref[...] = y * ws_ref[...]

    @pl.when(t == n_act - 1)
    def _():
        wait(1 - slot)

    @pl.when(t >= n_act)
    def _():
        y_ref[...] = jnp.zeros(y_ref.shape, F32)


def _experts(h2, plan, wg, wu, wd, *, tg):
    tok_of_slot, w_of_slot, tile_expert, n_active, _ = plan
    n_slots = tok_of_slot.shape[0]
    d = h2.shape[1]
    f = wg.shape[2]
    grid_spec = pltpu.PrefetchScalarGridSpec(
        num_scalar_prefetch=3,
        grid=(n_slots // tg,),
        in_specs=[
            pl.BlockSpec(memory_space=pl.ANY),
            pl.BlockSpec((1, d, f), lambda t, te, tok, na: (te[t], 0, 0)),
            pl.BlockSpec((1, d, f), lambda t, te, tok, na: (te[t], 0, 0)),
            pl.BlockSpec((1, f, d), lambda t, te, tok, na: (te[t], 0, 0)),
            pl.BlockSpec((tg, 1), lambda t, te, tok, na: (t, 0)),
        ],
        out_specs=pl.BlockSpec((tg, d), lambda t, te, tok, na: (t, 0)),
        scratch_shapes=[pltpu.VMEM((2, tg, d), F32), pltpu.SemaphoreType.DMA((2,))],
    )
    return pl.pallas_call(
        functools.partial(_expert_kernel, tg=tg),
        grid_spec=grid_spec,
        out_shape=jax.ShapeDtypeStruct((n_slots, d), F32),
        compiler_params=_params("arbitrary"),
        name="experts",
    )(tile_expert, tok_of_slot, n_active, h2, wg, wu, wd, w_of_slot)


def _combine_kernel(s1_ref, s2_ref, x_ref, y_hbm, mod_ref, g_ref, modn_ref, xo_ref, *refs,
                    tm, seq, n_lat, nb, d, final):
    buf, sem = refs[-2], refs[-1]
    i = pl.program_id(0)
    base = i * tm

    def issue(r, carry):
        pltpu.make_async_copy(y_hbm.at[pl.ds(s1_ref[base + r], 1), :], buf.at[0, pl.ds(r, 1), :], sem).start()
        pltpu.make_async_copy(y_hbm.at[pl.ds(s2_ref[base + r], 1), :], buf.at[1, pl.ds(r, 1), :], sem).start()
        return carry

    lax.fori_loop(0, tm, issue, 0)
    pltpu.make_async_copy(y_hbm.at[pl.ds(0, tm), :], buf.at[0], sem).wait()
    pltpu.make_async_copy(y_hbm.at[pl.ds(0, tm), :], buf.at[1], sem).wait()
    seg = _segment(base, seq, n_lat, nb)
    m = mod_ref[pl.ds(seg, 1), :]
    xn = x_ref[...] + m[:, 5 * d:6 * d] * (buf[0] + buf[1])
    if final:
        xo_ref[...] = _rms(xn) * g_ref[...]
    else:
        xo_ref[...] = xn
        refs[0][...] = _adaln(xn, g_ref, modn_ref, seg, 0, d).astype(BF16)


def _combine(xa, y_sorted, slots, mod_l, g_next, mod_next, *, tm, seq, n_lat, nb, final):
    n_rows, d = xa.shape
    row_spec = pl.BlockSpec((tm, d), lambda i, s1, s2: (i, 0))
    mod_spec = pl.BlockSpec((MOD_ROWS, 6 * d), lambda i, s1, s2: (0, 0))
    out_specs = [row_spec]
    out_shape = [jax.ShapeDtypeStruct((n_rows, d), F32)]
    if not final:
        out_specs.append(row_spec)
        out_shape.append(jax.ShapeDtypeStruct((n_rows, d), BF16))
    grid_spec = pltpu.PrefetchScalarGridSpec(
        num_scalar_prefetch=2,
        grid=(n_rows // tm,),
        in_specs=[
            row_spec,
            pl.BlockSpec(memory_space=pl.ANY),
            mod_spec,
            pl.BlockSpec((1, d), lambda i, s1, s2: (0, 0)),
            mod_spec,
        ],
        out_specs=out_specs,
        scratch_shapes=[pltpu.VMEM((2, tm, d), F32), pltpu.SemaphoreType.DMA(())],
    )
    kern = functools.partial(_combine_kernel, tm=tm, seq=seq, n_lat=n_lat, nb=nb, d=d, final=final)
    out = pl.pallas_call(
        kern,
        grid_spec=grid_spec,
        out_shape=out_shape,
        compiler_params=_params("arbitrary"),
        name="combine",
    )(slots[0], slots[1], xa, y_sorted, mod_l, g_next.reshape(1, d), mod_next)
    return (out[0], None) if final else (out[0], out[1])


def _tile(limit, *sizes):
    t = limit
    while any(s % t for s in sizes):
        t //= 2
    return t


def kernel(x, c, ctx, c_ctx, w_mod, b_mod, norm1_g, norm2_g, w_in, lambda_q1, lambda_k1, lambda_q2,
           lambda_k2, subln_g, conv_w, conv_b, w_attn_branch, w_conv_branch, w_out, w_router,
           router_bias, w_exp_gate, w_exp_up, w_exp_down, final_g):
    nb, seq, d = x.shape
    n_ctx_tok = ctx.shape[1]
    depth = w_mod.shape[0]
    n_exp = w_router.shape[1]
    aw = d // 2
    n_lat = nb * seq
    nt = n_lat + nb * n_ctx_tok
    assert nb < MOD_ROWS and seq % GRID_W == 0 and aw % V_DIM == 0 and n_exp % N_GROUPS == 0

    tm_in = _tile(1024, seq, nb * n_ctx_tok)
    tm_mg = _tile(256, seq, n_ctx_tok)
    tq = _tile(512, seq)
    tg = _tile(256, tm_mg)

    xa = jnp.concatenate([x.reshape(n_lat, d), ctx.reshape(nb * n_ctx_tok, d)], axis=0)
    cvec = jnp.zeros((MOD_ROWS, d), F32).at[:nb].set(c).at[nb].set(c_ctx)
    mod = _modulation(cvec, w_mod, b_mod)
    rope = _rope_tables(seq, tm_in)
    h = _prenorm(xa, mod[0], norm1_g[0], tm=tm_in, seq=seq, n_lat=n_lat, nb=nb)

    for l in range(depth):
        last = l == depth - 1
        lam_init = 0.8 - 0.6 * math.exp(-0.3 * l)
        lam = (jnp.exp(jnp.sum(lambda_q1[l] * lambda_k1[l])) - jnp.exp(jnp.sum(lambda_q2[l] * lambda_k2[l]))
               + lam_init)
        lam_vec = jnp.stack([lam, jnp.asarray(1.0 - lam_init, F32)]).astype(F32)

        qkv, rest = _in_proj(h, w_in, l, rope, tm=tm_in, seq=seq, n_lat=n_lat)
        attn_lat, attn_ctx = _attention(qkv, lam_vec, subln_g[l], tq=tq, seq=seq, ctx=n_ctx_tok, n_lat=n_lat,
                                        nb=nb, aw=aw)
        n_rows = n_lat if last else nt
        xa, h2, eidx, ew = _merge(
            xa, attn_lat, attn_ctx, rest, mod[l], conv_w[l], conv_b[l], w_attn_branch[l].astype(BF16),
            w_conv_branch[l].astype(BF16), w_out[l].astype(BF16), norm2_g[l], w_router, router_bias,
            n_rows=n_rows, tm=tm_mg, seq=seq, ctx=n_ctx_tok, n_lat=n_lat, nb=nb)
        plan = _dispatch_plan(eidx, ew, n_rows, tg, n_exp)
        y_sorted = _experts(h2, plan, w_exp_gate[l].astype(BF16), w_exp_up[l].astype(BF16),
                            w_exp_down[l].astype(BF16), tg=tg)
        g_next, mod_next = (final_g, mod[l]) if last else (norm1_g[l + 1], mod[l + 1])
        xa, h = _combine(xa, y_sorted, plan[4], mod[l], g_next, mod_next, tm=tm_mg, seq=seq, n_lat=n_lat,
                         nb=nb, final=last)

    return xa.reshape(nb, seq, d)
```

```python
import functools
import math

import jax
import jax.numpy as jnp
from jax import lax
from jax.experimental import pallas as pl
from jax.experimental.pallas import tpu as pltpu

HEAD_DIM = 64
V_DIM = 2 * HEAD_DIM
GRID_W = 64
ROPE_THETA = 10000.0
ROPE_AXIS_DIM = HEAD_DIM // 2
N_GROUPS = 4
EPS = 1e-6
LANES = 128
SUBLANES = 8
VMEM_LIMIT_BYTES = 56 * 1024 * 1024
MOD_ROWS = 8
LOG2E = math.log2(math.e)

F32 = jnp.float32
BF16 = jnp.bfloat16
HIGHEST = lax.Precision.HIGHEST
NT_DIMS = (((1,), (1,)), ((), ()))


def _params(*sem):
    return pltpu.CompilerParams(dimension_semantics=sem, vmem_limit_bytes=VMEM_LIMIT_BYTES)


def _rms(x):
    return x * lax.rsqrt(jnp.mean(x * x, axis=-1, keepdims=True) + EPS)


def _segment(row0, seq, n_lat, nb):
    return jnp.where(row0 < n_lat, row0 // seq, nb)


def _adaln(x, g_ref, mod_ref, seg, shift_col, d):
    m = mod_ref[pl.ds(seg, 1), :]
    return _rms(x) * g_ref[...] * (1.0 + m[:, (shift_col + 1) * d:(shift_col + 2) * d]) + m[:, shift_col * d:(shift_col + 1) * d]


def _mod_kernel(c_ref, w_ref, b_ref, o_ref):
    c = c_ref[...]
    sc = c * jax.nn.sigmoid(c)
    o_ref[0] = jnp.dot(sc, w_ref[0], precision=HIGHEST, preferred_element_type=F32) + b_ref[0]


def _modulation(cvec, w_mod, b_mod):
    depth, d, n6 = w_mod.shape
    tn = _tile(1024, n6)
    return pl.pallas_call(
        _mod_kernel,
        grid=(depth, n6 // tn),
        in_specs=[
            pl.BlockSpec((MOD_ROWS, d), lambda l, j: (0, 0)),
            pl.BlockSpec((1, d, tn), lambda l, j: (l, 0, j)),
            pl.BlockSpec((1, 1, tn), lambda l, j: (l, 0, j)),
        ],
        out_specs=pl.BlockSpec((1, MOD_ROWS, tn), lambda l, j: (l, 0, j)),
        out_shape=jax.ShapeDtypeStruct((depth, MOD_ROWS, n6), F32),
        compiler_params=_params("arbitrary", "arbitrary"),
        name="modulation",
    )(cvec, w_mod, b_mod.reshape(depth, 1, n6))


def _prenorm_kernel(x_ref, c_ref, mod_ref, g_ref, xa_ref, h_ref, *, tm, seq, n_lat, nb, d):
    r0 = pl.program_id(0) * tm
    x = jnp.where(r0 < n_lat, x_ref[...], c_ref[...])
    xa_ref[...] = x
    h_ref[...] = _adaln(x, g_ref, mod_ref, _segment(r0, seq, n_lat, nb), 0, d).astype(BF16)


def _prenorm(x2, c2, mod_l, g1, *, tm, seq, nb):
    n_lat, d = x2.shape
    nt = n_lat + c2.shape[0]
    lat_tiles = n_lat // tm
    ctx_tiles = c2.shape[0] // tm
    row_spec = pl.BlockSpec((tm, d), lambda i: (i, 0))
    return pl.pallas_call(
        functools.partial(_prenorm_kernel, tm=tm, seq=seq, n_lat=n_lat, nb=nb, d=d),
        grid=(nt // tm,),
        in_specs=[
            pl.BlockSpec((tm, d), lambda i: (jnp.minimum(i, lat_tiles - 1), 0)),
            pl.BlockSpec((tm, d), lambda i: (jnp.clip(i - lat_tiles, 0, ctx_tiles - 1), 0)),
            pl.BlockSpec((MOD_ROWS, 6 * d), lambda i: (0, 0)),
            pl.BlockSpec((1, d), lambda i: (0, 0)),
        ],
        out_specs=[row_spec, row_spec],
        out_shape=[jax.ShapeDtypeStruct((nt, d), F32), jax.ShapeDtypeStruct((nt, d), BF16)],
        compiler_params=_params("arbitrary"),
        name="prenorm",
    )(x2, c2, mod_l, g1.reshape(1, d))


def _in_kernel(h_ref, w_ref, *refs, rope, aw):
    o_ref, w_scr = refs[-2], refs[-1]
    j = pl.program_id(0)

    @pl.when(pl.program_id(1) == 0)
    def _():
        w_scr[...] = w_ref[0].astype(BF16)

    acc = jnp.dot(h_ref[...], w_scr[...], preferred_element_type=F32)
    if not rope:
        o_ref[...] = acc
        return
    rc_ref, ra_ref, rb_ref = refs[:3]

    @pl.when(j < 2)
    def _():
        scale = jnp.where(j == 0, HEAD_DIM ** -0.5 * LOG2E, 1.0).astype(F32)
        rc, ra, rb = rc_ref[...], ra_ref[...], rb_ref[...]
        for c in range(aw // LANES):
            a = acc[:, c * LANES:(c + 1) * LANES]
            r = (a * rc + pltpu.roll(a, LANES - ROPE_AXIS_DIM // 2, 1) * ra
                 + pltpu.roll(a, ROPE_AXIS_DIM // 2, 1) * rb)
            o_ref[:, c * LANES:(c + 1) * LANES] = (r * scale).astype(BF16)

    @pl.when(j == 2)
    def _():
        o_ref[...] = acc.astype(BF16)


def _rope_tables(seq, tm):
    pos = jnp.arange(seq)
    row = (pos // GRID_W).astype(F32)
    col = (pos % GRID_W).astype(F32)
    inv = ROPE_THETA ** (-jnp.arange(0, ROPE_AXIS_DIM, 2, dtype=F32) / ROPE_AXIS_DIM)
    lane = jnp.arange(LANES)
    jj = lane % HEAD_DIM
    axis = jj // ROPE_AXIS_DIM
    r = jj % ROPE_AXIS_DIM
    f = r % (ROPE_AXIS_DIM // 2)
    half = r // (ROPE_AXIS_DIM // 2)
    posv = jnp.where(axis[None, :] == 0, row[:, None], col[:, None])
    ang = posv * inv[f][None, :]
    cos, sin = jnp.cos(ang), jnp.sin(ang)
    rc = jnp.concatenate([cos, jnp.ones((tm, LANES), F32)], axis=0)
    ra = jnp.concatenate([jnp.where(half[None, :] == 0, -sin, 0.0), jnp.zeros((tm, LANES), F32)], axis=0)
    rb = jnp.concatenate([jnp.where(half[None, :] == 1, sin, 0.0), jnp.zeros((tm, LANES), F32)], axis=0)
    return rc, ra, rb


def _in_proj(h, w_in, layer, rope, *, tm, seq, n_lat):
    nt, d = h.shape
    in_cols = w_in.shape[2]
    aw = d // 2
    n_qkv = 3
    n_rest = in_cols // aw - n_qkv
    n_lat_tiles = n_lat // tm
    seq_tiles = seq // tm

    def rope_idx(j, i):
        return (jnp.where(i < n_lat_tiles, i % seq_tiles, seq_tiles), 0)

    def call(col0, n_col, out_dtype, with_rope):
        in_specs = [
            pl.BlockSpec((tm, d), lambda j, i: (i, 0)),
            pl.BlockSpec((1, d, aw), lambda j, i: (layer, 0, col0 + j)),
        ]
        args = [h, w_in]
        if with_rope:
            in_specs += [pl.BlockSpec((tm, LANES), rope_idx)] * 3
            args += list(rope)
        return pl.pallas_call(
            functools.partial(_in_kernel, rope=with_rope, aw=aw),
            grid=(n_col, nt // tm),
            in_specs=in_specs,
            out_specs=pl.BlockSpec((tm, aw), lambda j, i: (i, j)),
            out_shape=jax.ShapeDtypeStruct((nt, n_col * aw), out_dtype),
            scratch_shapes=[pltpu.VMEM((d, aw), BF16)],
            compiler_params=_params("arbitrary", "arbitrary"),
            name="in_proj_qkv" if with_rope else "in_proj_rest",
        )(*args)

    return call(0, n_qkv, BF16, True), call(n_qkv, n_rest, F32, False)


EXP_CHUNK = 128
ATTN_HALF = 512


def _diff_attention(lam_ref, q_ref, k_ref, v_ref, g_ref, o_ref, s_refs, e_refs, k0, n, half):
    n_half = q_ref.shape[0] // half
    q_maps = []
    for hh in range(n_half):
        q = q_ref[hh * half:(hh + 1) * half, :]
        lane = lax.broadcasted_iota(jnp.int32, q.shape, 1)
        zero = jnp.zeros_like(q)
        q_maps += [jnp.where(lane < HEAD_DIM, q, zero), jnp.where(lane >= HEAD_DIM, q, zero)]
    n_chain = len(q_maps)
    outs = [None] * n_chain

    def scores(i):
        s_refs[i][:, 0:n] = lax.dot_general(q_maps[i], k_ref[k0:k0 + n, :], NT_DIMS,
                                            preferred_element_type=F32)

    def exps(i):
        mx = jnp.max(s_refs[i][:, 0:n], axis=-1, keepdims=True)
        for c in range(n // EXP_CHUNK):
            sl = slice(c * EXP_CHUNK, (c + 1) * EXP_CHUNK)
            e_refs[i][:, sl] = jnp.exp2(s_refs[i][:, sl] - mx).astype(BF16)

    def values(i):
        outs[i] = jnp.dot(e_refs[i][:, 0:n], v_ref[k0:k0 + n, :], preferred_element_type=F32)

    scores(0)
    for i in range(n_chain):
        if i + 1 < n_chain:
            scores(i + 1)
        exps(i)
        if i >= 1:
            values(i - 1)
    values(n_chain - 1)

    for hh in range(n_half):
        a, b = outs[2 * hh], outs[2 * hh + 1]
        o1 = a[:, 0:V_DIM] / a[:, V_DIM:V_DIM + 1]
        o2 = b[:, 0:V_DIM] * (lam_ref[0] / b[:, V_DIM:V_DIM + 1])
        o = _rms(o1 - o2) * g_ref[...] * lam_ref[1]
        o_ref[hh * half:(hh + 1) * half, :] = o.astype(BF16)


def _attn_lat_kernel(lam_ref, q_ref, kl_ref, kc_ref, vl_ref, vc_ref, g_ref, o_ref, k_scr, v_scr, *chain_scr,
                     seq, ctx, half):
    nk = seq + ctx

    @pl.when(pl.program_id(2) == 0)
    def _():
        k_scr[0:seq, :] = kl_ref[...]
        k_scr[seq:nk, :] = kc_ref[...]
        v_scr[0:seq, 0:V_DIM] = vl_ref[...]
        v_scr[seq:nk, 0:V_DIM] = vc_ref[...]
        v_scr[:, V_DIM:2 * V_DIM] = jnp.ones((nk, V_DIM), BF16)

    n_chain = len(chain_scr) // 2
    _diff_attention(lam_ref, q_ref, k_scr, v_scr, g_ref, o_ref, chain_scr[:n_chain], chain_scr[n_chain:],
                    0, nk, half)


def _attn_ctx_kernel(lam_ref, q_ref, k_ref, v_ref, g_ref, o_ref, v_scr, *chain_scr, ctx, half):
    v_scr[:, 0:V_DIM] = v_ref[...]
    v_scr[:, V_DIM:2 * V_DIM] = jnp.ones((ctx, V_DIM), BF16)
    n_chain = len(chain_scr) // 2
    _diff_attention(lam_ref, q_ref, k_ref, v_scr, g_ref, o_ref, chain_scr[:n_chain], chain_scr[n_chain:],
                    0, ctx, half)


def _attention(qkv, lam_vec, subln_g, *, tq, seq, ctx, n_lat, nb, aw):
    nh = aw // V_DIM
    kcol = aw // V_DIM
    vcol = 2 * aw // V_DIM
    ctx0 = n_lat // ctx
    qt = seq // tq
    nk = seq + ctx
    g = subln_g.reshape(1, V_DIM)
    smem = pl.BlockSpec(memory_space=pltpu.SMEM)

    def chain_scratch(rows, half, n):
        n_chain = 2 * (rows // half)
        return ([pltpu.VMEM((half, n), F32)] * n_chain) + ([pltpu.VMEM((half, n), BF16)] * n_chain)

    half = _tile(ATTN_HALF, tq)
    lat = pl.pallas_call(
        functools.partial(_attn_lat_kernel, seq=seq, ctx=ctx, half=half),
        grid=(nb, nh, qt),
        in_specs=[
            smem,
            pl.BlockSpec((tq, V_DIM), lambda b, h, t: (b * qt + t, h)),
            pl.BlockSpec((seq, V_DIM), lambda b, h, t: (b, kcol + h)),
            pl.BlockSpec((ctx, V_DIM), lambda b, h, t: (ctx0 + b, kcol + h)),
            pl.BlockSpec((seq, V_DIM), lambda b, h, t: (b, vcol + h)),
            pl.BlockSpec((ctx, V_DIM), lambda b, h, t: (ctx0 + b, vcol + h)),
            pl.BlockSpec((1, V_DIM), lambda b, h, t: (0, 0)),
        ],
        out_specs=pl.BlockSpec((tq, V_DIM), lambda b, h, t: (b * qt + t, h)),
        out_shape=jax.ShapeDtypeStruct((n_lat, aw), BF16),
        scratch_shapes=[pltpu.VMEM((nk, V_DIM), BF16), pltpu.VMEM((nk, 2 * V_DIM), BF16)]
        + chain_scratch(tq, half, nk),
        compiler_params=_params("arbitrary", "arbitrary", "arbitrary"),
        name="attn_latent",
    )(lam_vec, qkv, qkv, qkv, qkv, qkv, g)

    half_c = _tile(ATTN_HALF, ctx)
    cx = pl.pallas_call(
        functools.partial(_attn_ctx_kernel, ctx=ctx, half=half_c),
        grid=(nb, nh),
        in_specs=[
            smem,
            pl.BlockSpec((ctx, V_DIM), lambda b, h: (ctx0 + b, h)),
            pl.BlockSpec((ctx, V_DIM), lambda b, h: (ctx0 + b, kcol + h)),
            pl.BlockSpec((ctx, V_DIM), lambda b, h: (ctx0 + b, vcol + h)),
            pl.BlockSpec((1, V_DIM), lambda b, h: (0, 0)),
        ],
        out_specs=pl.BlockSpec((ctx, V_DIM), lambda b, h: (b, h)),
        out_shape=jax.ShapeDtypeStruct((nb * ctx, aw), BF16),
        scratch_shapes=[pltpu.VMEM((ctx, 2 * V_DIM), BF16)] + chain_scratch(ctx, half_c, ctx),
        compiler_params=_params("arbitrary", "arbitrary"),
        name="attn_context",
    )(lam_vec, qkv, qkv, qkv, g)
    return lat, cx


def _top2_of4(a, b, c, d):
    m01, n01 = jnp.maximum(a, b), jnp.minimum(a, b)
    m23, n23 = jnp.maximum(c, d), jnp.minimum(c, d)
    return jnp.maximum(m01, m23) + jnp.maximum(jnp.minimum(m01, m23), jnp.maximum(n01, n23))


def _route(logits_t, bias_ref):
    n_exp = logits_t.shape[0]
    per = n_exp // N_GROUPS
    s = [jax.nn.sigmoid(logits_t[e:e + 1, :]) for e in range(n_exp)]
    sb = [s[e] + bias_ref[e:e + 1, :] for e in range(n_exp)]
    gscore = [_top2_of4(*sb[g * per:(g + 1) * per]) for g in range(N_GROUPS)]
    best, gidx = gscore[0], jnp.zeros_like(gscore[0], dtype=jnp.int32)
    for g in range(1, N_GROUPS):
        better = gscore[g] > best
        gidx = jnp.where(better, g, gidx)
        best = jnp.where(better, gscore[g], best)
    cand_b, cand_s = [], []
    for jx in range(per):
        vb, vs = sb[jx], s[jx]
        for g in range(1, N_GROUPS):
            sel = gidx == g
            vb = jnp.where(sel, sb[g * per + jx], vb)
            vs = jnp.where(sel, s[g * per + jx], vs)
        cand_b.append(vb)
        cand_s.append(vs)

    def argmax_first(vals, exclude):
        bv = bi = bs = None
        for jx in range(per):
            v = vals[jx] if exclude is None else jnp.where(exclude == jx, -jnp.inf, vals[jx])
            if bv is None:
                bv, bi, bs = v, jnp.zeros_like(gidx), cand_s[jx]
            else:
                better = v > bv
                bi = jnp.where(better, jx, bi)
                bs = jnp.where(better, cand_s[jx], bs)
                bv = jnp.where(better, v, bv)
        return bi, bs

    j1, w1 = argmax_first(cand_b, None)
    j2, w2 = argmax_first(cand_b, j1)
    tot = w1 + w2
    return gidx * per + j1, gidx * per + j2, w1 / tot, w2 / tot


def _merge_kernel(x_ref, attn_lat_ref, attn_ctx_ref, rest_ref, ccp_ref, cxp_ref, ccn_ref, cxn_ref, mod_ref,
                  convw_ref, convb_ref, wab_ref, wcb_ref, wo_ref, g2_ref, wrh_ref, wrl_ref,
                  xo_ref, h2_ref, logit_ref, *, tm, seq, ctx, n_lat, nb, d, cw):
    i = pl.program_id(0)
    r0 = i * tm
    is_lat = r0 < n_lat
    seg = _segment(r0, seq, n_lat, nb)
    pos = jnp.where(is_lat, r0 % seq, (r0 - n_lat) % ctx)
    slen = jnp.where(is_lat, seq, ctx)
    has_prev = (pos > 0).astype(F32)
    has_next = (pos + tm < slen).astype(F32)

    cb = rest_ref[:, 0:cw]
    u = rest_ref[:, cw:2 * cw] * rest_ref[:, 2 * cw:3 * cw]
    ga = rest_ref[:, 3 * cw:3 * cw + d]
    gc = rest_ref[:, 3 * cw + d:3 * cw + 2 * d]
    halo_prev = ccp_ref[SUBLANES - 1:SUBLANES, :] * cxp_ref[SUBLANES - 1:SUBLANES, :] * has_prev
    halo_next = ccn_ref[0:1, :] * cxn_ref[0:1, :] * has_next
    rid = lax.broadcasted_iota(jnp.int32, u.shape, 0)
    u_prev = jnp.where(rid == 0, halo_prev, pltpu.roll(u, 1, 0))
    u_next = jnp.where(rid == tm - 1, halo_next, pltpu.roll(u, tm - 1, 0))
    y = cb * (u_prev * convw_ref[0:1, :] + u * convw_ref[1:2, :] + u_next * convw_ref[2:3, :]
              + convb_ref[...])

    attn = jnp.where(is_lat, attn_lat_ref[...], attn_ctx_ref[...])
    ma = jnp.dot(attn, wab_ref[0], preferred_element_type=F32)
    mc = jnp.dot(y.astype(BF16), wcb_ref[0], preferred_element_type=F32)
    merged = jax.nn.sigmoid(ga) * ma + jax.nn.sigmoid(gc) * mc
    out = jnp.dot(merged.astype(BF16), wo_ref[0], preferred_element_type=F32)

    m = mod_ref[pl.ds(seg, 1), :]
    xn = x_ref[...] + m[:, 2 * d:3 * d] * out
    xo_ref[...] = xn
    h2 = _adaln(xn, g2_ref, mod_ref, seg, 3, d)
    h2_ref[...] = h2

    h_hi = h2.astype(BF16)
    h_lo = (h2 - h_hi.astype(F32)).astype(BF16)
    w_hi = wrh_ref[...]
    logit_ref[...] = (lax.dot_general(w_hi, h_hi, NT_DIMS, preferred_element_type=F32)
                      + lax.dot_general(w_hi, h_lo, NT_DIMS, preferred_element_type=F32)
                      + lax.dot_general(wrl_ref[...], h_hi, NT_DIMS, preferred_element_type=F32))


def _route_kernel(logit_ref, rbias_ref, eidx_ref, ew_ref):
    e1, e2, w1, w2 = _route(logit_ref[...], rbias_ref)
    eidx_ref[...] = jnp.zeros(eidx_ref.shape, jnp.int32)
    ew_ref[...] = jnp.zeros(ew_ref.shape, F32)
    eidx_ref[0:1, :] = e1
    eidx_ref[1:2, :] = e2
    ew_ref[0:1, :] = w1
    ew_ref[1:2, :] = w2


def _route_tokens(logits_t, router_bias):
    n_exp, n_rows = logits_t.shape
    tr = _tile(2048, n_rows)
    out_spec = pl.BlockSpec((SUBLANES, tr), lambda i: (0, i))
    return pl.pallas_call(
        _route_kernel,
        grid=(n_rows // tr,),
        in_specs=[pl.BlockSpec((n_exp, tr), lambda i: (0, i)), pl.BlockSpec((n_exp, 1), lambda i: (0, 0))],
        out_specs=[out_spec, out_spec],
        out_shape=[jax.ShapeDtypeStruct((SUBLANES, n_rows), jnp.int32),
                   jax.ShapeDtypeStruct((SUBLANES, n_rows), F32)],
        compiler_params=_params("arbitrary"),
        name="route",
    )(logits_t, router_bias.reshape(n_exp, 1))


def _merge(xa, attn_lat, attn_ctx, rest, mod_l, conv_w, conv_b, w_ab, w_cb, w_o, layer, g2, w_router,
           *, n_rows, tm, seq, ctx, n_lat, nb):
    nt, d = xa.shape
    aw = attn_lat.shape[1]
    lat_tiles = n_lat // tm
    ctx_tiles = attn_ctx.shape[0] // tm
    cw = conv_w.shape[1]
    n_exp = w_router.shape[1]
    wr_t = w_router.T
    wr_hi = wr_t.astype(BF16)
    wr_lo = (wr_t - wr_hi.astype(F32)).astype(BF16)
    n_rest = rest.shape[1]
    hb = tm // SUBLANES
    last_hblk = nt // SUBLANES - 1

    def prev_idx(col):
        return lambda i: (jnp.maximum(i * hb - 1, 0), col)

    def next_idx(col):
        return lambda i: (jnp.minimum((i + 1) * hb, last_hblk), col)

    const = lambda i: (0, 0)
    kern = functools.partial(_merge_kernel, tm=tm, seq=seq, ctx=ctx, n_lat=n_lat, nb=nb, d=d, cw=cw)
    return pl.pallas_call(
        kern,
        grid=(n_rows // tm,),
        in_specs=[
            pl.BlockSpec((tm, d), lambda i: (i, 0)),
            pl.BlockSpec((tm, aw), lambda i: (jnp.minimum(i, lat_tiles - 1), 0)),
            pl.BlockSpec((tm, aw), lambda i: (jnp.clip(i - lat_tiles, 0, ctx_tiles - 1), 0)),
            pl.BlockSpec((tm, n_rest), lambda i: (i, 0)),
            pl.BlockSpec((SUBLANES, cw), prev_idx(1)),
            pl.BlockSpec((SUBLANES, cw), prev_idx(2)),
            pl.BlockSpec((SUBLANES, cw), next_idx(1)),
            pl.BlockSpec((SUBLANES, cw), next_idx(2)),
            pl.BlockSpec((MOD_ROWS, 6 * d), const),
            pl.BlockSpec((3, cw), const),
            pl.BlockSpec((1, cw), const),
            pl.BlockSpec((1,) + w_ab.shape[1:], lambda i: (layer, 0, 0)),
            pl.BlockSpec((1,) + w_cb.shape[1:], lambda i: (layer, 0, 0)),
            pl.BlockSpec((1,) + w_o.shape[1:], lambda i: (layer, 0, 0)),
            pl.BlockSpec((1, d), const),
            pl.BlockSpec((n_exp, d), const),
            pl.BlockSpec((n_exp, d), const),
        ],
        out_specs=[
            pl.BlockSpec((tm, d), lambda i: (i, 0)),
            pl.BlockSpec((tm, d), lambda i: (i, 0)),
            pl.BlockSpec((n_exp, tm), lambda i: (0, i)),
        ],
        out_shape=[
            jax.ShapeDtypeStruct((n_rows, d), F32),
            jax.ShapeDtypeStruct((n_rows, d), F32),
            jax.ShapeDtypeStruct((n_exp, n_rows), F32),
        ],
        compiler_params=_params("arbitrary"),
        name="merge",
    )(xa, attn_lat, attn_ctx, rest, rest, rest, rest, rest, mod_l, conv_w, conv_b.reshape(1, cw), w_ab, w_cb, w_o,
      g2.reshape(1, d), wr_hi, wr_lo)


def _dispatch_plan(eidx, ew, n_tok, tg, n_exp):
    ef = eidx[:2].reshape(-1)
    wf = ew[:2].reshape(-1)
    n2 = 2 * n_tok
    onehot = (ef[:, None] == jnp.arange(n_exp)[None, :]).astype(jnp.int32)
    csum = jnp.cumsum(onehot, axis=0)
    rank = jnp.take_along_axis(csum, ef[:, None], axis=1)[:, 0] - 1
    counts = csum[-1]
    padded = ((counts + tg - 1) // tg) * tg
    gend = jnp.cumsum(padded)
    gstart = gend - padded
    dest = gstart[ef] + rank
    n_slots = -(-n2 // tg) * tg + n_exp * tg
    n_tiles = n_slots // tg
    src_of_slot = jnp.full((n_slots,), -1, jnp.int32).at[dest].set(jnp.arange(n2, dtype=jnp.int32))
    filled = src_of_slot >= 0
    src = jnp.maximum(src_of_slot, 0)
    tok_of_slot = jnp.where(filled, src % n_tok, 0)
    w_of_slot = jnp.where(filled, wf[src], 0.0)
    tile_start = jnp.arange(n_tiles, dtype=jnp.int32) * tg
    tile_expert = jnp.sum((gend[None, :] <= tile_start[:, None]).astype(jnp.int32), axis=1)
    tile_expert = jnp.minimum(tile_expert, n_exp - 1).astype(jnp.int32)
    n_active = (gend[-1] // tg).astype(jnp.int32).reshape(1)
    slots = dest.reshape(2, n_tok).astype(jnp.int32)
    return tok_of_slot, w_of_slot.reshape(n_slots, 1), tile_expert, n_active, slots


def _expert_kernel(te_ref, tok_ref, na_ref, h2_hbm, wg_ref, wu_ref, wd_ref, ws_ref, y_ref, xbuf, sem,
                   *, tg):
    del te_ref
    t = pl.program_id(0)
    n_act = na_ref[0]
    slot = t % 2

    def issue(tile, dst_slot):
        base = tile * tg
        for r in range(tg):
            pltpu.make_async_copy(h2_hbm.at[pl.ds(tok_ref[base + r], 1), :],
                                  xbuf.at[dst_slot, pl.ds(r, 1), :], sem.at[dst_slot]).start()

    def wait(dst_slot):
        pltpu.make_async_copy(h2_hbm.at[pl.ds(0, tg), :], xbuf.at[dst_slot], sem.at[dst_slot]).wait()

    @pl.when(t == 0)
    def _():
        issue(0, 0)

    @pl.when(t < n_act)
    def _():
        wait(slot)
        issue(jnp.minimum(t + 1, n_act - 1), 1 - slot)
        hx = xbuf[slot].astype(BF16)
        gte = jnp.dot(hx, wg_ref[0, 0], preferred_element_type=F32)
        up = jnp.dot(hx, wu_ref[0, 0], preferred_element_type=F32)
        he = (gte * jax.nn.sigmoid(gte) * up).astype(BF16)
        y = jnp.dot(he, wd_ref[0, 0], preferred_element_type=F32)
        y_ref[...] = y * ws_ref[...]

    @pl.when(t == n_act - 1)
    def _():
        wait(1 - slot)

    @pl.when(t >= n_act)
    def _():
        y_ref[...] = jnp.zeros(y_ref.shape, F32)


def _experts(h2, plan, wg, wu, wd, layer, *, tg):
    tok_of_slot, w_of_slot, tile_expert, n_active, _ = plan
    n_slots = tok_of_slot.shape[0]
    d = h2.shape[1]
    f = wg.shape[3]
    grid_spec = pltpu.PrefetchScalarGridSpec(
        num_scalar_prefetch=3,
        grid=(n_slots // tg,),
        in_specs=[
            pl.BlockSpec(memory_space=pl.ANY),
            pl.BlockSpec((1, 1, d, f), lambda t, te, tok, na: (layer, te[t], 0, 0)),
            pl.BlockSpec((1, 1, d, f), lambda t, te, tok, na: (layer, te[t], 0, 0)),
            pl.BlockSpec((1, 1, f, d), lambda t, te, tok, na: (layer, te[t], 0, 0)),
            pl.BlockSpec((tg, 1), lambda t, te, tok, na: (t, 0)),
        ],
        out_specs=pl.BlockSpec((tg, d), lambda t, te, tok, na: (t, 0)),
        scratch_shapes=[pltpu.VMEM((2, tg, d), F32), pltpu.SemaphoreType.DMA((2,))],
    )
    return pl.pallas_call(
        functools.partial(_expert_kernel, tg=tg),
        grid_spec=grid_spec,
        out_shape=jax.ShapeDtypeStruct((n_slots, d), F32),
        compiler_params=_params("arbitrary"),
        name="experts",
    )(tile_expert, tok_of_slot, n_active, h2, wg, wu, wd, w_of_slot)


def _combine_kernel(s1_ref, s2_ref, x_ref, y_hbm, mod_ref, g_ref, modn_ref, xo_ref, *refs,
                    tm, seq, n_lat, nb, d, final):
    buf, sem = refs[-2], refs[-1]
    i = pl.program_id(0)
    last = pl.num_programs(0) - 1
    base = i * tm
    slot = i % 2

    def issue(tile, dst_slot):
        def body(r, carry):
            for k, s_ref in enumerate((s1_ref, s2_ref)):
                pltpu.make_async_copy(y_hbm.at[pl.ds(s_ref[tile * tm + r], 1), :],
                                      buf.at[dst_slot, k, pl.ds(r, 1), :], sem.at[dst_slot]).start()
            return carry
        lax.fori_loop(0, tm, body, 0, unroll=8)

    def wait(dst_slot):
        for k in range(2):
            pltpu.make_async_copy(y_hbm.at[pl.ds(0, tm), :], buf.at[dst_slot, k], sem.at[dst_slot]).wait()

    @pl.when(i == 0)
    def _():
        issue(0, 0)

    @pl.when(i < last)
    def _():
        issue(i + 1, 1 - slot)

    wait(slot)
    seg = _segment(base, seq, n_lat, nb)
    m = mod_ref[pl.ds(seg, 1), :]
    xn = x_ref[...] + m[:, 5 * d:6 * d] * (buf[slot, 0] + buf[slot, 1])
    if final:
        xo_ref[...] = _rms(xn) * g_ref[...]
    else:
        xo_ref[...] = xn
        refs[0][...] = _adaln(xn, g_ref, modn_ref, seg, 0, d).astype(BF16)


def _combine(xa, y_sorted, slots, mod_l, g_next, mod_next, *, tm, seq, n_lat, nb, final):
    n_rows, d = xa.shape
    row_spec = pl.BlockSpec((tm, d), lambda i, s1, s2: (i, 0))
    mod_spec = pl.BlockSpec((MOD_ROWS, 6 * d), lambda i, s1, s2: (0, 0))
    out_specs = [row_spec]
    out_shape = [jax.ShapeDtypeStruct((n_rows, d), F32)]
    if not final:
        out_specs.append(row_spec)
        out_shape.append(jax.ShapeDtypeStruct((n_rows, d), BF16))
    grid_spec = pltpu.PrefetchScalarGridSpec(
        num_scalar_prefetch=2,
        grid=(n_rows // tm,),
        in_specs=[
            row_spec,
            pl.BlockSpec(memory_space=pl.ANY),
            mod_spec,
            pl.BlockSpec((1, d), lambda i, s1, s2: (0, 0)),
            mod_spec,
        ],
        out_specs=out_specs,
        scratch_shapes=[pltpu.VMEM((2, 2, tm, d), F32), pltpu.SemaphoreType.DMA((2,))],
    )
    kern = functools.partial(_combine_kernel, tm=tm, seq=seq, n_lat=n_lat, nb=nb, d=d, final=final)
    out = pl.pallas_call(
        kern,
        grid_spec=grid_spec,
        out_shape=out_shape,
        compiler_params=_params("arbitrary"),
        name="combine",
    )(slots[0], slots[1], xa, y_sorted, mod_l, g_next.reshape(1, d), mod_next)
    return (out[0], None) if final else (out[0], out[1])


def _tile(limit, *sizes):
    t = limit
    while any(s % t for s in sizes):
        t //= 2
    return t


def kernel(x, c, ctx, c_ctx, w_mod, b_mod, norm1_g, norm2_g, w_in, lambda_q1, lambda_k1, lambda_q2,
           lambda_k2, subln_g, conv_w, conv_b, w_attn_branch, w_conv_branch, w_out, w_router,
           router_bias, w_exp_gate, w_exp_up, w_exp_down, final_g):
    nb, seq, d = x.shape
    n_ctx_tok = ctx.shape[1]
    depth = w_mod.shape[0]
    n_exp = w_router.shape[1]
    aw = d // 2
    n_lat = nb * seq
    nt = n_lat + nb * n_ctx_tok
    assert nb < MOD_ROWS and seq % GRID_W == 0 and aw % V_DIM == 0 and n_exp % N_GROUPS == 0

    tm_in = _tile(1024, seq, nb * n_ctx_tok)
    tm_mg = _tile(256, seq, n_ctx_tok)
    tq = _tile(512, seq)
    tg = _tile(256, tm_mg)

    cvec = jnp.zeros((MOD_ROWS, d), F32).at[:nb].set(c).at[nb].set(c_ctx)
    mod = _modulation(cvec, w_mod, b_mod)
    rope = _rope_tables(seq, tm_in)
    xa, h = _prenorm(x.reshape(n_lat, d), ctx.reshape(nb * n_ctx_tok, d), mod[0], norm1_g[0],
                     tm=tm_mg, seq=seq, nb=nb)
    w_ab, w_cb, w_o = (w.astype(BF16) for w in (w_attn_branch, w_conv_branch, w_out))
    wg, wu, wd = (w.astype(BF16) for w in (w_exp_gate, w_exp_up, w_exp_down))

    for l in range(depth):
        last = l == depth - 1
        lam_init = 0.8 - 0.6 * math.exp(-0.3 * l)
        lam = (jnp.exp(jnp.sum(lambda_q1[l] * lambda_k1[l])) - jnp.exp(jnp.sum(lambda_q2[l] * lambda_k2[l]))
               + lam_init)
        lam_vec = jnp.stack([lam, jnp.asarray(1.0 - lam_init, F32)]).astype(F32)

        qkv, rest = _in_proj(h, w_in, l, rope, tm=tm_in, seq=seq, n_lat=n_lat)
        attn_lat, attn_ctx = _attention(qkv, lam_vec, subln_g[l], tq=tq, seq=seq, ctx=n_ctx_tok, n_lat=n_lat,
                                        nb=nb, aw=aw)
        n_rows = n_lat if last else nt
        xa, h2, logits_t = _merge(
            xa, attn_lat, attn_ctx, rest, mod[l], conv_w[l], conv_b[l], w_ab, w_cb, w_o, l, norm2_g[l],
            w_router, n_rows=n_rows, tm=tm_mg, seq=seq, ctx=n_ctx_tok, n_lat=n_lat, nb=nb)
        eidx, ew = _route_tokens(logits_t, router_bias)
        plan = _dispatch_plan(eidx, ew, n_rows, tg, n_exp)
        y_sorted = _experts(h2, plan, wg, wu, wd, l, tg=tg)
        g_next, mod_next = (final_g, mod[l]) if last else (norm1_g[l + 1], mod[l + 1])
        xa, h = _combine(xa, y_sorted, plan[4], mod[l], g_next, mod_next, tm=tm_mg, seq=seq, n_lat=n_lat,
                         nb=nb, final=last)

    return xa.reshape(nb, seq, d)
```

```python
import functools
import math

import jax
import jax.numpy as jnp
from jax import lax
from jax.experimental import pallas as pl
from jax.experimental.pallas import tpu as pltpu

HEAD_DIM = 64
V_DIM = 2 * HEAD_DIM
GRID_W = 64
ROPE_THETA = 10000.0
ROPE_AXIS_DIM = HEAD_DIM // 2
N_GROUPS = 4
EPS = 1e-6
LANES = 128
SUBLANES = 8
HALO_ROWS = 2 * SUBLANES
VMEM_LIMIT_BYTES = 56 * 1024 * 1024
MOD_ROWS = 8
LOG2E = math.log2(math.e)

F32 = jnp.float32
BF16 = jnp.bfloat16
HIGHEST = lax.Precision.HIGHEST
NT_DIMS = (((1,), (1,)), ((), ()))


def _params(*sem):
    return pltpu.CompilerParams(dimension_semantics=sem, vmem_limit_bytes=VMEM_LIMIT_BYTES)


def _rms(x):
    return x * lax.rsqrt(jnp.mean(x * x, axis=-1, keepdims=True) + EPS)


def _segment(row0, seq, n_lat, nb):
    return jnp.where(row0 < n_lat, row0 // seq, nb)


def _adaln(x, g_ref, mod_ref, seg, shift_col, d):
    m = mod_ref[pl.ds(seg, 1), :]
    return _rms(x) * g_ref[...] * (1.0 + m[:, (shift_col + 1) * d:(shift_col + 2) * d]) + m[:, shift_col * d:(shift_col + 1) * d]


def _mod_kernel(c_ref, w_ref, b_ref, o_ref):
    c = c_ref[...]
    sc = c * jax.nn.sigmoid(c)
    o_ref[0] = jnp.dot(sc, w_ref[0], precision=HIGHEST, preferred_element_type=F32) + b_ref[0]


def _modulation(cvec, w_mod, b_mod):
    depth, d, n6 = w_mod.shape
    tn = _tile(1024, n6)
    return pl.pallas_call(
        _mod_kernel,
        grid=(depth, n6 // tn),
        in_specs=[
            pl.BlockSpec((MOD_ROWS, d), lambda l, j: (0, 0)),
            pl.BlockSpec((1, d, tn), lambda l, j: (l, 0, j)),
            pl.BlockSpec((1, 1, tn), lambda l, j: (l, 0, j)),
        ],
        out_specs=pl.BlockSpec((1, MOD_ROWS, tn), lambda l, j: (l, 0, j)),
        out_shape=jax.ShapeDtypeStruct((depth, MOD_ROWS, n6), F32),
        compiler_params=_params("arbitrary", "arbitrary"),
        name="modulation",
    )(cvec, w_mod, b_mod.reshape(depth, 1, n6))


def _prenorm_kernel(x_ref, c_ref, mod_ref, g_ref, xa_ref, h_ref, *, tm, seq, n_lat, nb, d):
    r0 = pl.program_id(0) * tm
    x = jnp.where(r0 < n_lat, x_ref[...], c_ref[...])
    xa_ref[...] = x
    h_ref[...] = _adaln(x, g_ref, mod_ref, _segment(r0, seq, n_lat, nb), 0, d).astype(BF16)


def _prenorm(x2, c2, mod_l, g1, *, tm, seq, nb):
    n_lat, d = x2.shape
    nt = n_lat + c2.shape[0]
    lat_tiles = n_lat // tm
    ctx_tiles = c2.shape[0] // tm
    row_spec = pl.BlockSpec((tm, d), lambda i: (i, 0))
    return pl.pallas_call(
        functools.partial(_prenorm_kernel, tm=tm, seq=seq, n_lat=n_lat, nb=nb, d=d),
        grid=(nt // tm,),
        in_specs=[
            pl.BlockSpec((tm, d), lambda i: (jnp.minimum(i, lat_tiles - 1), 0)),
            pl.BlockSpec((tm, d), lambda i: (jnp.clip(i - lat_tiles, 0, ctx_tiles - 1), 0)),
            pl.BlockSpec((MOD_ROWS, 6 * d), lambda i: (0, 0)),
            pl.BlockSpec((1, d), lambda i: (0, 0)),
        ],
        out_specs=[row_spec, row_spec],
        out_shape=[jax.ShapeDtypeStruct((nt, d), F32), jax.ShapeDtypeStruct((nt, d), BF16)],
        compiler_params=_params("arbitrary"),
        name="prenorm",
    )(x2, c2, mod_l, g1.reshape(1, d))


def _in_kernel(h_ref, w_ref, *refs, rope, aw):
    o_ref, w_scr = refs[-2], refs[-1]
    j = pl.program_id(0)

    @pl.when(pl.program_id(1) == 0)
    def _():
        w_scr[...] = w_ref[0].astype(BF16)

    acc = jnp.dot(h_ref[...], w_scr[...], preferred_element_type=F32)
    if not rope:
        o_ref[...] = acc.astype(o_ref.dtype)
        return
    rc_ref, ra_ref, rb_ref = refs[:3]

    @pl.when(j < 2)
    def _():
        scale = jnp.where(j == 0, HEAD_DIM ** -0.5 * LOG2E, 1.0).astype(F32)
        rc, ra, rb = rc_ref[...], ra_ref[...], rb_ref[...]
        for c in range(aw // LANES):
            a = acc[:, c * LANES:(c + 1) * LANES]
            r = (a * rc + pltpu.roll(a, LANES - ROPE_AXIS_DIM // 2, 1) * ra
                 + pltpu.roll(a, ROPE_AXIS_DIM // 2, 1) * rb)
            o_ref[:, c * LANES:(c + 1) * LANES] = (r * scale).astype(BF16)

    @pl.when(j == 2)
    def _():
        o_ref[...] = acc.astype(BF16)


def _rope_tables(seq, tm):
    pos = jnp.arange(seq)
    row = (pos // GRID_W).astype(F32)
    col = (pos % GRID_W).astype(F32)
    inv = ROPE_THETA ** (-jnp.arange(0, ROPE_AXIS_DIM, 2, dtype=F32) / ROPE_AXIS_DIM)
    lane = jnp.arange(LANES)
    jj = lane % HEAD_DIM
    axis = jj // ROPE_AXIS_DIM
    r = jj % ROPE_AXIS_DIM
    f = r % (ROPE_AXIS_DIM // 2)
    half = r // (ROPE_AXIS_DIM // 2)
    posv = jnp.where(axis[None, :] == 0, row[:, None], col[:, None])
    ang = posv * inv[f][None, :]
    cos, sin = jnp.cos(ang), jnp.sin(ang)
    rc = jnp.concatenate([cos, jnp.ones((tm, LANES), F32)], axis=0)
    ra = jnp.concatenate([jnp.where(half[None, :] == 0, -sin, 0.0), jnp.zeros((tm, LANES), F32)], axis=0)
    rb = jnp.concatenate([jnp.where(half[None, :] == 1, sin, 0.0), jnp.zeros((tm, LANES), F32)], axis=0)
    return rc, ra, rb


def _in_proj(h, w_in, layer, rope, *, tm, seq, n_lat):
    nt, d = h.shape
    in_cols = w_in.shape[2]
    aw = d // 2
    n_qkv = 3
    n_rest = in_cols // aw - n_qkv
    n_lat_tiles = n_lat // tm
    seq_tiles = seq // tm

    def rope_idx(j, i):
        return (jnp.where(i < n_lat_tiles, i % seq_tiles, seq_tiles), 0)

    def call(col0, n_col, out_dtype, with_rope):
        in_specs = [
            pl.BlockSpec((tm, d), lambda j, i: (i, 0)),
            pl.BlockSpec((1, d, aw), lambda j, i: (layer, 0, col0 + j)),
        ]
        args = [h, w_in]
        if with_rope:
            in_specs += [pl.BlockSpec((tm, LANES), rope_idx)] * 3
            args += list(rope)
        return pl.pallas_call(
            functools.partial(_in_kernel, rope=with_rope, aw=aw),
            grid=(n_col, nt // tm),
            in_specs=in_specs,
            out_specs=pl.BlockSpec((tm, aw), lambda j, i: (i, j)),
            out_shape=jax.ShapeDtypeStruct((nt, n_col * aw), out_dtype),
            scratch_shapes=[pltpu.VMEM((d, aw), BF16)],
            compiler_params=_params("arbitrary", "arbitrary"),
            name="in_proj_qkv" if with_rope else "in_proj_rest",
        )(*args)

    return call(0, n_qkv, BF16, True), call(n_qkv, n_rest, BF16, False)


EXP_CHUNK = 128
ATTN_HALF = 512


def _split_maps(q_ref, half):
    q_maps = []
    for hh in range(q_ref.shape[0] // half):
        q = q_ref[hh * half:(hh + 1) * half, :]
        lane = lax.broadcasted_iota(jnp.int32, q.shape, 1)
        zero = jnp.zeros_like(q)
        q_maps += [jnp.where(lane < HEAD_DIM, q, zero), jnp.where(lane >= HEAD_DIM, q, zero)]
    return q_maps


def _finish_heads(outs, lam_ref, g_ref, o_ref, half):
    for hh in range(len(outs) // 2):
        a, b = outs[2 * hh], outs[2 * hh + 1]
        o1 = a[:, 0:V_DIM] / a[:, V_DIM:V_DIM + 1]
        o2 = b[:, 0:V_DIM] * (lam_ref[0] / b[:, V_DIM:V_DIM + 1])
        o = _rms(o1 - o2) * g_ref[...] * lam_ref[1]
        o_ref[hh * half:(hh + 1) * half, :] = o.astype(BF16)


def _diff_attention(lam_ref, q_ref, k_ref, v_ref, g_ref, o_ref, s_refs, e_refs, k0, n, half):
    q_maps = _split_maps(q_ref, half)
    n_chain = len(q_maps)
    outs = [None] * n_chain

    def scores(i):
        s_refs[i][:, 0:n] = lax.dot_general(q_maps[i], k_ref[k0:k0 + n, :], NT_DIMS,
                                            preferred_element_type=F32)

    def exps(i):
        mx = jnp.max(s_refs[i][:, 0:n], axis=-1, keepdims=True)
        for c in range(n // EXP_CHUNK):
            sl = slice(c * EXP_CHUNK, (c + 1) * EXP_CHUNK)
            e_refs[i][:, sl] = jnp.exp2(s_refs[i][:, sl] - mx).astype(BF16)

    def values(i):
        outs[i] = jnp.dot(e_refs[i][:, 0:n], v_ref[k0:k0 + n, :], preferred_element_type=F32)

    scores(0)
    for i in range(n_chain):
        if i + 1 < n_chain:
            scores(i + 1)
        exps(i)
        if i >= 1:
            values(i - 1)
    values(n_chain - 1)
    _finish_heads(outs, lam_ref, g_ref, o_ref, half)


def _attn_lat_kernel(lam_ref, q_ref, kl_ref, kc_ref, vl_ref, vc_ref, g_ref, o_ref, k_scr, v_scr,
                     *chain_scr, seq, ctx, half):
    nk = seq + ctx

    @pl.when(pl.program_id(2) == 0)
    def _():
        k_scr[0:seq, :] = kl_ref[...]
        k_scr[seq:nk, :] = kc_ref[...]
        v_scr[0:seq, 0:V_DIM] = vl_ref[...]
        v_scr[seq:nk, 0:V_DIM] = vc_ref[...]
        v_scr[:, V_DIM:2 * V_DIM] = jnp.ones((nk, V_DIM), BF16)

    n_chain = len(chain_scr) // 2
    _diff_attention(lam_ref, q_ref, k_scr, v_scr, g_ref, o_ref, chain_scr[:n_chain], chain_scr[n_chain:],
                    0, nk, half)


def _attn_ctx_kernel(lam_ref, q_ref, k_ref, v_ref, g_ref, o_ref, v_scr, *chain_scr, ctx, half):
    v_scr[:, 0:V_DIM] = v_ref[...]
    v_scr[:, V_DIM:2 * V_DIM] = jnp.ones((ctx, V_DIM), BF16)
    n_chain = len(chain_scr) // 2
    _diff_attention(lam_ref, q_ref, k_ref, v_scr, g_ref, o_ref, chain_scr[:n_chain], chain_scr[n_chain:],
                    0, ctx, half)


def _attention(qkv, lam_vec, subln_g, *, tq, seq, ctx, n_lat, nb, aw):
    nh = aw // V_DIM
    kcol = aw // V_DIM
    vcol = 2 * aw // V_DIM
    ctx0 = n_lat // ctx
    qt = seq // tq
    nk = seq + ctx
    g = subln_g.reshape(1, V_DIM)
    smem = pl.BlockSpec(memory_space=pltpu.SMEM)

    def chain_scratch(rows, half, n):
        n_chain = 2 * (rows // half)
        return ([pltpu.VMEM((half, n), F32)] * n_chain) + ([pltpu.VMEM((half, n), BF16)] * n_chain)

    half = _tile(ATTN_HALF, tq)
    lat = pl.pallas_call(
        functools.partial(_attn_lat_kernel, seq=seq, ctx=ctx, half=half),
        grid=(nb, nh, qt),
        in_specs=[
            smem,
            pl.BlockSpec((tq, V_DIM), lambda b, h, t: (b * qt + t, h)),
            pl.BlockSpec((seq, V_DIM), lambda b, h, t: (b, kcol + h)),
            pl.BlockSpec((ctx, V_DIM), lambda b, h, t: (ctx0 + b, kcol + h)),
            pl.BlockSpec((seq, V_DIM), lambda b, h, t: (b, vcol + h)),
            pl.BlockSpec((ctx, V_DIM), lambda b, h, t: (ctx0 + b, vcol + h)),
            pl.BlockSpec((1, V_DIM), lambda b, h, t: (0, 0)),
        ],
        out_specs=pl.BlockSpec((tq, V_DIM), lambda b, h, t: (b * qt + t, h)),
        out_shape=jax.ShapeDtypeStruct((n_lat, aw), BF16),
        scratch_shapes=[pltpu.VMEM((nk, V_DIM), BF16), pltpu.VMEM((nk, 2 * V_DIM), BF16)]
        + chain_scratch(tq, half, nk),
        compiler_params=_params("arbitrary", "arbitrary", "arbitrary"),
        name="attn_latent",
    )(lam_vec, qkv, qkv, qkv, qkv, qkv, g)

    half_c = _tile(ATTN_HALF, ctx)
    cx = pl.pallas_call(
        functools.partial(_attn_ctx_kernel, ctx=ctx, half=half_c),
        grid=(nb, nh),
        in_specs=[
            smem,
            pl.BlockSpec((ctx, V_DIM), lambda b, h: (ctx0 + b, h)),
            pl.BlockSpec((ctx, V_DIM), lambda b, h: (ctx0 + b, kcol + h)),
            pl.BlockSpec((ctx, V_DIM), lambda b, h: (ctx0 + b, vcol + h)),
            pl.BlockSpec((1, V_DIM), lambda b, h: (0, 0)),
        ],
        out_specs=pl.BlockSpec((ctx, V_DIM), lambda b, h: (b, h)),
        out_shape=jax.ShapeDtypeStruct((nb * ctx, aw), BF16),
        scratch_shapes=[pltpu.VMEM((ctx, 2 * V_DIM), BF16)] + chain_scratch(ctx, half_c, ctx),
        compiler_params=_params("arbitrary", "arbitrary"),
        name="attn_context",
    )(lam_vec, qkv, qkv, qkv, g)
    return lat, cx


def _top2_of4(a, b, c, d):
    m01, n01 = jnp.maximum(a, b), jnp.minimum(a, b)
    m23, n23 = jnp.maximum(c, d), jnp.minimum(c, d)
    return jnp.maximum(m01, m23) + jnp.maximum(jnp.minimum(m01, m23), jnp.maximum(n01, n23))


def _route(logits_t, bias_ref):
    n_exp = logits_t.shape[0]
    per = n_exp // N_GROUPS
    s = [jax.nn.sigmoid(logits_t[e:e + 1, :]) for e in range(n_exp)]
    sb = [s[e] + bias_ref[e:e + 1, :] for e in range(n_exp)]
    gscore = [_top2_of4(*sb[g * per:(g + 1) * per]) for g in range(N_GROUPS)]
    best, gidx = gscore[0], jnp.zeros_like(gscore[0], dtype=jnp.int32)
    for g in range(1, N_GROUPS):
        better = gscore[g] > best
        gidx = jnp.where(better, g, gidx)
        best = jnp.where(better, gscore[g], best)
    cand_b, cand_s = [], []
    for jx in range(per):
        vb, vs = sb[jx], s[jx]
        for g in range(1, N_GROUPS):
            sel = gidx == g
            vb = jnp.where(sel, sb[g * per + jx], vb)
            vs = jnp.where(sel, s[g * per + jx], vs)
        cand_b.append(vb)
        cand_s.append(vs)

    def argmax_first(vals, exclude):
        bv = bi = bs = None
        for jx in range(per):
            v = vals[jx] if exclude is None else jnp.where(exclude == jx, -jnp.inf, vals[jx])
            if bv is None:
                bv, bi, bs = v, jnp.zeros_like(gidx), cand_s[jx]
            else:
                better = v > bv
                bi = jnp.where(better, jx, bi)
                bs = jnp.where(better, cand_s[jx], bs)
                bv = jnp.where(better, v, bv)
        return bi, bs

    j1, w1 = argmax_first(cand_b, None)
    j2, w2 = argmax_first(cand_b, j1)
    tot = w1 + w2
    return gidx * per + j1, gidx * per + j2, w1 / tot, w2 / tot


def _merge_kernel(x_ref, attn_lat_ref, attn_ctx_ref, rest_ref, ccp_ref, cxp_ref, ccn_ref, cxn_ref, mod_ref,
                  convw_ref, convb_ref, wab_ref, wcb_ref, wo_ref, g2_ref, wrh_ref, wrl_ref,
                  xo_ref, h2_ref, logit_ref, *, tm, seq, ctx, n_lat, nb, d, cw):
    i = pl.program_id(0)
    r0 = i * tm
    is_lat = r0 < n_lat
    seg = _segment(r0, seq, n_lat, nb)
    pos = jnp.where(is_lat, r0 % seq, (r0 - n_lat) % ctx)
    slen = jnp.where(is_lat, seq, ctx)
    has_prev = (pos > 0).astype(F32)
    has_next = (pos + tm < slen).astype(F32)

    cb = rest_ref[:, 0:cw].astype(F32)
    u = rest_ref[:, cw:2 * cw].astype(F32) * rest_ref[:, 2 * cw:3 * cw].astype(F32)
    ga = rest_ref[:, 3 * cw:3 * cw + d].astype(F32)
    gc = rest_ref[:, 3 * cw + d:3 * cw + 2 * d].astype(F32)
    hl = HALO_ROWS - 1
    halo_prev = ccp_ref[hl:hl + 1, :].astype(F32) * cxp_ref[hl:hl + 1, :].astype(F32) * has_prev
    halo_next = ccn_ref[0:1, :].astype(F32) * cxn_ref[0:1, :].astype(F32) * has_next
    rid = lax.broadcasted_iota(jnp.int32, u.shape, 0)
    u_prev = jnp.where(rid == 0, halo_prev, pltpu.roll(u, 1, 0))
    u_next = jnp.where(rid == tm - 1, halo_next, pltpu.roll(u, tm - 1, 0))
    y = cb * (u_prev * convw_ref[0:1, :] + u * convw_ref[1:2, :] + u_next * convw_ref[2:3, :]
              + convb_ref[...])

    attn = jnp.where(is_lat, attn_lat_ref[...], attn_ctx_ref[...])
    ma = jnp.dot(attn, wab_ref[0], preferred_element_type=F32)
    mc = jnp.dot(y.astype(BF16), wcb_ref[0], preferred_element_type=F32)
    merged = jax.nn.sigmoid(ga) * ma + jax.nn.sigmoid(gc) * mc
    out = jnp.dot(merged.astype(BF16), wo_ref[0], preferred_element_type=F32)

    m = mod_ref[pl.ds(seg, 1), :]
    xn = x_ref[...] + m[:, 2 * d:3 * d] * out
    xo_ref[...] = xn
    h2 = _adaln(xn, g2_ref, mod_ref, seg, 3, d)
    h2_ref[...] = h2

    h_hi = h2.astype(BF16)
    h_lo = (h2 - h_hi.astype(F32)).astype(BF16)
    w_hi = wrh_ref[...]
    logits = (jnp.dot(h_hi, w_hi, preferred_element_type=F32)
              + jnp.dot(h_lo, w_hi, preferred_element_type=F32)
              + jnp.dot(h_hi, wrl_ref[...], preferred_element_type=F32))
    logit_ref[...] = logits.T[0:logit_ref.shape[0], :]


def _route_kernel(logit_ref, rbias_ref, eidx_ref, ew_ref):
    e1, e2, w1, w2 = _route(logit_ref[...], rbias_ref)
    eidx_ref[...] = jnp.zeros(eidx_ref.shape, jnp.int32)
    ew_ref[...] = jnp.zeros(ew_ref.shape, F32)
    eidx_ref[0:1, :] = e1
    eidx_ref[1:2, :] = e2
    ew_ref[0:1, :] = w1
    ew_ref[1:2, :] = w2


def _route_tokens(logits_t, router_bias):
    n_exp, n_rows = logits_t.shape
    tr = _tile(2048, n_rows)
    out_spec = pl.BlockSpec((SUBLANES, tr), lambda i: (0, i))
    return pl.pallas_call(
        _route_kernel,
        grid=(n_rows // tr,),
        in_specs=[pl.BlockSpec((n_exp, tr), lambda i: (0, i)), pl.BlockSpec((n_exp, 1), lambda i: (0, 0))],
        out_specs=[out_spec, out_spec],
        out_shape=[jax.ShapeDtypeStruct((SUBLANES, n_rows), jnp.int32),
                   jax.ShapeDtypeStruct((SUBLANES, n_rows), F32)],
        compiler_params=_params("arbitrary"),
        name="route",
    )(logits_t, router_bias.reshape(n_exp, 1))


def _merge(xa, attn_lat, attn_ctx, rest, mod_l, conv_w, conv_b, w_ab, w_cb, w_o, layer, g2, w_router,
           *, n_rows, tm, seq, ctx, n_lat, nb):
    nt, d = xa.shape
    aw = attn_lat.shape[1]
    lat_tiles = n_lat // tm
    ctx_tiles = attn_ctx.shape[0] // tm
    cw = conv_w.shape[1]
    n_exp = w_router.shape[1]
    wr_pad = jnp.pad(w_router, ((0, 0), (0, LANES - n_exp)))
    wr_hi = wr_pad.astype(BF16)
    wr_lo = (wr_pad - wr_hi.astype(F32)).astype(BF16)
    n_rest = rest.shape[1]
    hb = tm // HALO_ROWS
    last_hblk = nt // HALO_ROWS - 1

    def prev_idx(col):
        return lambda i: (jnp.maximum(i * hb - 1, 0), col)

    def next_idx(col):
        return lambda i: (jnp.minimum((i + 1) * hb, last_hblk), col)

    const = lambda i: (0, 0)
    kern = functools.partial(_merge_kernel, tm=tm, seq=seq, ctx=ctx, n_lat=n_lat, nb=nb, d=d, cw=cw)
    return pl.pallas_call(
        kern,
        grid=(n_rows // tm,),
        in_specs=[
            pl.BlockSpec((tm, d), lambda i: (i, 0)),
            pl.BlockSpec((tm, aw), lambda i: (jnp.minimum(i, lat_tiles - 1), 0)),
            pl.BlockSpec((tm, aw), lambda i: (jnp.clip(i - lat_tiles, 0, ctx_tiles - 1), 0)),
            pl.BlockSpec((tm, n_rest), lambda i: (i, 0)),
            pl.BlockSpec((HALO_ROWS, cw), prev_idx(1)),
            pl.BlockSpec((HALO_ROWS, cw), prev_idx(2)),
            pl.BlockSpec((HALO_ROWS, cw), next_idx(1)),
            pl.BlockSpec((HALO_ROWS, cw), next_idx(2)),
            pl.BlockSpec((MOD_ROWS, 6 * d), const),
            pl.BlockSpec((3, cw), const),
            pl.BlockSpec((1, cw), const),
            pl.BlockSpec((1,) + w_ab.shape[1:], lambda i: (layer, 0, 0)),
            pl.BlockSpec((1,) + w_cb.shape[1:], lambda i: (layer, 0, 0)),
            pl.BlockSpec((1,) + w_o.shape[1:], lambda i: (layer, 0, 0)),
            pl.BlockSpec((1, d), const),
            pl.BlockSpec((d, LANES), const),
            pl.BlockSpec((d, LANES), const),
        ],
        out_specs=[
            pl.BlockSpec((tm, d), lambda i: (i, 0)),
            pl.BlockSpec((tm, d), lambda i: (i, 0)),
            pl.BlockSpec((n_exp, tm), lambda i: (0, i)),
        ],
        out_shape=[
            jax.ShapeDtypeStruct((n_rows, d), F32),
            jax.ShapeDtypeStruct((n_rows, d), F32),
            jax.ShapeDtypeStruct((n_exp, n_rows), F32),
        ],
        compiler_params=_params("arbitrary"),
        name="merge",
    )(xa, attn_lat, attn_ctx, rest, rest, rest, rest, rest, mod_l, conv_w, conv_b.reshape(1, cw), w_ab, w_cb, w_o,
      g2.reshape(1, d), wr_hi, wr_lo)


def _dispatch_plan(eidx, n_tok, tg, n_exp):
    ef = eidx[:2].reshape(-1)
    n2 = 2 * n_tok
    onehot = (ef[:, None] == jnp.arange(n_exp)[None, :]).astype(jnp.int32)
    csum = jnp.cumsum(onehot, axis=0)
    rank = jnp.sum(onehot * csum, axis=1) - 1
    counts = csum[-1]
    padded = ((counts + tg - 1) // tg) * tg
    gend = jnp.cumsum(padded)
    gstart = gend - padded
    dest = jnp.sum(onehot * gstart[None, :], axis=1) + rank
    n_slots = -(-n2 // tg) * tg + n_exp * tg
    tile_start = jnp.arange(n_slots // tg, dtype=jnp.int32) * tg
    tile_expert = jnp.sum((gend[None, :] <= tile_start[:, None]).astype(jnp.int32), axis=1)
    tile_expert = jnp.minimum(tile_expert, n_exp - 1).astype(jnp.int32)
    n_active = (gend[-1] // tg).astype(jnp.int32).reshape(1)
    slots = dest.reshape(2, n_tok).astype(jnp.int32)
    pad_start = jnp.concatenate([gstart + counts, gend[-1:]]).astype(jnp.int32)
    pad_count = jnp.concatenate([padded - counts, n_slots - gend[-1:]]).astype(jnp.int32)
    return slots, pad_start, pad_count, tile_expert, n_active, n_slots


def _dispatch_kernel(d1_ref, d2_ref, ps_ref, pc_ref, h2_ref, xs_hbm, xbuf, zbuf, sem, zsem, *, tm, n_pad):
    i = pl.program_id(0)
    last = pl.num_programs(0) - 1
    slot = i % 2

    def wait_rows(s):
        for _ in range(2):
            pltpu.make_async_copy(xbuf.at[s], xs_hbm.at[pl.ds(0, tm), :], sem.at[s]).wait()

    def zero_copy(row):
        return pltpu.make_async_copy(zbuf.at[pl.ds(0, 1), :], xs_hbm.at[pl.ds(row, 1), :], zsem)

    @pl.when(i == 0)
    def _():
        zbuf[...] = jnp.zeros(zbuf.shape, F32)
        for e in range(n_pad):
            def start_zero(r, carry, e=e):
                zero_copy(ps_ref[e] + r).start()
                return carry
            lax.fori_loop(0, pc_ref[e], start_zero, 0)

    @pl.when(i >= 2)
    def _():
        wait_rows(slot)

    xbuf[slot] = h2_ref[...]

    def scatter(r, carry):
        for d_ref in (d1_ref, d2_ref):
            pltpu.make_async_copy(xbuf.at[slot, pl.ds(r, 1), :], xs_hbm.at[pl.ds(d_ref[i * tm + r], 1), :],
                                  sem.at[slot]).start()
        return carry

    lax.fori_loop(0, tm, scatter, 0, unroll=8)

    @pl.when(i == last)
    def _():
        wait_rows(slot)
        wait_rows(1 - slot)
        for e in range(n_pad):
            def wait_zero(r, carry):
                zero_copy(0).wait()
                return carry
            lax.fori_loop(0, pc_ref[e], wait_zero, 0)


def _dispatch(h2, plan, *, tm):
    slots, pad_start, pad_count, _, _, n_slots = plan
    n_tok, d = h2.shape
    assert n_tok // tm >= 2
    grid_spec = pltpu.PrefetchScalarGridSpec(
        num_scalar_prefetch=4,
        grid=(n_tok // tm,),
        in_specs=[pl.BlockSpec((tm, d), lambda i, *_: (i, 0))],
        out_specs=pl.BlockSpec(memory_space=pl.ANY),
        scratch_shapes=[pltpu.VMEM((2, tm, d), F32), pltpu.VMEM((SUBLANES, d), F32),
                        pltpu.SemaphoreType.DMA((2,)), pltpu.SemaphoreType.DMA(())],
    )
    return pl.pallas_call(
        functools.partial(_dispatch_kernel, tm=tm, n_pad=pad_start.shape[0]),
        grid_spec=grid_spec,
        out_shape=jax.ShapeDtypeStruct((n_slots, d), F32),
        compiler_params=_params("arbitrary"),
        name="dispatch",
    )(slots[0], slots[1], pad_start, pad_count, h2)


def _expert_kernel(te_ref, na_ref, x_ref, wg_ref, wu_ref, wd_ref, y_ref):
    del te_ref
    t = pl.program_id(0)

    @pl.when(t < na_ref[0])
    def _():
        hx = x_ref[...].astype(BF16)
        gte = jnp.dot(hx, wg_ref[0, 0], preferred_element_type=F32)
        up = jnp.dot(hx, wu_ref[0, 0], preferred_element_type=F32)
        he = (gte * jax.nn.sigmoid(gte) * up).astype(BF16)
        y_ref[...] = jnp.dot(he, wd_ref[0, 0], preferred_element_type=F32)

    @pl.when(t >= na_ref[0])
    def _():
        y_ref[...] = jnp.zeros(y_ref.shape, F32)


def _experts(x_sorted, plan, wg, wu, wd, layer, *, tg):
    _, _, _, tile_expert, n_active, n_slots = plan
    d = x_sorted.shape[1]
    f = wg.shape[3]
    w_idx = lambda t, te, na: (layer, te[t], 0, 0)
    grid_spec = pltpu.PrefetchScalarGridSpec(
        num_scalar_prefetch=2,
        grid=(n_slots // tg,),
        in_specs=[
            pl.BlockSpec((tg, d), lambda t, te, na: (t, 0)),
            pl.BlockSpec((1, 1, d, f), w_idx),
            pl.BlockSpec((1, 1, d, f), w_idx),
            pl.BlockSpec((1, 1, f, d), w_idx),
        ],
        out_specs=pl.BlockSpec((tg, d), lambda t, te, na: (t, 0)),
    )
    return pl.pallas_call(
        _expert_kernel,
        grid_spec=grid_spec,
        out_shape=jax.ShapeDtypeStruct((n_slots, d), F32),
        compiler_params=_params("arbitrary"),
        name="experts",
    )(tile_expert, n_active, x_sorted, wg, wu, wd)


def _combine_kernel(s1_ref, s2_ref, x_ref, wt_ref, y_hbm, mod_ref, g_ref, modn_ref, xo_ref, *refs,
                    tm, seq, n_lat, nb, d, final):
    buf, sem = refs[-2], refs[-1]
    i = pl.program_id(0)
    last = pl.num_programs(0) - 1
    base = i * tm
    slot = i % 2

    def issue(tile, dst_slot):
        def body(r, carry):
            for k, s_ref in enumerate((s1_ref, s2_ref)):
                pltpu.make_async_copy(y_hbm.at[pl.ds(s_ref[tile * tm + r], 1), :],
                                      buf.at[dst_slot, k, pl.ds(r, 1), :], sem.at[dst_slot]).start()
            return carry
        lax.fori_loop(0, tm, body, 0, unroll=8)

    def wait(dst_slot):
        for k in range(2):
            pltpu.make_async_copy(y_hbm.at[pl.ds(0, tm), :], buf.at[dst_slot, k], sem.at[dst_slot]).wait()

    @pl.when(i == 0)
    def _():
        issue(0, 0)

    @pl.when(i < last)
    def _():
        issue(i + 1, 1 - slot)

    wait(slot)
    seg = _segment(base, seq, n_lat, nb)
    m = mod_ref[pl.ds(seg, 1), :]
    moe = wt_ref[:, 0:1] * buf[slot, 0] + wt_ref[:, 1:2] * buf[slot, 1]
    xn = x_ref[...] + m[:, 5 * d:6 * d] * moe
    if final:
        xo_ref[...] = _rms(xn) * g_ref[...]
    else:
        xo_ref[...] = xn
        refs[0][...] = _adaln(xn, g_ref, modn_ref, seg, 0, d).astype(BF16)


def _combine(xa, y_sorted, slots, w_tok, mod_l, g_next, mod_next, *, tm, seq, n_lat, nb, final):
    n_rows, d = xa.shape
    row_spec = pl.BlockSpec((tm, d), lambda i, s1, s2: (i, 0))
    mod_spec = pl.BlockSpec((MOD_ROWS, 6 * d), lambda i, s1, s2: (0, 0))
    out_specs = [row_spec]
    out_shape = [jax.ShapeDtypeStruct((n_rows, d), F32)]
    if not final:
        out_specs.append(row_spec)
        out_shape.append(jax.ShapeDtypeStruct((n_rows, d), BF16))
    grid_spec = pltpu.PrefetchScalarGridSpec(
        num_scalar_prefetch=2,
        grid=(n_rows // tm,),
        in_specs=[
            row_spec,
            pl.BlockSpec((tm, 2), lambda i, s1, s2: (i, 0)),
            pl.BlockSpec(memory_space=pl.ANY),
            mod_spec,
            pl.BlockSpec((1, d), lambda i, s1, s2: (0, 0)),
            mod_spec,
        ],
        out_specs=out_specs,
        scratch_shapes=[pltpu.VMEM((2, 2, tm, d), F32), pltpu.SemaphoreType.DMA((2,))],
    )
    kern = functools.partial(_combine_kernel, tm=tm, seq=seq, n_lat=n_lat, nb=nb, d=d, final=final)
    out = pl.pallas_call(
        kern,
        grid_spec=grid_spec,
        out_shape=out_shape,
        compiler_params=_params("arbitrary"),
        name="combine",
    )(slots[0], slots[1], xa, w_tok, y_sorted, mod_l, g_next.reshape(1, d), mod_next)
    return (out[0], None) if final else (out[0], out[1])


def _tile(limit, *sizes):
    t = limit
    while any(s % t for s in sizes):
        t //= 2
    return t


def kernel(x, c, ctx, c_ctx, w_mod, b_mod, norm1_g, norm2_g, w_in, lambda_q1, lambda_k1, lambda_q2,
           lambda_k2, subln_g, conv_w, conv_b, w_attn_branch, w_conv_branch, w_out, w_router,
           router_bias, w_exp_gate, w_exp_up, w_exp_down, final_g):
    nb, seq, d = x.shape
    n_ctx_tok = ctx.shape[1]
    depth = w_mod.shape[0]
    n_exp = w_router.shape[1]
    aw = d // 2
    n_lat = nb * seq
    nt = n_lat + nb * n_ctx_tok
    assert nb < MOD_ROWS and seq % GRID_W == 0 and aw % V_DIM == 0 and n_exp % N_GROUPS == 0

    tm_in = _tile(1024, seq, nb * n_ctx_tok)
    tm_mg = _tile(256, seq, n_ctx_tok)
    tq = _tile(512, seq)
    tg = _tile(256, tm_mg)

    cvec = jnp.zeros((MOD_ROWS, d), F32).at[:nb].set(c).at[nb].set(c_ctx)
    mod = _modulation(cvec, w_mod, b_mod)
    rope = _rope_tables(seq, tm_in)
    xa, h = _prenorm(x.reshape(n_lat, d), ctx.reshape(nb * n_ctx_tok, d), mod[0], norm1_g[0],
                     tm=tm_mg, seq=seq, nb=nb)
    w_ab, w_cb, w_o = (w.astype(BF16) for w in (w_attn_branch, w_conv_branch, w_out))
    wg, wu, wd = (w.astype(BF16) for w in (w_exp_gate, w_exp_up, w_exp_down))

    for l in range(depth):
        last = l == depth - 1
        lam_init = 0.8 - 0.6 * math.exp(-0.3 * l)
        lam = (jnp.exp(jnp.sum(lambda_q1[l] * lambda_k1[l])) - jnp.exp(jnp.sum(lambda_q2[l] * lambda_k2[l]))
               + lam_init)
        lam_vec = jnp.stack([lam, jnp.asarray(1.0 - lam_init, F32)]).astype(F32)

        qkv, rest = _in_proj(h, w_in, l, rope, tm=tm_in, seq=seq, n_lat=n_lat)
        attn_lat, attn_ctx = _attention(qkv, lam_vec, subln_g[l], tq=tq, seq=seq, ctx=n_ctx_tok, n_lat=n_lat,
                                        nb=nb, aw=aw)
        n_rows = n_lat if last else nt
        xa, h2, logits_t = _merge(
            xa, attn_lat, attn_ctx, rest, mod[l], conv_w[l], conv_b[l], w_ab, w_cb, w_o, l, norm2_g[l],
            w_router, n_rows=n_rows, tm=tm_mg, seq=seq, ctx=n_ctx_tok, n_lat=n_lat, nb=nb)
        eidx, ew = _route_tokens(logits_t, router_bias)
        plan = _dispatch_plan(eidx, n_rows, tg, n_exp)
        x_sorted = _dispatch(h2, plan, tm=tm_mg)
        y_sorted = _experts(x_sorted, plan, wg, wu, wd, l, tg=tg)
        g_next, mod_next = (final_g, mod[l]) if last else (norm1_g[l + 1], mod[l + 1])
        xa, h = _combine(xa, y_sorted, plan[0], ew[:2].T, mod[l], g_next, mod_next, tm=tm_mg, seq=seq,
                         n_lat=n_lat, nb=nb, final=last)

    return xa.reshape(nb, seq, d)
```

```python
import functools
import math

import jax
import jax.numpy as jnp
from jax import lax
from jax.experimental import pallas as pl
from jax.experimental.pallas import tpu as pltpu

HEAD_DIM = 64
V_DIM = 2 * HEAD_DIM
GRID_W = 64
ROPE_THETA = 10000.0
ROPE_AXIS_DIM = HEAD_DIM // 2
N_GROUPS = 4
EPS = 1e-6
LANES = 128
SUBLANES = 8
HALO_ROWS = 2 * SUBLANES
VMEM_LIMIT_BYTES = 56 * 1024 * 1024
MOD_ROWS = 8
LOG2E = math.log2(math.e)

F32 = jnp.float32
BF16 = jnp.bfloat16
HIGHEST = lax.Precision.HIGHEST
NT_DIMS = (((1,), (1,)), ((), ()))


def _params(*sem):
    return pltpu.CompilerParams(dimension_semantics=sem, vmem_limit_bytes=VMEM_LIMIT_BYTES)


def _rms(x):
    return x * lax.rsqrt(jnp.mean(x * x, axis=-1, keepdims=True) + EPS)


U32 = jnp.uint32
HIGH_HALF = 0xFFFF0000


def _pack_halves(x):
    c2 = x.shape[1] // 2
    lo = lax.bitcast_convert_type(x[:, 0:c2].astype(BF16).astype(F32), U32) >> 16
    hi = lax.bitcast_convert_type(x[:, c2:2 * c2].astype(BF16).astype(F32), U32) & U32(HIGH_HALF)
    return hi | lo


def _unpack_halves(w):
    return (lax.bitcast_convert_type(w << 16, F32), lax.bitcast_convert_type(w & U32(HIGH_HALF), F32))


def _segment(row0, seq, n_lat, nb):
    return jnp.where(row0 < n_lat, row0 // seq, nb)


def _adaln(x, g_ref, mod_ref, seg, shift_col, d):
    m = mod_ref[pl.ds(seg, 1), :]
    return _rms(x) * g_ref[...] * (1.0 + m[:, (shift_col + 1) * d:(shift_col + 2) * d]) + m[:, shift_col * d:(shift_col + 1) * d]


def _mod_kernel(c_ref, w_ref, b_ref, o_ref):
    c = c_ref[...]
    sc = c * jax.nn.sigmoid(c)
    o_ref[0] = jnp.dot(sc, w_ref[0], precision=HIGHEST, preferred_element_type=F32) + b_ref[0]


def _modulation(cvec, w_mod, b_mod):
    depth, d, n6 = w_mod.shape
    tn = _tile(1024, n6)
    return pl.pallas_call(
        _mod_kernel,
        grid=(depth, n6 // tn),
        in_specs=[
            pl.BlockSpec((MOD_ROWS, d), lambda l, j: (0, 0)),
            pl.BlockSpec((1, d, tn), lambda l, j: (l, 0, j)),
            pl.BlockSpec((1, 1, tn), lambda l, j: (l, 0, j)),
        ],
        out_specs=pl.BlockSpec((1, MOD_ROWS, tn), lambda l, j: (l, 0, j)),
        out_shape=jax.ShapeDtypeStruct((depth, MOD_ROWS, n6), F32),
        compiler_params=_params("arbitrary", "arbitrary"),
        name="modulation",
    )(cvec, w_mod, b_mod.reshape(depth, 1, n6))


def _prenorm_kernel(x_ref, c_ref, mod_ref, g_ref, xa_ref, h_ref, *, tm, seq, n_lat, nb, d):
    r0 = pl.program_id(0) * tm
    x = jnp.where(r0 < n_lat, x_ref[...], c_ref[...])
    xa_ref[...] = x
    h_ref[...] = _adaln(x, g_ref, mod_ref, _segment(r0, seq, n_lat, nb), 0, d).astype(BF16)


def _prenorm(x2, c2, mod_l, g1, *, tm, seq, nb):
    n_lat, d = x2.shape
    nt = n_lat + c2.shape[0]
    lat_tiles = n_lat // tm
    ctx_tiles = c2.shape[0] // tm
    row_spec = pl.BlockSpec((tm, d), lambda i: (i, 0))
    return pl.pallas_call(
        functools.partial(_prenorm_kernel, tm=tm, seq=seq, n_lat=n_lat, nb=nb, d=d),
        grid=(nt // tm,),
        in_specs=[
            pl.BlockSpec((tm, d), lambda i: (jnp.minimum(i, lat_tiles - 1), 0)),
            pl.BlockSpec((tm, d), lambda i: (jnp.clip(i - lat_tiles, 0, ctx_tiles - 1), 0)),
            pl.BlockSpec((MOD_ROWS, 6 * d), lambda i: (0, 0)),
            pl.BlockSpec((1, d), lambda i: (0, 0)),
        ],
        out_specs=[row_spec, row_spec],
        out_shape=[jax.ShapeDtypeStruct((nt, d), F32), jax.ShapeDtypeStruct((nt, d), BF16)],
        compiler_params=_params("arbitrary"),
        name="prenorm",
    )(x2, c2, mod_l, g1.reshape(1, d))


def _in_kernel(h_ref, w_ref, *refs, rope, aw):
    o_ref, w_scr = refs[-2], refs[-1]
    j = pl.program_id(0)

    @pl.when(pl.program_id(1) == 0)
    def _():
        w_scr[...] = w_ref[0].astype(BF16)

    acc = jnp.dot(h_ref[...], w_scr[...], preferred_element_type=F32)
    if not rope:
        o_ref[...] = acc.astype(o_ref.dtype)
        return
    rc_ref, ra_ref, rb_ref = refs[:3]

    @pl.when(j < 2)
    def _():
        scale = jnp.where(j == 0, HEAD_DIM ** -0.5 * LOG2E, 1.0).astype(F32)
        rc, ra, rb = rc_ref[...], ra_ref[...], rb_ref[...]
        for c in range(aw // LANES):
            a = acc[:, c * LANES:(c + 1) * LANES]
            r = (a * rc + pltpu.roll(a, LANES - ROPE_AXIS_DIM // 2, 1) * ra
                 + pltpu.roll(a, ROPE_AXIS_DIM // 2, 1) * rb)
            o_ref[:, c * LANES:(c + 1) * LANES] = (r * scale).astype(BF16)

    @pl.when(j == 2)
    def _():
        o_ref[...] = acc.astype(BF16)


def _rope_tables(seq, tm):
    pos = jnp.arange(seq)
    row = (pos // GRID_W).astype(F32)
    col = (pos % GRID_W).astype(F32)
    inv = ROPE_THETA ** (-jnp.arange(0, ROPE_AXIS_DIM, 2, dtype=F32) / ROPE_AXIS_DIM)
    lane = jnp.arange(LANES)
    jj = lane % HEAD_DIM
    axis = jj // ROPE_AXIS_DIM
    r = jj % ROPE_AXIS_DIM
    f = r % (ROPE_AXIS_DIM // 2)
    half = r // (ROPE_AXIS_DIM // 2)
    posv = jnp.where(axis[None, :] == 0, row[:, None], col[:, None])
    ang = posv * inv[f][None, :]
    cos, sin = jnp.cos(ang), jnp.sin(ang)
    rc = jnp.concatenate([cos, jnp.ones((tm, LANES), F32)], axis=0)
    ra = jnp.concatenate([jnp.where(half[None, :] == 0, -sin, 0.0), jnp.zeros((tm, LANES), F32)], axis=0)
    rb = jnp.concatenate([jnp.where(half[None, :] == 1, sin, 0.0), jnp.zeros((tm, LANES), F32)], axis=0)
    return rc, ra, rb


def _in_proj(h, w_in, layer, rope, *, tm, seq, n_lat):
    nt, d = h.shape
    in_cols = w_in.shape[2]
    aw = d // 2
    n_qkv = 3
    n_rest = in_cols // aw - n_qkv
    n_lat_tiles = n_lat // tm
    seq_tiles = seq // tm

    def rope_idx(j, i):
        return (jnp.where(i < n_lat_tiles, i % seq_tiles, seq_tiles), 0)

    def call(col0, n_col, out_dtype, with_rope):
        in_specs = [
            pl.BlockSpec((tm, d), lambda j, i: (i, 0)),
            pl.BlockSpec((1, d, aw), lambda j, i: (layer, 0, col0 + j)),
        ]
        args = [h, w_in]
        if with_rope:
            in_specs += [pl.BlockSpec((tm, LANES), rope_idx)] * 3
            args += list(rope)
        return pl.pallas_call(
            functools.partial(_in_kernel, rope=with_rope, aw=aw),
            grid=(n_col, nt // tm),
            in_specs=in_specs,
            out_specs=pl.BlockSpec((tm, aw), lambda j, i: (i, j)),
            out_shape=jax.ShapeDtypeStruct((nt, n_col * aw), out_dtype),
            scratch_shapes=[pltpu.VMEM((d, aw), BF16)],
            compiler_params=_params("arbitrary", "arbitrary"),
            name="in_proj_qkv" if with_rope else "in_proj_rest",
        )(*args)

    return call(0, n_qkv, BF16, True), call(n_qkv, n_rest, BF16, False)


EXP_CHUNK = 128
ATTN_HALF = 512


def _split_maps(q_ref, half):
    q_maps = []
    for hh in range(q_ref.shape[0] // half):
        q = q_ref[hh * half:(hh + 1) * half, :]
        lane = lax.broadcasted_iota(jnp.int32, q.shape, 1)
        zero = jnp.zeros_like(q)
        q_maps += [jnp.where(lane < HEAD_DIM, q, zero), jnp.where(lane >= HEAD_DIM, q, zero)]
    return q_maps


def _finish_heads(outs, lam_ref, g_ref, o_ref, half):
    for hh in range(len(outs) // 2):
        a, b = outs[2 * hh], outs[2 * hh + 1]
        o1 = a[:, 0:V_DIM] / a[:, V_DIM:V_DIM + 1]
        o2 = b[:, 0:V_DIM] * (lam_ref[0] / b[:, V_DIM:V_DIM + 1])
        o = _rms(o1 - o2) * g_ref[...] * lam_ref[1]
        o_ref[hh * half:(hh + 1) * half, :] = o.astype(BF16)


def _diff_attention(lam_ref, q_ref, k_ref, v_ref, g_ref, o_ref, s_refs, e_refs, k0, n, half):
    q_maps = _split_maps(q_ref, half)
    n_chain = len(q_maps)
    outs = [None] * n_chain

    def scores(i):
        s_refs[i][:, 0:n] = lax.dot_general(q_maps[i], k_ref[k0:k0 + n, :], NT_DIMS,
                                            preferred_element_type=F32)

    def exps(i):
        mx = jnp.max(s_refs[i][:, 0:n], axis=-1, keepdims=True)
        for c in range(n // EXP_CHUNK):
            sl = slice(c * EXP_CHUNK, (c + 1) * EXP_CHUNK)
            e_refs[i][:, sl] = jnp.exp2(s_refs[i][:, sl] - mx).astype(BF16)

    def values(i):
        outs[i] = jnp.dot(e_refs[i][:, 0:n], v_ref[k0:k0 + n, :], preferred_element_type=F32)

    scores(0)
    for i in range(n_chain):
        if i + 1 < n_chain:
            scores(i + 1)
        exps(i)
        if i >= 1:
            values(i - 1)
    values(n_chain - 1)
    _finish_heads(outs, lam_ref, g_ref, o_ref, half)


def _attn_lat_kernel(lam_ref, q_ref, kl_ref, kc_ref, vl_ref, vc_ref, g_ref, o_ref, k_scr, v_scr,
                     *chain_scr, seq, ctx, half):
    nk = seq + ctx

    @pl.when(pl.program_id(2) == 0)
    def _():
        k_scr[0:seq, :] = kl_ref[...]
        k_scr[seq:nk, :] = kc_ref[...]
        v_scr[0:seq, 0:V_DIM] = vl_ref[...]
        v_scr[seq:nk, 0:V_DIM] = vc_ref[...]
        v_scr[:, V_DIM:2 * V_DIM] = jnp.ones((nk, V_DIM), BF16)

    n_chain = len(chain_scr) // 2
    _diff_attention(lam_ref, q_ref, k_scr, v_scr, g_ref, o_ref, chain_scr[:n_chain], chain_scr[n_chain:],
                    0, nk, half)


def _attn_ctx_kernel(lam_ref, q_ref, k_ref, v_ref, g_ref, o_ref, v_scr, *chain_scr, ctx, half):
    v_scr[:, 0:V_DIM] = v_ref[...]
    v_scr[:, V_DIM:2 * V_DIM] = jnp.ones((ctx, V_DIM), BF16)
    n_chain = len(chain_scr) // 2
    _diff_attention(lam_ref, q_ref, k_ref, v_scr, g_ref, o_ref, chain_scr[:n_chain], chain_scr[n_chain:],
                    0, ctx, half)


def _attention(qkv, lam_vec, subln_g, *, tq, seq, ctx, n_lat, nb, aw):
    nh = aw // V_DIM
    kcol = aw // V_DIM
    vcol = 2 * aw // V_DIM
    ctx0 = n_lat // ctx
    qt = seq // tq
    nk = seq + ctx
    g = subln_g.reshape(1, V_DIM)
    smem = pl.BlockSpec(memory_space=pltpu.SMEM)

    def chain_scratch(rows, half, n):
        n_chain = 2 * (rows // half)
        return ([pltpu.VMEM((half, n), F32)] * n_chain) + ([pltpu.VMEM((half, n), BF16)] * n_chain)

    half = _tile(ATTN_HALF, tq)
    lat = pl.pallas_call(
        functools.partial(_attn_lat_kernel, seq=seq, ctx=ctx, half=half),
        grid=(nb, nh, qt),
        in_specs=[
            smem,
            pl.BlockSpec((tq, V_DIM), lambda b, h, t: (b * qt + t, h)),
            pl.BlockSpec((seq, V_DIM), lambda b, h, t: (b, kcol + h)),
            pl.BlockSpec((ctx, V_DIM), lambda b, h, t: (ctx0 + b, kcol + h)),
            pl.BlockSpec((seq, V_DIM), lambda b, h, t: (b, vcol + h)),
            pl.BlockSpec((ctx, V_DIM), lambda b, h, t: (ctx0 + b, vcol + h)),
            pl.BlockSpec((1, V_DIM), lambda b, h, t: (0, 0)),
        ],
        out_specs=pl.BlockSpec((tq, V_DIM), lambda b, h, t: (b * qt + t, h)),
        out_shape=jax.ShapeDtypeStruct((n_lat, aw), BF16),
        scratch_shapes=[pltpu.VMEM((nk, V_DIM), BF16), pltpu.VMEM((nk, 2 * V_DIM), BF16)]
        + chain_scratch(tq, half, nk),
        compiler_params=_params("arbitrary", "arbitrary", "arbitrary"),
        name="attn_latent",
    )(lam_vec, qkv, qkv, qkv, qkv, qkv, g)

    half_c = _tile(ATTN_HALF, ctx)
    cx = pl.pallas_call(
        functools.partial(_attn_ctx_kernel, ctx=ctx, half=half_c),
        grid=(nb, nh),
        in_specs=[
            smem,
            pl.BlockSpec((ctx, V_DIM), lambda b, h: (ctx0 + b, h)),
            pl.BlockSpec((ctx, V_DIM), lambda b, h: (ctx0 + b, kcol + h)),
            pl.BlockSpec((ctx, V_DIM), lambda b, h: (ctx0 + b, vcol + h)),
            pl.BlockSpec((1, V_DIM), lambda b, h: (0, 0)),
        ],
        out_specs=pl.BlockSpec((ctx, V_DIM), lambda b, h: (b, h)),
        out_shape=jax.ShapeDtypeStruct((nb * ctx, aw), BF16),
        scratch_shapes=[pltpu.VMEM((ctx, 2 * V_DIM), BF16)] + chain_scratch(ctx, half_c, ctx),
        compiler_params=_params("arbitrary", "arbitrary"),
        name="attn_context",
    )(lam_vec, qkv, qkv, qkv, g)
    return lat, cx


def _top2_of4(a, b, c, d):
    m01, n01 = jnp.maximum(a, b), jnp.minimum(a, b)
    m23, n23 = jnp.maximum(c, d), jnp.minimum(c, d)
    return jnp.maximum(m01, m23) + jnp.maximum(jnp.minimum(m01, m23), jnp.maximum(n01, n23))


def _route(logits_t, bias_ref):
    n_exp = logits_t.shape[0]
    per = n_exp // N_GROUPS
    s = [jax.nn.sigmoid(logits_t[e:e + 1, :]) for e in range(n_exp)]
    sb = [s[e] + bias_ref[e:e + 1, :] for e in range(n_exp)]
    gscore = [_top2_of4(*sb[g * per:(g + 1) * per]) for g in range(N_GROUPS)]
    best, gidx = gscore[0], jnp.zeros_like(gscore[0], dtype=jnp.int32)
    for g in range(1, N_GROUPS):
        better = gscore[g] > best
        gidx = jnp.where(better, g, gidx)
        best = jnp.where(better, gscore[g], best)
    cand_b, cand_s = [], []
    for jx in range(per):
        vb, vs = sb[jx], s[jx]
        for g in range(1, N_GROUPS):
            sel = gidx == g
            vb = jnp.where(sel, sb[g * per + jx], vb)
            vs = jnp.where(sel, s[g * per + jx], vs)
        cand_b.append(vb)
        cand_s.append(vs)

    def argmax_first(vals, exclude):
        bv = bi = bs = None
        for jx in range(per):
            v = vals[jx] if exclude is None else jnp.where(exclude == jx, -jnp.inf, vals[jx])
            if bv is None:
                bv, bi, bs = v, jnp.zeros_like(gidx), cand_s[jx]
            else:
                better = v > bv
                bi = jnp.where(better, jx, bi)
                bs = jnp.where(better, cand_s[jx], bs)
                bv = jnp.where(better, v, bv)
        return bi, bs

    j1, w1 = argmax_first(cand_b, None)
    j2, w2 = argmax_first(cand_b, j1)
    tot = w1 + w2
    return gidx * per + j1, gidx * per + j2, w1 / tot, w2 / tot


def _merge_kernel(x_ref, attn_lat_ref, attn_ctx_ref, rest_ref, ccp_ref, cxp_ref, ccn_ref, cxn_ref, mod_ref,
                  convw_ref, convb_ref, wab_ref, wcb_ref, wo_ref, g2_ref, wrh_ref, wrl_ref,
                  xo_ref, h2_ref, logit_ref, *, tm, seq, ctx, n_lat, nb, d, cw):
    i = pl.program_id(0)
    r0 = i * tm
    is_lat = r0 < n_lat
    seg = _segment(r0, seq, n_lat, nb)
    pos = jnp.where(is_lat, r0 % seq, (r0 - n_lat) % ctx)
    slen = jnp.where(is_lat, seq, ctx)
    has_prev = (pos > 0).astype(F32)
    has_next = (pos + tm < slen).astype(F32)

    cb = rest_ref[:, 0:cw].astype(F32)
    u = rest_ref[:, cw:2 * cw].astype(F32) * rest_ref[:, 2 * cw:3 * cw].astype(F32)
    ga = rest_ref[:, 3 * cw:3 * cw + d].astype(F32)
    gc = rest_ref[:, 3 * cw + d:3 * cw + 2 * d].astype(F32)
    hl = HALO_ROWS - 1
    halo_prev = ccp_ref[hl:hl + 1, :].astype(F32) * cxp_ref[hl:hl + 1, :].astype(F32) * has_prev
    halo_next = ccn_ref[0:1, :].astype(F32) * cxn_ref[0:1, :].astype(F32) * has_next
    rid = lax.broadcasted_iota(jnp.int32, u.shape, 0)
    u_prev = jnp.where(rid == 0, halo_prev, pltpu.roll(u, 1, 0))
    u_next = jnp.where(rid == tm - 1, halo_next, pltpu.roll(u, tm - 1, 0))
    y = cb * (u_prev * convw_ref[0:1, :] + u * convw_ref[1:2, :] + u_next * convw_ref[2:3, :]
              + convb_ref[...])

    attn = jnp.where(is_lat, attn_lat_ref[...], attn_ctx_ref[...])
    ma = jnp.dot(attn, wab_ref[0], preferred_element_type=F32)
    mc = jnp.dot(y.astype(BF16), wcb_ref[0], preferred_element_type=F32)
    merged = jax.nn.sigmoid(ga) * ma + jax.nn.sigmoid(gc) * mc
    out = jnp.dot(merged.astype(BF16), wo_ref[0], preferred_element_type=F32)

    m = mod_ref[pl.ds(seg, 1), :]
    xn = x_ref[...] + m[:, 2 * d:3 * d] * out
    xo_ref[...] = xn
    h2 = _adaln(xn, g2_ref, mod_ref, seg, 3, d)
    h2_ref[...] = _pack_halves(h2)

    h_hi = h2.astype(BF16)
    h_lo = (h2 - h_hi.astype(F32)).astype(BF16)
    w_hi = wrh_ref[...]
    logits = (jnp.dot(h_hi, w_hi, preferred_element_type=F32)
              + jnp.dot(h_lo, w_hi, preferred_element_type=F32)
              + jnp.dot(h_hi, wrl_ref[...], preferred_element_type=F32))
    logit_ref[...] = logits.T[0:logit_ref.shape[0], :]


def _route_kernel(logit_ref, rbias_ref, eidx_ref, ew_ref):
    e1, e2, w1, w2 = _route(logit_ref[...], rbias_ref)
    eidx_ref[...] = jnp.zeros(eidx_ref.shape, jnp.int32)
    ew_ref[...] = jnp.zeros(ew_ref.shape, F32)
    eidx_ref[0:1, :] = e1
    eidx_ref[1:2, :] = e2
    ew_ref[0:1, :] = w1
    ew_ref[1:2, :] = w2


def _route_tokens(logits_t, router_bias):
    n_exp, n_rows = logits_t.shape
    tr = _tile(2048, n_rows)
    out_spec = pl.BlockSpec((SUBLANES, tr), lambda i: (0, i))
    return pl.pallas_call(
        _route_kernel,
        grid=(n_rows // tr,),
        in_specs=[pl.BlockSpec((n_exp, tr), lambda i: (0, i)), pl.BlockSpec((n_exp, 1), lambda i: (0, 0))],
        out_specs=[out_spec, out_spec],
        out_shape=[jax.ShapeDtypeStruct((SUBLANES, n_rows), jnp.int32),
                   jax.ShapeDtypeStruct((SUBLANES, n_rows), F32)],
        compiler_params=_params("arbitrary"),
        name="route",
    )(logits_t, router_bias.reshape(n_exp, 1))


def _merge(xa, attn_lat, attn_ctx, rest, mod_l, conv_w, conv_b, w_ab, w_cb, w_o, layer, g2, w_router,
           *, n_rows, tm, seq, ctx, n_lat, nb):
    nt, d = xa.shape
    aw = attn_lat.shape[1]
    lat_tiles = n_lat // tm
    ctx_tiles = attn_ctx.shape[0] // tm
    cw = conv_w.shape[1]
    n_exp = w_router.shape[1]
    wr_pad = jnp.pad(w_router, ((0, 0), (0, LANES - n_exp)))
    wr_hi = wr_pad.astype(BF16)
    wr_lo = (wr_pad - wr_hi.astype(F32)).astype(BF16)
    n_rest = rest.shape[1]
    hb = tm // HALO_ROWS
    last_hblk = nt // HALO_ROWS - 1

    def prev_idx(col):
        return lambda i: (jnp.maximum(i * hb - 1, 0), col)

    def next_idx(col):
        return lambda i: (jnp.minimum((i + 1) * hb, last_hblk), col)

    const = lambda i: (0, 0)
    kern = functools.partial(_merge_kernel, tm=tm, seq=seq, ctx=ctx, n_lat=n_lat, nb=nb, d=d, cw=cw)
    return pl.pallas_call(
        kern,
        grid=(n_rows // tm,),
        in_specs=[
            pl.BlockSpec((tm, d), lambda i: (i, 0)),
            pl.BlockSpec((tm, aw), lambda i: (jnp.minimum(i, lat_tiles - 1), 0)),
            pl.BlockSpec((tm, aw), lambda i: (jnp.clip(i - lat_tiles, 0, ctx_tiles - 1), 0)),
            pl.BlockSpec((tm, n_rest), lambda i: (i, 0)),
            pl.BlockSpec((HALO_ROWS, cw), prev_idx(1)),
            pl.BlockSpec((HALO_ROWS, cw), prev_idx(2)),
            pl.BlockSpec((HALO_ROWS, cw), next_idx(1)),
            pl.BlockSpec((HALO_ROWS, cw), next_idx(2)),
            pl.BlockSpec((MOD_ROWS, 6 * d), const),
            pl.BlockSpec((3, cw), const),
            pl.BlockSpec((1, cw), const),
            pl.BlockSpec((1,) + w_ab.shape[1:], lambda i: (layer, 0, 0)),
            pl.BlockSpec((1,) + w_cb.shape[1:], lambda i: (layer, 0, 0)),
            pl.BlockSpec((1,) + w_o.shape[1:], lambda i: (layer, 0, 0)),
            pl.BlockSpec((1, d), const),
            pl.BlockSpec((d, LANES), const),
            pl.BlockSpec((d, LANES), const),
        ],
        out_specs=[
            pl.BlockSpec((tm, d), lambda i: (i, 0)),
            pl.BlockSpec((tm, d // 2), lambda i: (i, 0)),
            pl.BlockSpec((n_exp, tm), lambda i: (0, i)),
        ],
        out_shape=[
            jax.ShapeDtypeStruct((n_rows, d), F32),
            jax.ShapeDtypeStruct((n_rows, d // 2), U32),
            jax.ShapeDtypeStruct((n_exp, n_rows), F32),
        ],
        compiler_params=_params("arbitrary"),
        name="merge",
    )(xa, attn_lat, attn_ctx, rest, rest, rest, rest, rest, mod_l, conv_w, conv_b.reshape(1, cw), w_ab, w_cb, w_o,
      g2.reshape(1, d), wr_hi, wr_lo)


def _dispatch_plan(eidx, n_tok, tg, n_exp):
    ef = eidx[:2].reshape(-1)
    n2 = 2 * n_tok
    onehot = (ef[:, None] == jnp.arange(n_exp)[None, :]).astype(jnp.int32)
    csum = jnp.cumsum(onehot, axis=0)
    rank = jnp.sum(onehot * csum, axis=1) - 1
    counts = csum[-1]
    padded = ((counts + tg - 1) // tg) * tg
    gend = jnp.cumsum(padded)
    gstart = gend - padded
    dest = jnp.sum(onehot * gstart[None, :], axis=1) + rank
    n_slots = -(-n2 // tg) * tg + n_exp * tg
    tile_start = jnp.arange(n_slots // tg, dtype=jnp.int32) * tg
    tile_expert = jnp.sum((gend[None, :] <= tile_start[:, None]).astype(jnp.int32), axis=1)
    tile_expert = jnp.minimum(tile_expert, n_exp - 1).astype(jnp.int32)
    n_active = (gend[-1] // tg).astype(jnp.int32).reshape(1)
    slots = dest.reshape(2, n_tok).astype(jnp.int32)
    pad_start = jnp.concatenate([gstart + counts, gend[-1:]]).astype(jnp.int32)
    pad_count = jnp.concatenate([padded - counts, n_slots - gend[-1:]]).astype(jnp.int32)
    return slots, pad_start, pad_count, tile_expert, n_active, n_slots


def _dispatch_kernel(d1_ref, d2_ref, ps_ref, pc_ref, h2_ref, xs_hbm, xbuf, zbuf, sem, zsem, *, tm, n_pad):
    i = pl.program_id(0)
    last = pl.num_programs(0) - 1
    slot = i % 2

    def wait_rows(s):
        for _ in range(2):
            pltpu.make_async_copy(xbuf.at[s], xs_hbm.at[pl.ds(0, tm), :], sem.at[s]).wait()

    def zero_copy(row):
        return pltpu.make_async_copy(zbuf.at[pl.ds(0, 1), :], xs_hbm.at[pl.ds(row, 1), :], zsem)

    @pl.when(i == 0)
    def _():
        zbuf[...] = jnp.zeros(zbuf.shape, zbuf.dtype)
        for e in range(n_pad):
            def start_zero(r, carry, e=e):
                zero_copy(ps_ref[e] + r).start()
                return carry
            lax.fori_loop(0, pc_ref[e], start_zero, 0)

    @pl.when(i >= 2)
    def _():
        wait_rows(slot)

    xbuf[slot] = h2_ref[...]

    def scatter(r, carry):
        for d_ref in (d1_ref, d2_ref):
            pltpu.make_async_copy(xbuf.at[slot, pl.ds(r, 1), :], xs_hbm.at[pl.ds(d_ref[i * tm + r], 1), :],
                                  sem.at[slot]).start()
        return carry

    lax.fori_loop(0, tm, scatter, 0, unroll=8)

    @pl.when(i == last)
    def _():
        wait_rows(slot)
        wait_rows(1 - slot)
        for e in range(n_pad):
            def wait_zero(r, carry):
                zero_copy(0).wait()
                return carry
            lax.fori_loop(0, pc_ref[e], wait_zero, 0)


def _dispatch(h2, plan, *, tm):
    slots, pad_start, pad_count, _, _, n_slots = plan
    n_tok, d = h2.shape
    assert n_tok // tm >= 2
    grid_spec = pltpu.PrefetchScalarGridSpec(
        num_scalar_prefetch=4,
        grid=(n_tok // tm,),
        in_specs=[pl.BlockSpec((tm, d), lambda i, *_: (i, 0))],
        out_specs=pl.BlockSpec(memory_space=pl.ANY),
        scratch_shapes=[pltpu.VMEM((2, tm, d), h2.dtype), pltpu.VMEM((SUBLANES, d), h2.dtype),
                        pltpu.SemaphoreType.DMA((2,)), pltpu.SemaphoreType.DMA(())],
    )
    return pl.pallas_call(
        functools.partial(_dispatch_kernel, tm=tm, n_pad=pad_start.shape[0]),
        grid_spec=grid_spec,
        out_shape=jax.ShapeDtypeStruct((n_slots, d), h2.dtype),
        compiler_params=_params("arbitrary"),
        name="dispatch",
    )(slots[0], slots[1], pad_start, pad_count, h2)


def _expert_kernel(te_ref, na_ref, x_ref, wg_ref, wu_ref, wd_ref, y_ref):
    del te_ref
    t = pl.program_id(0)

    @pl.when(t < na_ref[0])
    def _():
        d2 = x_ref.shape[1]
        x_lo, x_hi = (h.astype(BF16) for h in _unpack_halves(x_ref[...]))
        gte = (jnp.dot(x_lo, wg_ref[0, 0, 0:d2, :], preferred_element_type=F32)
               + jnp.dot(x_hi, wg_ref[0, 0, d2:2 * d2, :], preferred_element_type=F32))
        up = (jnp.dot(x_lo, wu_ref[0, 0, 0:d2, :], preferred_element_type=F32)
              + jnp.dot(x_hi, wu_ref[0, 0, d2:2 * d2, :], preferred_element_type=F32))
        he = (gte * jax.nn.sigmoid(gte) * up).astype(BF16)
        y_ref[...] = _pack_halves(jnp.dot(he, wd_ref[0, 0], preferred_element_type=F32))

    @pl.when(t >= na_ref[0])
    def _():
        y_ref[...] = jnp.zeros(y_ref.shape, U32)


def _experts(x_sorted, plan, wg, wu, wd, layer, *, tg):
    _, _, _, tile_expert, n_active, n_slots = plan
    d2 = x_sorted.shape[1]
    d = 2 * d2
    f = wg.shape[3]
    w_idx = lambda t, te, na: (layer, te[t], 0, 0)
    grid_spec = pltpu.PrefetchScalarGridSpec(
        num_scalar_prefetch=2,
        grid=(n_slots // tg,),
        in_specs=[
            pl.BlockSpec((tg, d2), lambda t, te, na: (t, 0)),
            pl.BlockSpec((1, 1, d, f), w_idx),
            pl.BlockSpec((1, 1, d, f), w_idx),
            pl.BlockSpec((1, 1, f, d), w_idx),
        ],
        out_specs=pl.BlockSpec((tg, d2), lambda t, te, na: (t, 0)),
    )
    return pl.pallas_call(
        _expert_kernel,
        grid_spec=grid_spec,
        out_shape=jax.ShapeDtypeStruct((n_slots, d2), U32),
        compiler_params=_params("arbitrary"),
        name="experts",
    )(tile_expert, n_active, x_sorted, wg, wu, wd)


def _combine_kernel(s1_ref, s2_ref, x_ref, wt_ref, y_hbm, mod_ref, g_ref, modn_ref, xo_ref, *refs,
                    tm, seq, n_lat, nb, d, final):
    buf, sem = refs[-2], refs[-1]
    i = pl.program_id(0)
    last = pl.num_programs(0) - 1
    base = i * tm
    slot = i % 2

    def issue(tile, dst_slot):
        def body(r, carry):
            for k, s_ref in enumerate((s1_ref, s2_ref)):
                pltpu.make_async_copy(y_hbm.at[pl.ds(s_ref[tile * tm + r], 1), :],
                                      buf.at[dst_slot, k, pl.ds(r, 1), :], sem.at[dst_slot]).start()
            return carry
        lax.fori_loop(0, tm, body, 0, unroll=8)

    def wait(dst_slot):
        for k in range(2):
            pltpu.make_async_copy(y_hbm.at[pl.ds(0, tm), :], buf.at[dst_slot, k], sem.at[dst_slot]).wait()

    @pl.when(i == 0)
    def _():
        issue(0, 0)

    @pl.when(i < last)
    def _():
        issue(i + 1, 1 - slot)

    wait(slot)
    seg = _segment(base, seq, n_lat, nb)
    m = mod_ref[pl.ds(seg, 1), :]
    y1_lo, y1_hi = _unpack_halves(buf[slot, 0])
    y2_lo, y2_hi = _unpack_halves(buf[slot, 1])
    w1, w2 = wt_ref[:, 0:1], wt_ref[:, 1:2]
    moe = jnp.concatenate([w1 * y1_lo + w2 * y2_lo, w1 * y1_hi + w2 * y2_hi], axis=1)
    xn = x_ref[...] + m[:, 5 * d:6 * d] * moe
    if final:
        xo_ref[...] = _rms(xn) * g_ref[...]
    else:
        xo_ref[...] = xn
        refs[0][...] = _adaln(xn, g_ref, modn_ref, seg, 0, d).astype(BF16)


def _combine(xa, y_sorted, slots, w_tok, mod_l, g_next, mod_next, *, tm, seq, n_lat, nb, final):
    n_rows, d = xa.shape
    row_spec = pl.BlockSpec((tm, d), lambda i, s1, s2: (i, 0))
    mod_spec = pl.BlockSpec((MOD_ROWS, 6 * d), lambda i, s1, s2: (0, 0))
    out_specs = [row_spec]
    out_shape = [jax.ShapeDtypeStruct((n_rows, d), F32)]
    if not final:
        out_specs.append(row_spec)
        out_shape.append(jax.ShapeDtypeStruct((n_rows, d), BF16))
    grid_spec = pltpu.PrefetchScalarGridSpec(
        num_scalar_prefetch=2,
        grid=(n_rows // tm,),
        in_specs=[
            row_spec,
            pl.BlockSpec((tm, 2), lambda i, s1, s2: (i, 0)),
            pl.BlockSpec(memory_space=pl.ANY),
            mod_spec,
            pl.BlockSpec((1, d), lambda i, s1, s2: (0, 0)),
            mod_spec,
        ],
        out_specs=out_specs,
        scratch_shapes=[pltpu.VMEM((2, 2, tm, d // 2), U32), pltpu.SemaphoreType.DMA((2,))],
    )
    kern = functools.partial(_combine_kernel, tm=tm, seq=seq, n_lat=n_lat, nb=nb, d=d, final=final)
    out = pl.pallas_call(
        kern,
        grid_spec=grid_spec,
        out_shape=out_shape,
        compiler_params=_params("arbitrary"),
        name="combine",
    )(slots[0], slots[1], xa, w_tok, y_sorted, mod_l, g_next.reshape(1, d), mod_next)
    return (out[0], None) if final else (out[0], out[1])


def _tile(limit, *sizes):
    t = limit
    while any(s % t for s in sizes):
        t //= 2
    return t


def kernel(x, c, ctx, c_ctx, w_mod, b_mod, norm1_g, norm2_g, w_in, lambda_q1, lambda_k1, lambda_q2,
           lambda_k2, subln_g, conv_w, conv_b, w_attn_branch, w_conv_branch, w_out, w_router,
           router_bias, w_exp_gate, w_exp_up, w_exp_down, final_g):
    nb, seq, d = x.shape
    n_ctx_tok = ctx.shape[1]
    depth = w_mod.shape[0]
    n_exp = w_router.shape[1]
    aw = d // 2
    n_lat = nb * seq
    nt = n_lat + nb * n_ctx_tok
    assert nb < MOD_ROWS and seq % GRID_W == 0 and aw % V_DIM == 0 and n_exp % N_GROUPS == 0

    tm_in = _tile(1024, seq, nb * n_ctx_tok)
    tm_mg = _tile(256, seq, n_ctx_tok)
    tq = _tile(512, seq)
    tg = _tile(256, tm_mg)

    cvec = jnp.zeros((MOD_ROWS, d), F32).at[:nb].set(c).at[nb].set(c_ctx)
    mod = _modulation(cvec, w_mod, b_mod)
    rope = _rope_tables(seq, tm_in)
    xa, h = _prenorm(x.reshape(n_lat, d), ctx.reshape(nb * n_ctx_tok, d), mod[0], norm1_g[0],
                     tm=tm_mg, seq=seq, nb=nb)
    w_ab, w_cb, w_o = (w.astype(BF16) for w in (w_attn_branch, w_conv_branch, w_out))
    wg, wu, wd = (w.astype(BF16) for w in (w_exp_gate, w_exp_up, w_exp_down))

    for l in range(depth):
        last = l == depth - 1
        lam_init = 0.8 - 0.6 * math.exp(-0.3 * l)
        lam = (jnp.exp(jnp.sum(lambda_q1[l] * lambda_k1[l])) - jnp.exp(jnp.sum(lambda_q2[l] * lambda_k2[l]))
               + lam_init)
        lam_vec = jnp.stack([lam, jnp.asarray(1.0 - lam_init, F32)]).astype(F32)

        qkv, rest = _in_proj(h, w_in, l, rope, tm=tm_in, seq=seq, n_lat=n_lat)
        attn_lat, attn_ctx = _attention(qkv, lam_vec, subln_g[l], tq=tq, seq=seq, ctx=n_ctx_tok, n_lat=n_lat,
                                        nb=nb, aw=aw)
        n_rows = n_lat if last else nt
        xa, h2, logits_t = _merge(
            xa, attn_lat, attn_ctx, rest, mod[l], conv_w[l], conv_b[l], w_ab, w_cb, w_o, l, norm2_g[l],
            w_router, n_rows=n_rows, tm=tm_mg, seq=seq, ctx=n_ctx_tok, n_lat=n_lat, nb=nb)
        eidx, ew = _route_tokens(logits_t, router_bias)
        plan = _dispatch_plan(eidx, n_rows, tg, n_exp)
        x_sorted = _dispatch(h2, plan, tm=tm_mg)
        y_sorted = _experts(x_sorted, plan, wg, wu, wd, l, tg=tg)
        g_next, mod_next = (final_g, mod[l]) if last else (norm1_g[l + 1], mod[l + 1])
        xa, h = _combine(xa, y_sorted, plan[0], ew[:2].T, mod[l], g_next, mod_next, tm=tm_mg, seq=seq,
                         n_lat=n_lat, nb=nb, final=last)

    return xa.reshape(nb, seq, d)
```

```python
import functools
import math

import jax
import jax.numpy as jnp
from jax import lax
from jax.experimental import pallas as pl
from jax.experimental.pallas import tpu as pltpu

HEAD_DIM = 64
V_DIM = 2 * HEAD_DIM
GRID_W = 64
ROPE_THETA = 10000.0
ROPE_AXIS_DIM = HEAD_DIM // 2
N_GROUPS = 4
EPS = 1e-6
LANES = 128
SUBLANES = 8
HALO_ROWS = 2 * SUBLANES
VMEM_LIMIT_BYTES = 56 * 1024 * 1024
MOD_ROWS = 8
LOG2E = math.log2(math.e)

F32 = jnp.float32
BF16 = jnp.bfloat16
HIGHEST = lax.Precision.HIGHEST
NT_DIMS = (((1,), (1,)), ((), ()))


def _params(*sem):
    return pltpu.CompilerParams(dimension_semantics=sem, vmem_limit_bytes=VMEM_LIMIT_BYTES)


def _rms(x):
    return x * lax.rsqrt(jnp.mean(x * x, axis=-1, keepdims=True) + EPS)


U32 = jnp.uint32
HIGH_HALF = 0xFFFF0000


def _pack_halves(x):
    c2 = x.shape[1] // 2
    lo = lax.bitcast_convert_type(x[:, 0:c2].astype(BF16).astype(F32), U32) >> 16
    hi = lax.bitcast_convert_type(x[:, c2:2 * c2].astype(BF16).astype(F32), U32) & U32(HIGH_HALF)
    return hi | lo


def _unpack_halves(w):
    return (lax.bitcast_convert_type(w << 16, F32), lax.bitcast_convert_type(w & U32(HIGH_HALF), F32))


def _segment(row0, seq, n_lat, nb):
    return jnp.where(row0 < n_lat, row0 // seq, nb)


def _adaln(x, g_ref, mod_ref, seg, shift_col, d):
    m = mod_ref[pl.ds(seg, 1), :]
    return _rms(x) * g_ref[...] * (1.0 + m[:, (shift_col + 1) * d:(shift_col + 2) * d]) + m[:, shift_col * d:(shift_col + 1) * d]


def _mod_kernel(c_ref, w_ref, b_ref, o_ref):
    c = c_ref[...]
    sc = c * jax.nn.sigmoid(c)
    o_ref[0] = jnp.dot(sc, w_ref[0], precision=HIGHEST, preferred_element_type=F32) + b_ref[0]


def _modulation(cvec, w_mod, b_mod):
    depth, d, n6 = w_mod.shape
    tn = _tile(1024, n6)
    return pl.pallas_call(
        _mod_kernel,
        grid=(depth, n6 // tn),
        in_specs=[
            pl.BlockSpec((MOD_ROWS, d), lambda l, j: (0, 0)),
            pl.BlockSpec((1, d, tn), lambda l, j: (l, 0, j)),
            pl.BlockSpec((1, 1, tn), lambda l, j: (l, 0, j)),
        ],
        out_specs=pl.BlockSpec((1, MOD_ROWS, tn), lambda l, j: (l, 0, j)),
        out_shape=jax.ShapeDtypeStruct((depth, MOD_ROWS, n6), F32),
        compiler_params=_params("arbitrary", "arbitrary"),
        name="modulation",
    )(cvec, w_mod, b_mod.reshape(depth, 1, n6))


def _prenorm_kernel(x_ref, c_ref, mod_ref, g_ref, xa_ref, h_ref, *, tm, seq, n_lat, nb, d):
    r0 = pl.program_id(0) * tm
    x = jnp.where(r0 < n_lat, x_ref[...], c_ref[...])
    xa_ref[...] = x
    h_ref[...] = _adaln(x, g_ref, mod_ref, _segment(r0, seq, n_lat, nb), 0, d).astype(BF16)


def _prenorm(x2, c2, mod_l, g1, *, tm, seq, nb):
    n_lat, d = x2.shape
    nt = n_lat + c2.shape[0]
    lat_tiles = n_lat // tm
    ctx_tiles = c2.shape[0] // tm
    row_spec = pl.BlockSpec((tm, d), lambda i: (i, 0))
    return pl.pallas_call(
        functools.partial(_prenorm_kernel, tm=tm, seq=seq, n_lat=n_lat, nb=nb, d=d),
        grid=(nt // tm,),
        in_specs=[
            pl.BlockSpec((tm, d), lambda i: (jnp.minimum(i, lat_tiles - 1), 0)),
            pl.BlockSpec((tm, d), lambda i: (jnp.clip(i - lat_tiles, 0, ctx_tiles - 1), 0)),
            pl.BlockSpec((MOD_ROWS, 6 * d), lambda i: (0, 0)),
            pl.BlockSpec((1, d), lambda i: (0, 0)),
        ],
        out_specs=[row_spec, row_spec],
        out_shape=[jax.ShapeDtypeStruct((nt, d), F32), jax.ShapeDtypeStruct((nt, d), BF16)],
        compiler_params=_params("arbitrary"),
        name="prenorm",
    )(x2, c2, mod_l, g1.reshape(1, d))


def _in_kernel(h_ref, w_ref, *refs, rope, aw):
    o_ref, w_scr = refs[-2], refs[-1]
    j = pl.program_id(0)

    @pl.when(pl.program_id(1) == 0)
    def _():
        w_scr[...] = w_ref[0].astype(BF16)

    acc = jnp.dot(h_ref[...], w_scr[...], preferred_element_type=F32)
    if not rope:
        o_ref[...] = acc.astype(o_ref.dtype)
        return
    rc_ref, ra_ref, rb_ref = refs[:3]

    @pl.when(j < 2)
    def _():
        scale = jnp.where(j == 0, HEAD_DIM ** -0.5 * LOG2E, 1.0).astype(F32)
        rc, ra, rb = rc_ref[...], ra_ref[...], rb_ref[...]
        for c in range(aw // LANES):
            a = acc[:, c * LANES:(c + 1) * LANES]
            r = (a * rc + pltpu.roll(a, LANES - ROPE_AXIS_DIM // 2, 1) * ra
                 + pltpu.roll(a, ROPE_AXIS_DIM // 2, 1) * rb)
            o_ref[:, c * LANES:(c + 1) * LANES] = (r * scale).astype(BF16)

    @pl.when(j == 2)
    def _():
        o_ref[...] = acc.astype(BF16)


def _rope_tables(seq, tm):
    pos = jnp.arange(seq)
    row = (pos // GRID_W).astype(F32)
    col = (pos % GRID_W).astype(F32)
    inv = ROPE_THETA ** (-jnp.arange(0, ROPE_AXIS_DIM, 2, dtype=F32) / ROPE_AXIS_DIM)
    lane = jnp.arange(LANES)
    jj = lane % HEAD_DIM
    axis = jj // ROPE_AXIS_DIM
    r = jj % ROPE_AXIS_DIM
    f = r % (ROPE_AXIS_DIM // 2)
    half = r // (ROPE_AXIS_DIM // 2)
    posv = jnp.where(axis[None, :] == 0, row[:, None], col[:, None])
    ang = posv * inv[f][None, :]
    cos, sin = jnp.cos(ang), jnp.sin(ang)
    rc = jnp.concatenate([cos, jnp.ones((tm, LANES), F32)], axis=0)
    ra = jnp.concatenate([jnp.where(half[None, :] == 0, -sin, 0.0), jnp.zeros((tm, LANES), F32)], axis=0)
    rb = jnp.concatenate([jnp.where(half[None, :] == 1, sin, 0.0), jnp.zeros((tm, LANES), F32)], axis=0)
    return rc, ra, rb


def _in_proj(h, w_in, layer, rope, *, tm, seq, n_lat):
    nt, d = h.shape
    in_cols = w_in.shape[2]
    aw = d // 2
    n_qkv = 3
    n_rest = in_cols // aw - n_qkv
    n_lat_tiles = n_lat // tm
    seq_tiles = seq // tm

    def rope_idx(j, i):
        return (jnp.where(i < n_lat_tiles, i % seq_tiles, seq_tiles), 0)

    def call(col0, n_col, out_dtype, with_rope):
        in_specs = [
            pl.BlockSpec((tm, d), lambda j, i: (i, 0)),
            pl.BlockSpec((1, d, aw), lambda j, i: (layer, 0, col0 + j)),
        ]
        args = [h, w_in]
        if with_rope:
            in_specs += [pl.BlockSpec((tm, LANES), rope_idx)] * 3
            args += list(rope)
        return pl.pallas_call(
            functools.partial(_in_kernel, rope=with_rope, aw=aw),
            grid=(n_col, nt // tm),
            in_specs=in_specs,
            out_specs=pl.BlockSpec((tm, aw), lambda j, i: (i, j)),
            out_shape=jax.ShapeDtypeStruct((nt, n_col * aw), out_dtype),
            scratch_shapes=[pltpu.VMEM((d, aw), BF16)],
            compiler_params=_params("arbitrary", "arbitrary"),
            name="in_proj_qkv" if with_rope else "in_proj_rest",
        )(*args)

    return call(0, n_qkv, BF16, True), call(n_qkv, n_rest, BF16, False)


EXP_CHUNK = 128
ATTN_HALF = 512


def _split_maps(q_ref, half):
    q_maps = []
    for hh in range(q_ref.shape[0] // half):
        q = q_ref[hh * half:(hh + 1) * half, :]
        lane = lax.broadcasted_iota(jnp.int32, q.shape, 1)
        zero = jnp.zeros_like(q)
        q_maps += [jnp.where(lane < HEAD_DIM, q, zero), jnp.where(lane >= HEAD_DIM, q, zero)]
    return q_maps


def _finish_heads(outs, lam_ref, g_ref, o_ref, half):
    for hh in range(len(outs) // 2):
        a, b = outs[2 * hh], outs[2 * hh + 1]
        o1 = a[:, 0:V_DIM] / a[:, V_DIM:V_DIM + 1]
        o2 = b[:, 0:V_DIM] * (lam_ref[0] / b[:, V_DIM:V_DIM + 1])
        o = _rms(o1 - o2) * g_ref[...] * lam_ref[1]
        o_ref[hh * half:(hh + 1) * half, :] = o.astype(BF16)


def _diff_attention(lam_ref, q_ref, k_ref, v_ref, g_ref, o_ref, s_refs, e_refs, k0, n, half):
    q_maps = _split_maps(q_ref, half)
    n_chain = len(q_maps)
    outs = [None] * n_chain

    def scores(i):
        s_refs[i][:, 0:n] = lax.dot_general(q_maps[i], k_ref[k0:k0 + n, :], NT_DIMS,
                                            preferred_element_type=F32)

    def exps(i):
        mx = jnp.max(s_refs[i][:, 0:n], axis=-1, keepdims=True)
        for c in range(n // EXP_CHUNK):
            sl = slice(c * EXP_CHUNK, (c + 1) * EXP_CHUNK)
            e_refs[i][:, sl] = jnp.exp2(s_refs[i][:, sl] - mx).astype(BF16)

    def values(i):
        outs[i] = jnp.dot(e_refs[i][:, 0:n], v_ref[k0:k0 + n, :], preferred_element_type=F32)

    scores(0)
    for i in range(n_chain):
        if i + 1 < n_chain:
            scores(i + 1)
        exps(i)
        if i >= 1:
            values(i - 1)
    values(n_chain - 1)
    _finish_heads(outs, lam_ref, g_ref, o_ref, half)


def _attn_lat_kernel(lam_ref, q_ref, kl_ref, kc_ref, vl_ref, vc_ref, g_ref, o_ref, k_scr, v_scr,
                     *chain_scr, seq, ctx, half):
    nk = seq + ctx

    @pl.when(pl.program_id(2) == 0)
    def _():
        k_scr[0:seq, :] = kl_ref[...]
        k_scr[seq:nk, :] = kc_ref[...]
        v_scr[0:seq, 0:V_DIM] = vl_ref[...]
        v_scr[seq:nk, 0:V_DIM] = vc_ref[...]
        v_scr[:, V_DIM:2 * V_DIM] = jnp.ones((nk, V_DIM), BF16)

    n_chain = len(chain_scr) // 2
    _diff_attention(lam_ref, q_ref, k_scr, v_scr, g_ref, o_ref, chain_scr[:n_chain], chain_scr[n_chain:],
                    0, nk, half)


def _attn_ctx_kernel(lam_ref, q_ref, k_ref, v_ref, g_ref, o_ref, v_scr, *chain_scr, ctx, half):
    v_scr[:, 0:V_DIM] = v_ref[...]
    v_scr[:, V_DIM:2 * V_DIM] = jnp.ones((ctx, V_DIM), BF16)
    n_chain = len(chain_scr) // 2
    _diff_attention(lam_ref, q_ref, k_ref, v_scr, g_ref, o_ref, chain_scr[:n_chain], chain_scr[n_chain:],
                    0, ctx, half)


def _attention(qkv, lam_vec, subln_g, *, tq, seq, ctx, n_lat, nb, aw):
    nh = aw // V_DIM
    kcol = aw // V_DIM
    vcol = 2 * aw // V_DIM
    ctx0 = n_lat // ctx
    qt = seq // tq
    nk = seq + ctx
    g = subln_g.reshape(1, V_DIM)
    smem = pl.BlockSpec(memory_space=pltpu.SMEM)

    def chain_scratch(rows, half, n):
        n_chain = 2 * (rows // half)
        return ([pltpu.VMEM((half, n), F32)] * n_chain) + ([pltpu.VMEM((half, n), BF16)] * n_chain)

    half = _tile(ATTN_HALF, tq)
    lat = pl.pallas_call(
        functools.partial(_attn_lat_kernel, seq=seq, ctx=ctx, half=half),
        grid=(nb, nh, qt),
        in_specs=[
            smem,
            pl.BlockSpec((tq, V_DIM), lambda b, h, t: (b * qt + t, h)),
            pl.BlockSpec((seq, V_DIM), lambda b, h, t: (b, kcol + h)),
            pl.BlockSpec((ctx, V_DIM), lambda b, h, t: (ctx0 + b, kcol + h)),
            pl.BlockSpec((seq, V_DIM), lambda b, h, t: (b, vcol + h)),
            pl.BlockSpec((ctx, V_DIM), lambda b, h, t: (ctx0 + b, vcol + h)),
            pl.BlockSpec((1, V_DIM), lambda b, h, t: (0, 0)),
        ],
        out_specs=pl.BlockSpec((tq, V_DIM), lambda b, h, t: (b * qt + t, h)),
        out_shape=jax.ShapeDtypeStruct((n_lat, aw), BF16),
        scratch_shapes=[pltpu.VMEM((nk, V_DIM), BF16), pltpu.VMEM((nk, 2 * V_DIM), BF16)]
        + chain_scratch(tq, half, nk),
        compiler_params=_params("arbitrary", "arbitrary", "arbitrary"),
        name="attn_latent",
    )(lam_vec, qkv, qkv, qkv, qkv, qkv, g)

    half_c = _tile(ATTN_HALF, ctx)
    cx = pl.pallas_call(
        functools.partial(_attn_ctx_kernel, ctx=ctx, half=half_c),
        grid=(nb, nh),
        in_specs=[
            smem,
            pl.BlockSpec((ctx, V_DIM), lambda b, h: (ctx0 + b, h)),
            pl.BlockSpec((ctx, V_DIM), lambda b, h: (ctx0 + b, kcol + h)),
            pl.BlockSpec((ctx, V_DIM), lambda b, h: (ctx0 + b, vcol + h)),
            pl.BlockSpec((1, V_DIM), lambda b, h: (0, 0)),
        ],
        out_specs=pl.BlockSpec((ctx, V_DIM), lambda b, h: (b, h)),
        out_shape=jax.ShapeDtypeStruct((nb * ctx, aw), BF16),
        scratch_shapes=[pltpu.VMEM((ctx, 2 * V_DIM), BF16)] + chain_scratch(ctx, half_c, ctx),
        compiler_params=_params("arbitrary", "arbitrary"),
        name="attn_context",
    )(lam_vec, qkv, qkv, qkv, g)
    return lat, cx


def _top2_of4(a, b, c, d):
    m01, n01 = jnp.maximum(a, b), jnp.minimum(a, b)
    m23, n23 = jnp.maximum(c, d), jnp.minimum(c, d)
    return jnp.maximum(m01, m23) + jnp.maximum(jnp.minimum(m01, m23), jnp.maximum(n01, n23))


def _route(logits_t, bias_ref):
    n_exp = logits_t.shape[0]
    per = n_exp // N_GROUPS
    s = [jax.nn.sigmoid(logits_t[e:e + 1, :]) for e in range(n_exp)]
    sb = [s[e] + bias_ref[e:e + 1, :] for e in range(n_exp)]
    gscore = [_top2_of4(*sb[g * per:(g + 1) * per]) for g in range(N_GROUPS)]
    best, gidx = gscore[0], jnp.zeros_like(gscore[0], dtype=jnp.int32)
    for g in range(1, N_GROUPS):
        better = gscore[g] > best
        gidx = jnp.where(better, g, gidx)
        best = jnp.where(better, gscore[g], best)
    cand_b, cand_s = [], []
    for jx in range(per):
        vb, vs = sb[jx], s[jx]
        for g in range(1, N_GROUPS):
            sel = gidx == g
            vb = jnp.where(sel, sb[g * per + jx], vb)
            vs = jnp.where(sel, s[g * per + jx], vs)
        cand_b.append(vb)
        cand_s.append(vs)

    def argmax_first(vals, exclude):
        bv = bi = bs = None
        for jx in range(per):
            v = vals[jx] if exclude is None else jnp.where(exclude == jx, -jnp.inf, vals[jx])
            if bv is None:
                bv, bi, bs = v, jnp.zeros_like(gidx), cand_s[jx]
            else:
                better = v > bv
                bi = jnp.where(better, jx, bi)
                bs = jnp.where(better, cand_s[jx], bs)
                bv = jnp.where(better, v, bv)
        return bi, bs

    j1, w1 = argmax_first(cand_b, None)
    j2, w2 = argmax_first(cand_b, j1)
    tot = w1 + w2
    return gidx * per + j1, gidx * per + j2, w1 / tot, w2 / tot


def _merge_kernel(x_ref, attn_lat_ref, attn_ctx_ref, rest_ref, ccp_ref, cxp_ref, ccn_ref, cxn_ref, mod_ref,
                  convw_ref, convb_ref, wab_ref, wcb_ref, wo_ref, g2_ref, wrh_ref, wrl_ref,
                  xo_ref, h2_ref, logit_ref, *, tm, seq, ctx, n_lat, nb, d, cw):
    i = pl.program_id(0)
    r0 = i * tm
    is_lat = r0 < n_lat
    seg = _segment(r0, seq, n_lat, nb)
    pos = jnp.where(is_lat, r0 % seq, (r0 - n_lat) % ctx)
    slen = jnp.where(is_lat, seq, ctx)
    has_prev = (pos > 0).astype(F32)
    has_next = (pos + tm < slen).astype(F32)

    cb = rest_ref[:, 0:cw].astype(F32)
    u = rest_ref[:, cw:2 * cw].astype(F32) * rest_ref[:, 2 * cw:3 * cw].astype(F32)
    ga = rest_ref[:, 3 * cw:3 * cw + d].astype(F32)
    gc = rest_ref[:, 3 * cw + d:3 * cw + 2 * d].astype(F32)
    hl = HALO_ROWS - 1
    halo_prev = ccp_ref[hl:hl + 1, :].astype(F32) * cxp_ref[hl:hl + 1, :].astype(F32) * has_prev
    halo_next = ccn_ref[0:1, :].astype(F32) * cxn_ref[0:1, :].astype(F32) * has_next
    rid = lax.broadcasted_iota(jnp.int32, u.shape, 0)
    u_prev = jnp.where(rid == 0, halo_prev, pltpu.roll(u, 1, 0))
    u_next = jnp.where(rid == tm - 1, halo_next, pltpu.roll(u, tm - 1, 0))
    y = cb * (u_prev * convw_ref[0:1, :] + u * convw_ref[1:2, :] + u_next * convw_ref[2:3, :]
              + convb_ref[...])

    attn = jnp.where(is_lat, attn_lat_ref[...], attn_ctx_ref[...])
    ma = jnp.dot(attn, wab_ref[0], preferred_element_type=F32)
    mc = jnp.dot(y.astype(BF16), wcb_ref[0], preferred_element_type=F32)
    merged = jax.nn.sigmoid(ga) * ma + jax.nn.sigmoid(gc) * mc
    out = jnp.dot(merged.astype(BF16), wo_ref[0], preferred_element_type=F32)

    m = mod_ref[pl.ds(seg, 1), :]
    xn = x_ref[...] + m[:, 2 * d:3 * d] * out
    xo_ref[...] = xn
    h2 = _adaln(xn, g2_ref, mod_ref, seg, 3, d)
    h2_ref[...] = _pack_halves(h2)

    h_hi = h2.astype(BF16)
    h_lo = (h2 - h_hi.astype(F32)).astype(BF16)
    w_hi = wrh_ref[...]
    logits = (jnp.dot(h_hi, w_hi, preferred_element_type=F32)
              + jnp.dot(h_lo, w_hi, preferred_element_type=F32)
              + jnp.dot(h_hi, wrl_ref[...], preferred_element_type=F32))
    logit_ref[...] = logits.T[0:logit_ref.shape[0], :]


def _route_kernel(logit_ref, rbias_ref, eidx_ref, ew_ref):
    e1, e2, w1, w2 = _route(logit_ref[...], rbias_ref)
    eidx_ref[...] = jnp.zeros(eidx_ref.shape, jnp.int32)
    ew_ref[...] = jnp.zeros(ew_ref.shape, F32)
    eidx_ref[0:1, :] = e1
    eidx_ref[1:2, :] = e2
    ew_ref[0:1, :] = w1
    ew_ref[1:2, :] = w2


def _route_tokens(logits_t, router_bias):
    n_exp, n_rows = logits_t.shape
    tr = _tile(2048, n_rows)
    out_spec = pl.BlockSpec((SUBLANES, tr), lambda i: (0, i))
    return pl.pallas_call(
        _route_kernel,
        grid=(n_rows // tr,),
        in_specs=[pl.BlockSpec((n_exp, tr), lambda i: (0, i)), pl.BlockSpec((n_exp, 1), lambda i: (0, 0))],
        out_specs=[out_spec, out_spec],
        out_shape=[jax.ShapeDtypeStruct((SUBLANES, n_rows), jnp.int32),
                   jax.ShapeDtypeStruct((SUBLANES, n_rows), F32)],
        compiler_params=_params("arbitrary"),
        name="route",
    )(logits_t, router_bias.reshape(n_exp, 1))


def _merge(xa, attn_lat, attn_ctx, rest, mod_l, conv_w, conv_b, w_ab, w_cb, w_o, layer, g2, w_router,
           *, n_rows, tm, seq, ctx, n_lat, nb):
    nt, d = xa.shape
    aw = attn_lat.shape[1]
    lat_tiles = n_lat // tm
    ctx_tiles = attn_ctx.shape[0] // tm
    cw = conv_w.shape[1]
    n_exp = w_router.shape[1]
    wr_pad = jnp.pad(w_router, ((0, 0), (0, LANES - n_exp)))
    wr_hi = wr_pad.astype(BF16)
    wr_lo = (wr_pad - wr_hi.astype(F32)).astype(BF16)
    n_rest = rest.shape[1]
    hb = tm // HALO_ROWS
    last_hblk = nt // HALO_ROWS - 1

    def prev_idx(col):
        return lambda i: (jnp.maximum(i * hb - 1, 0), col)

    def next_idx(col):
        return lambda i: (jnp.minimum((i + 1) * hb, last_hblk), col)

    const = lambda i: (0, 0)
    kern = functools.partial(_merge_kernel, tm=tm, seq=seq, ctx=ctx, n_lat=n_lat, nb=nb, d=d, cw=cw)
    return pl.pallas_call(
        kern,
        grid=(n_rows // tm,),
        in_specs=[
            pl.BlockSpec((tm, d), lambda i: (i, 0)),
            pl.BlockSpec((tm, aw), lambda i: (jnp.minimum(i, lat_tiles - 1), 0)),
            pl.BlockSpec((tm, aw), lambda i: (jnp.clip(i - lat_tiles, 0, ctx_tiles - 1), 0)),
            pl.BlockSpec((tm, n_rest), lambda i: (i, 0)),
            pl.BlockSpec((HALO_ROWS, cw), prev_idx(1)),
            pl.BlockSpec((HALO_ROWS, cw), prev_idx(2)),
            pl.BlockSpec((HALO_ROWS, cw), next_idx(1)),
            pl.BlockSpec((HALO_ROWS, cw), next_idx(2)),
            pl.BlockSpec((MOD_ROWS, 6 * d), const),
            pl.BlockSpec((3, cw), const),
            pl.BlockSpec((1, cw), const),
            pl.BlockSpec((1,) + w_ab.shape[1:], lambda i: (layer, 0, 0)),
            pl.BlockSpec((1,) + w_cb.shape[1:], lambda i: (layer, 0, 0)),
            pl.BlockSpec((1,) + w_o.shape[1:], lambda i: (layer, 0, 0)),
            pl.BlockSpec((1, d), const),
            pl.BlockSpec((d, LANES), const),
            pl.BlockSpec((d, LANES), const),
        ],
        out_specs=[
            pl.BlockSpec((tm, d), lambda i: (i, 0)),
            pl.BlockSpec((tm, d // 2), lambda i: (i, 0)),
            pl.BlockSpec((n_exp, tm), lambda i: (0, i)),
        ],
        out_shape=[
            jax.ShapeDtypeStruct((n_rows, d), F32),
            jax.ShapeDtypeStruct((n_rows, d // 2), U32),
            jax.ShapeDtypeStruct((n_exp, n_rows), F32),
        ],
        compiler_params=_params("arbitrary"),
        name="merge",
    )(xa, attn_lat, attn_ctx, rest, rest, rest, rest, rest, mod_l, conv_w, conv_b.reshape(1, cw), w_ab, w_cb, w_o,
      g2.reshape(1, d), wr_hi, wr_lo)


W_CHUNKS = 4
N_STREAM = 3 * W_CHUNKS


def _weight_stream_schedule(tile_expert, n_active):
    n_tiles = tile_expert.shape[0]
    t = jnp.arange(n_tiles, dtype=jnp.int32)
    active = t < n_active
    prev = jnp.concatenate([tile_expert[:1] - 1, tile_expert[:-1]])
    first = active & (tile_expert != prev)
    gid = jnp.cumsum(first.astype(jnp.int32)) - 1
    group_start = lax.cummax(jnp.where(first, t, 0))
    same = (gid[:, None] == gid[None, :]) & active[None, :]
    size = jnp.maximum(jnp.sum(same.astype(jnp.int32), axis=1), 1)
    pos = t - group_start
    nxt = group_start + size
    has_next = active & (nxt < n_active)
    next_expert = jnp.where(has_next, tile_expert[jnp.minimum(nxt, n_tiles - 1)], 0)
    base, rem = N_STREAM // size, N_STREAM % size
    quota = jnp.where(has_next, base + (pos < rem).astype(jnp.int32), 0)
    chunk0 = pos * base + jnp.minimum(pos, rem)
    return tuple(a.astype(jnp.int32) for a in (gid % 2, next_expert, chunk0, quota))


def _dispatch_plan(eidx, n_tok, tg, n_exp):
    ef = eidx[:2].reshape(-1)
    n2 = 2 * n_tok
    onehot = (ef[:, None] == jnp.arange(n_exp)[None, :]).astype(jnp.int32)
    csum = jnp.cumsum(onehot, axis=0)
    rank = jnp.sum(onehot * csum, axis=1) - 1
    counts = csum[-1]
    padded = ((counts + tg - 1) // tg) * tg
    gend = jnp.cumsum(padded)
    gstart = gend - padded
    dest = jnp.sum(onehot * gstart[None, :], axis=1) + rank
    n_slots = -(-n2 // tg) * tg + n_exp * tg
    tile_start = jnp.arange(n_slots // tg, dtype=jnp.int32) * tg
    tile_expert = jnp.sum((gend[None, :] <= tile_start[:, None]).astype(jnp.int32), axis=1)
    tile_expert = jnp.minimum(tile_expert, n_exp - 1).astype(jnp.int32)
    n_active = (gend[-1] // tg).astype(jnp.int32).reshape(1)
    slots = dest.reshape(2, n_tok).astype(jnp.int32)
    pad_start = jnp.concatenate([gstart + counts, gend[-1:]]).astype(jnp.int32)
    pad_count = jnp.concatenate([padded - counts, n_slots - gend[-1:]]).astype(jnp.int32)
    stream = _weight_stream_schedule(tile_expert, n_active[0])
    return slots, pad_start, pad_count, tile_expert, n_active, n_slots, stream


def _dispatch_kernel(d1_ref, d2_ref, ps_ref, pc_ref, h2_ref, xs_hbm, xbuf, zbuf, sem, zsem, *, tm, n_pad):
    i = pl.program_id(0)
    last = pl.num_programs(0) - 1
    slot = i % 2

    def wait_rows(s):
        for _ in range(2):
            pltpu.make_async_copy(xbuf.at[s], xs_hbm.at[pl.ds(0, tm), :], sem.at[s]).wait()

    def zero_copy(row):
        return pltpu.make_async_copy(zbuf.at[pl.ds(0, 1), :], xs_hbm.at[pl.ds(row, 1), :], zsem)

    @pl.when(i == 0)
    def _():
        zbuf[...] = jnp.zeros(zbuf.shape, zbuf.dtype)
        for e in range(n_pad):
            def start_zero(r, carry, e=e):
                zero_copy(ps_ref[e] + r).start()
                return carry
            lax.fori_loop(0, pc_ref[e], start_zero, 0)

    @pl.when(i >= 2)
    def _():
        wait_rows(slot)

    xbuf[slot] = h2_ref[...]

    def scatter(r, carry):
        for d_ref in (d1_ref, d2_ref):
            pltpu.make_async_copy(xbuf.at[slot, pl.ds(r, 1), :], xs_hbm.at[pl.ds(d_ref[i * tm + r], 1), :],
                                  sem.at[slot]).start()
        return carry

    lax.fori_loop(0, tm, scatter, 0, unroll=8)

    @pl.when(i == last)
    def _():
        wait_rows(slot)
        wait_rows(1 - slot)
        for e in range(n_pad):
            def wait_zero(r, carry):
                zero_copy(0).wait()
                return carry
            lax.fori_loop(0, pc_ref[e], wait_zero, 0)


def _dispatch(h2, plan, *, tm):
    slots, pad_start, pad_count, _, _, n_slots = plan[:6]
    n_tok, d = h2.shape
    assert n_tok // tm >= 2
    grid_spec = pltpu.PrefetchScalarGridSpec(
        num_scalar_prefetch=4,
        grid=(n_tok // tm,),
        in_specs=[pl.BlockSpec((tm, d), lambda i, *_: (i, 0))],
        out_specs=pl.BlockSpec(memory_space=pl.ANY),
        scratch_shapes=[pltpu.VMEM((2, tm, d), h2.dtype), pltpu.VMEM((SUBLANES, d), h2.dtype),
                        pltpu.SemaphoreType.DMA((2,)), pltpu.SemaphoreType.DMA(())],
    )
    return pl.pallas_call(
        functools.partial(_dispatch_kernel, tm=tm, n_pad=pad_start.shape[0]),
        grid_spec=grid_spec,
        out_shape=jax.ShapeDtypeStruct((n_slots, d), h2.dtype),
        compiler_params=_params("arbitrary"),
        name="dispatch",
    )(slots[0], slots[1], pad_start, pad_count, h2)


def _expert_kernel(te_ref, na_ref, par_ref, ne_ref, c0_ref, nq_ref, x_ref, wg_hbm, wu_hbm, wd_hbm, y_ref,
                   wg_s, wu_s, wd_s, stg_a, stg_b, sem_a, sem_b, *, layer):
    t = pl.program_id(0)
    d, f = wg_s.shape[1:]
    rows_a, rows_b = d // W_CHUNKS, f // W_CHUNKS

    def by_matrix(c, fn):
        if isinstance(c, int):
            fn(c // W_CHUNKS, c % W_CHUNKS)
            return
        for m in range(3):
            @pl.when((c >= m * W_CHUNKS) & (c < (m + 1) * W_CHUNKS))
            def _(m=m):
                fn(m, c - m * W_CHUNKS)

    def chunk_copy(e, m, j, s):
        if m < 2:
            src = (wg_hbm, wu_hbm)[m].at[layer, e, pl.ds(pl.multiple_of(j * rows_a, rows_a), rows_a), :]
            return pltpu.make_async_copy(src, stg_a.at[s], sem_a.at[s])
        src = wd_hbm.at[layer, e, pl.ds(pl.multiple_of(j * rows_b, rows_b), rows_b), :]
        return pltpu.make_async_copy(src, stg_b.at[s], sem_b.at[s])

    def start(e, c):
        by_matrix(c, lambda m, j: chunk_copy(e, m, j, c % 2).start())

    def convert(e, c, w):
        def fn(m, j):
            s = c % 2
            chunk_copy(e, m, j, s).wait()
            if m < 2:
                dst = (wg_s, wu_s)[m]
                dst[w, pl.ds(pl.multiple_of(j * rows_a, rows_a), rows_a), :] = stg_a[s].astype(BF16)
            else:
                wd_s[w, pl.ds(pl.multiple_of(j * rows_b, rows_b), rows_b), :] = stg_b[s].astype(BF16)
        by_matrix(c, fn)

    @pl.when(t == 0)
    def _():
        e0 = te_ref[0]
        start(e0, 0)
        start(e0, 1)
        for c in range(N_STREAM):
            convert(e0, c, 0)
            if c + 2 < N_STREAM:
                start(e0, c + 2)

    @pl.when(t < na_ref[0])
    def _():
        e_next, c0, n_conv = ne_ref[t], c0_ref[t], nq_ref[t]
        w_cur = par_ref[t]

        @pl.when((n_conv > 0) & (c0 == 0))
        def _():
            start(e_next, 0)
            start(e_next, 1)

        d2 = x_ref.shape[1]
        x_lo, x_hi = (h.astype(BF16) for h in _unpack_halves(x_ref[...]))
        gte = (jnp.dot(x_lo, wg_s[w_cur, 0:d2, :], preferred_element_type=F32)
               + jnp.dot(x_hi, wg_s[w_cur, d2:2 * d2, :], preferred_element_type=F32))
        up = (jnp.dot(x_lo, wu_s[w_cur, 0:d2, :], preferred_element_type=F32)
              + jnp.dot(x_hi, wu_s[w_cur, d2:2 * d2, :], preferred_element_type=F32))
        he = (gte * jax.nn.sigmoid(gte) * up).astype(BF16)
        y_ref[...] = _pack_halves(jnp.dot(he, wd_s[w_cur], preferred_element_type=F32))

        def convert_next(k, carry):
            c = c0 + k
            convert(e_next, c, 1 - w_cur)

            @pl.when(c + 2 < N_STREAM)
            def _():
                start(e_next, c + 2)
            return carry

        lax.fori_loop(0, n_conv, convert_next, 0)

    @pl.when(t >= na_ref[0])
    def _():
        y_ref[...] = jnp.zeros(y_ref.shape, U32)


def _experts(x_sorted, plan, wg, wu, wd, layer, *, tg):
    tile_expert, n_active, n_slots, stream = plan[3:7]
    d2 = x_sorted.shape[1]
    d = 2 * d2
    f = wg.shape[3]
    hbm = pl.BlockSpec(memory_space=pl.ANY)
    grid_spec = pltpu.PrefetchScalarGridSpec(
        num_scalar_prefetch=6,
        grid=(n_slots // tg,),
        in_specs=[pl.BlockSpec((tg, d2), lambda t, *_: (t, 0)), hbm, hbm, hbm],
        out_specs=pl.BlockSpec((tg, d2), lambda t, *_: (t, 0)),
        scratch_shapes=[
            pltpu.VMEM((2, d, f), BF16), pltpu.VMEM((2, d, f), BF16), pltpu.VMEM((2, f, d), BF16),
            pltpu.VMEM((2, d // W_CHUNKS, f), F32), pltpu.VMEM((2, f // W_CHUNKS, d), F32),
            pltpu.SemaphoreType.DMA((2,)), pltpu.SemaphoreType.DMA((2,)),
        ],
    )
    return pl.pallas_call(
        functools.partial(_expert_kernel, layer=layer),
        grid_spec=grid_spec,
        out_shape=jax.ShapeDtypeStruct((n_slots, d2), U32),
        compiler_params=_params("arbitrary"),
        name="experts",
    )(tile_expert, n_active, *stream, x_sorted, wg, wu, wd)


def _combine_kernel(s1_ref, s2_ref, x_ref, wt_ref, y_hbm, mod_ref, g_ref, modn_ref, xo_ref, *refs,
                    tm, seq, n_lat, nb, d, final):
    buf, sem = refs[-2], refs[-1]
    i = pl.program_id(0)
    last = pl.num_programs(0) - 1
    base = i * tm
    slot = i % 2

    def issue(tile, dst_slot):
        def body(r, carry):
            for k, s_ref in enumerate((s1_ref, s2_ref)):
                pltpu.make_async_copy(y_hbm.at[pl.ds(s_ref[tile * tm + r], 1), :],
                                      buf.at[dst_slot, k, pl.ds(r, 1), :], sem.at[dst_slot]).start()
            return carry
        lax.fori_loop(0, tm, body, 0, unroll=8)

    def wait(dst_slot):
        for k in range(2):
            pltpu.make_async_copy(y_hbm.at[pl.ds(0, tm), :], buf.at[dst_slot, k], sem.at[dst_slot]).wait()

    @pl.when(i == 0)
    def _():
        issue(0, 0)

    @pl.when(i < last)
    def _():
        issue(i + 1, 1 - slot)

    wait(slot)
    seg = _segment(base, seq, n_lat, nb)
    m = mod_ref[pl.ds(seg, 1), :]
    y1_lo, y1_hi = _unpack_halves(buf[slot, 0])
    y2_lo, y2_hi = _unpack_halves(buf[slot, 1])
    w1, w2 = wt_ref[:, 0:1], wt_ref[:, 1:2]
    moe = jnp.concatenate([w1 * y1_lo + w2 * y2_lo, w1 * y1_hi + w2 * y2_hi], axis=1)
    xn = x_ref[...] + m[:, 5 * d:6 * d] * moe
    if final:
        xo_ref[...] = _rms(xn) * g_ref[...]
    else:
        xo_ref[...] = xn
        refs[0][...] = _adaln(xn, g_ref, modn_ref, seg, 0, d).astype(BF16)


def _combine(xa, y_sorted, slots, w_tok, mod_l, g_next, mod_next, *, tm, seq, n_lat, nb, final):
    n_rows, d = xa.shape
    row_spec = pl.BlockSpec((tm, d), lambda i, s1, s2: (i, 0))
    mod_spec = pl.BlockSpec((MOD_ROWS, 6 * d), lambda i, s1, s2: (0, 0))
    out_specs = [row_spec]
    out_shape = [jax.ShapeDtypeStruct((n_rows, d), F32)]
    if not final:
        out_specs.append(row_spec)
        out_shape.append(jax.ShapeDtypeStruct((n_rows, d), BF16))
    grid_spec = pltpu.PrefetchScalarGridSpec(
        num_scalar_prefetch=2,
        grid=(n_rows // tm,),
        in_specs=[
            row_spec,
            pl.BlockSpec((tm, 2), lambda i, s1, s2: (i, 0)),
            pl.BlockSpec(memory_space=pl.ANY),
            mod_spec,
            pl.BlockSpec((1, d), lambda i, s1, s2: (0, 0)),
            mod_spec,
        ],
        out_specs=out_specs,
        scratch_shapes=[pltpu.VMEM((2, 2, tm, d // 2), U32), pltpu.SemaphoreType.DMA((2,))],
    )
    kern = functools.partial(_combine_kernel, tm=tm, seq=seq, n_lat=n_lat, nb=nb, d=d, final=final)
    out = pl.pallas_call(
        kern,
        grid_spec=grid_spec,
        out_shape=out_shape,
        compiler_params=_params("arbitrary"),
        name="combine",
    )(slots[0], slots[1], xa, w_tok, y_sorted, mod_l, g_next.reshape(1, d), mod_next)
    return (out[0], None) if final else (out[0], out[1])


def _tile(limit, *sizes):
    t = limit
    while any(s % t for s in sizes):
        t //= 2
    return t


def kernel(x, c, ctx, c_ctx, w_mod, b_mod, norm1_g, norm2_g, w_in, lambda_q1, lambda_k1, lambda_q2,
           lambda_k2, subln_g, conv_w, conv_b, w_attn_branch, w_conv_branch, w_out, w_router,
           router_bias, w_exp_gate, w_exp_up, w_exp_down, final_g):
    nb, seq, d = x.shape
    n_ctx_tok = ctx.shape[1]
    depth = w_mod.shape[0]
    n_exp = w_router.shape[1]
    aw = d // 2
    n_lat = nb * seq
    nt = n_lat + nb * n_ctx_tok
    assert nb < MOD_ROWS and seq % GRID_W == 0 and aw % V_DIM == 0 and n_exp % N_GROUPS == 0

    tm_in = _tile(1024, seq, nb * n_ctx_tok)
    tm_mg = _tile(256, seq, n_ctx_tok)
    tq = _tile(512, seq)
    tg = _tile(256, tm_mg)

    cvec = jnp.zeros((MOD_ROWS, d), F32).at[:nb].set(c).at[nb].set(c_ctx)
    mod = _modulation(cvec, w_mod, b_mod)
    rope = _rope_tables(seq, tm_in)
    xa, h = _prenorm(x.reshape(n_lat, d), ctx.reshape(nb * n_ctx_tok, d), mod[0], norm1_g[0],
                     tm=tm_mg, seq=seq, nb=nb)
    w_ab, w_cb, w_o = (w.astype(BF16) for w in (w_attn_branch, w_conv_branch, w_out))

    for l in range(depth):
        last = l == depth - 1
        lam_init = 0.8 - 0.6 * math.exp(-0.3 * l)
        lam = (jnp.exp(jnp.sum(lambda_q1[l] * lambda_k1[l])) - jnp.exp(jnp.sum(lambda_q2[l] * lambda_k2[l]))
               + lam_init)
        lam_vec = jnp.stack([lam, jnp.asarray(1.0 - lam_init, F32)]).astype(F32)

        qkv, rest = _in_proj(h, w_in, l, rope, tm=tm_in, seq=seq, n_lat=n_lat)
        attn_lat, attn_ctx = _attention(qkv, lam_vec, subln_g[l], tq=tq, seq=seq, ctx=n_ctx_tok, n_lat=n_lat,
                                        nb=nb, aw=aw)
        n_rows = n_lat if last else nt
        xa, h2, logits_t = _merge(
            xa, attn_lat, attn_ctx, rest, mod[l], conv_w[l], conv_b[l], w_ab, w_cb, w_o, l, norm2_g[l],
            w_router, n_rows=n_rows, tm=tm_mg, seq=seq, ctx=n_ctx_tok, n_lat=n_lat, nb=nb)
        eidx, ew = _route_tokens(logits_t, router_bias)
        plan = _dispatch_plan(eidx, n_rows, tg, n_exp)
        x_sorted = _dispatch(h2, plan, tm=tm_mg)
        y_sorted = _experts(x_sorted, plan, w_exp_gate, w_exp_up, w_exp_down, l, tg=tg)
        g_next, mod_next = (final_g, mod[l]) if last else (norm1_g[l + 1], mod[l + 1])
        xa, h = _combine(xa, y_sorted, plan[0], ew[:2].T, mod[l], g_next, mod_next, tm=tm_mg, seq=seq,
                         n_lat=n_lat, nb=nb, final=last)

    return xa.reshape(nb, seq, d)
```

```python
import functools
import math

import jax
import jax.numpy as jnp
from jax import lax
from jax.experimental import pallas as pl
from jax.experimental.pallas import tpu as pltpu

HEAD_DIM = 64
V_DIM = 2 * HEAD_DIM
GRID_W = 64
ROPE_THETA = 10000.0
ROPE_AXIS_DIM = HEAD_DIM // 2
N_GROUPS = 4
EPS = 1e-6
LANES = 128
SUBLANES = 8
HALO_ROWS = 2 * SUBLANES
VMEM_LIMIT_BYTES = 56 * 1024 * 1024
MOD_ROWS = 8
LOG2E = math.log2(math.e)

F32 = jnp.float32
BF16 = jnp.bfloat16
HIGHEST = lax.Precision.HIGHEST
NT_DIMS = (((1,), (1,)), ((), ()))


def _params(*sem):
    return pltpu.CompilerParams(dimension_semantics=sem, vmem_limit_bytes=VMEM_LIMIT_BYTES)


def _rms(x):
    return x * lax.rsqrt(jnp.mean(x * x, axis=-1, keepdims=True) + EPS)


U32 = jnp.uint32
HIGH_HALF = 0xFFFF0000


def _pack_halves(x):
    c2 = x.shape[1] // 2
    lo = lax.bitcast_convert_type(x[:, 0:c2].astype(BF16).astype(F32), U32) >> 16
    hi = lax.bitcast_convert_type(x[:, c2:2 * c2].astype(BF16).astype(F32), U32) & U32(HIGH_HALF)
    return hi | lo


def _unpack_halves(w):
    return (lax.bitcast_convert_type(w << 16, F32), lax.bitcast_convert_type(w & U32(HIGH_HALF), F32))


def _segment(row0, seq, n_lat, nb):
    return jnp.where(row0 < n_lat, row0 // seq, nb)


def _adaln(x, g_ref, mod_ref, seg, shift_col, d):
    m = mod_ref[pl.ds(seg, 1), :]
    return _rms(x) * g_ref[...] * (1.0 + m[:, (shift_col + 1) * d:(shift_col + 2) * d]) + m[:, shift_col * d:(shift_col + 1) * d]


def _mod_kernel(c_ref, w_ref, b_ref, o_ref):
    c = c_ref[...]
    sc = c * jax.nn.sigmoid(c)
    o_ref[0] = jnp.dot(sc, w_ref[0], precision=HIGHEST, preferred_element_type=F32) + b_ref[0]


def _modulation(cvec, w_mod, b_mod):
    depth, d, n6 = w_mod.shape
    tn = _tile(1024, n6)
    return pl.pallas_call(
        _mod_kernel,
        grid=(depth, n6 // tn),
        in_specs=[
            pl.BlockSpec((MOD_ROWS, d), lambda l, j: (0, 0)),
            pl.BlockSpec((1, d, tn), lambda l, j: (l, 0, j)),
            pl.BlockSpec((1, 1, tn), lambda l, j: (l, 0, j)),
        ],
        out_specs=pl.BlockSpec((1, MOD_ROWS, tn), lambda l, j: (l, 0, j)),
        out_shape=jax.ShapeDtypeStruct((depth, MOD_ROWS, n6), F32),
        compiler_params=_params("arbitrary", "arbitrary"),
        name="modulation",
    )(cvec, w_mod, b_mod.reshape(depth, 1, n6))


def _prenorm_kernel(x_ref, c_ref, mod_ref, g_ref, xa_ref, h_ref, *, tm, seq, n_lat, nb, d):
    r0 = pl.program_id(0) * tm
    x = jnp.where(r0 < n_lat, x_ref[...], c_ref[...])
    xa_ref[...] = x
    h_ref[...] = _adaln(x, g_ref, mod_ref, _segment(r0, seq, n_lat, nb), 0, d).astype(BF16)


def _prenorm(x2, c2, mod_l, g1, *, tm, seq, nb):
    n_lat, d = x2.shape
    nt = n_lat + c2.shape[0]
    lat_tiles = n_lat // tm
    ctx_tiles = c2.shape[0] // tm
    row_spec = pl.BlockSpec((tm, d), lambda i: (i, 0))
    return pl.pallas_call(
        functools.partial(_prenorm_kernel, tm=tm, seq=seq, n_lat=n_lat, nb=nb, d=d),
        grid=(nt // tm,),
        in_specs=[
            pl.BlockSpec((tm, d), lambda i: (jnp.minimum(i, lat_tiles - 1), 0)),
            pl.BlockSpec((tm, d), lambda i: (jnp.clip(i - lat_tiles, 0, ctx_tiles - 1), 0)),
            pl.BlockSpec((MOD_ROWS, 6 * d), lambda i: (0, 0)),
            pl.BlockSpec((1, d), lambda i: (0, 0)),
        ],
        out_specs=[row_spec, row_spec],
        out_shape=[jax.ShapeDtypeStruct((nt, d), F32), jax.ShapeDtypeStruct((nt, d), BF16)],
        compiler_params=_params("arbitrary"),
        name="prenorm",
    )(x2, c2, mod_l, g1.reshape(1, d))


def _in_kernel(h_ref, w_ref, *refs, rope, aw):
    o_ref, w_scr = refs[-2], refs[-1]
    j = pl.program_id(0)

    @pl.when(pl.program_id(1) == 0)
    def _():
        w_scr[...] = w_ref[0].astype(BF16)

    acc = jnp.dot(h_ref[...], w_scr[...], preferred_element_type=F32)
    if not rope:
        o_ref[...] = acc.astype(o_ref.dtype)
        return
    rc_ref, ra_ref, rb_ref = refs[:3]

    @pl.when(j < 2)
    def _():
        scale = jnp.where(j == 0, HEAD_DIM ** -0.5 * LOG2E, 1.0).astype(F32)
        rc, ra, rb = rc_ref[...], ra_ref[...], rb_ref[...]
        for c in range(aw // LANES):
            a = acc[:, c * LANES:(c + 1) * LANES]
            r = (a * rc + pltpu.roll(a, LANES - ROPE_AXIS_DIM // 2, 1) * ra
                 + pltpu.roll(a, ROPE_AXIS_DIM // 2, 1) * rb)
            o_ref[:, c * LANES:(c + 1) * LANES] = (r * scale).astype(BF16)

    @pl.when(j == 2)
    def _():
        o_ref[...] = acc.astype(BF16)


def _rope_tables(seq, tm):
    pos = jnp.arange(seq)
    row = (pos // GRID_W).astype(F32)
    col = (pos % GRID_W).astype(F32)
    inv = ROPE_THETA ** (-jnp.arange(0, ROPE_AXIS_DIM, 2, dtype=F32) / ROPE_AXIS_DIM)
    lane = jnp.arange(LANES)
    jj = lane % HEAD_DIM
    axis = jj // ROPE_AXIS_DIM
    r = jj % ROPE_AXIS_DIM
    f = r % (ROPE_AXIS_DIM // 2)
    half = r // (ROPE_AXIS_DIM // 2)
    posv = jnp.where(axis[None, :] == 0, row[:, None], col[:, None])
    ang = posv * inv[f][None, :]
    cos, sin = jnp.cos(ang), jnp.sin(ang)
    rc = jnp.concatenate([cos, jnp.ones((tm, LANES), F32)], axis=0)
    ra = jnp.concatenate([jnp.where(half[None, :] == 0, -sin, 0.0), jnp.zeros((tm, LANES), F32)], axis=0)
    rb = jnp.concatenate([jnp.where(half[None, :] == 1, sin, 0.0), jnp.zeros((tm, LANES), F32)], axis=0)
    return rc, ra, rb


def _in_proj(h, w_in, layer, rope, *, tm, seq, n_lat):
    nt, d = h.shape
    in_cols = w_in.shape[2]
    aw = d // 2
    n_qkv = 3
    n_rest = in_cols // aw - n_qkv
    n_lat_tiles = n_lat // tm
    seq_tiles = seq // tm

    def rope_idx(j, i):
        return (jnp.where(i < n_lat_tiles, i % seq_tiles, seq_tiles), 0)

    def call(col0, n_col, out_dtype, with_rope):
        in_specs = [
            pl.BlockSpec((tm, d), lambda j, i: (i, 0)),
            pl.BlockSpec((1, d, aw), lambda j, i: (layer, 0, col0 + j)),
        ]
        args = [h, w_in]
        if with_rope:
            in_specs += [pl.BlockSpec((tm, LANES), rope_idx)] * 3
            args += list(rope)
        return pl.pallas_call(
            functools.partial(_in_kernel, rope=with_rope, aw=aw),
            grid=(n_col, nt // tm),
            in_specs=in_specs,
            out_specs=pl.BlockSpec((tm, aw), lambda j, i: (i, j)),
            out_shape=jax.ShapeDtypeStruct((nt, n_col * aw), out_dtype),
            scratch_shapes=[pltpu.VMEM((d, aw), BF16)],
            compiler_params=_params("arbitrary", "arbitrary"),
            name="in_proj_qkv" if with_rope else "in_proj_rest",
        )(*args)

    return call(0, n_qkv, BF16, True), call(n_qkv, n_rest, BF16, False)


EXP_CHUNK = 128
ATTN_HALF = 512


def _split_maps(q_ref, half):
    q_maps = []
    for hh in range(q_ref.shape[0] // half):
        q = q_ref[hh * half:(hh + 1) * half, :]
        lane = lax.broadcasted_iota(jnp.int32, q.shape, 1)
        zero = jnp.zeros_like(q)
        q_maps += [jnp.where(lane < HEAD_DIM, q, zero), jnp.where(lane >= HEAD_DIM, q, zero)]
    return q_maps


def _finish_heads(outs, lam_ref, g_ref, o_ref, half):
    for hh in range(len(outs) // 2):
        a, b = outs[2 * hh], outs[2 * hh + 1]
        o1 = a[:, 0:V_DIM] / a[:, V_DIM:V_DIM + 1]
        o2 = b[:, 0:V_DIM] * (lam_ref[0] / b[:, V_DIM:V_DIM + 1])
        o = _rms(o1 - o2) * g_ref[...] * lam_ref[1]
        o_ref[hh * half:(hh + 1) * half, :] = o.astype(BF16)


def _diff_attention(lam_ref, q_ref, k_ref, v_ref, g_ref, o_ref, s_refs, e_refs, k0, n, half):
    q_maps = _split_maps(q_ref, half)
    n_chain = len(q_maps)
    outs = [None] * n_chain

    def scores(i):
        s_refs[i][:, 0:n] = lax.dot_general(q_maps[i], k_ref[k0:k0 + n, :], NT_DIMS,
                                            preferred_element_type=F32)

    def exps(i):
        mx = jnp.max(s_refs[i][:, 0:n], axis=-1, keepdims=True)
        for c in range(n // EXP_CHUNK):
            sl = slice(c * EXP_CHUNK, (c + 1) * EXP_CHUNK)
            e_refs[i][:, sl] = jnp.exp2(s_refs[i][:, sl] - mx).astype(BF16)

    def values(i):
        outs[i] = jnp.dot(e_refs[i][:, 0:n], v_ref[k0:k0 + n, :], preferred_element_type=F32)

    scores(0)
    for i in range(n_chain):
        if i + 1 < n_chain:
            scores(i + 1)
        exps(i)
        if i >= 1:
            values(i - 1)
    values(n_chain - 1)
    _finish_heads(outs, lam_ref, g_ref, o_ref, half)


def _attn_lat_kernel(lam_ref, q_ref, kl_ref, kc_ref, vl_ref, vc_ref, g_ref, o_ref, k_scr, v_scr,
                     *chain_scr, seq, ctx, half):
    nk = seq + ctx

    @pl.when(pl.program_id(2) == 0)
    def _():
        k_scr[0:seq, :] = kl_ref[...]
        k_scr[seq:nk, :] = kc_ref[...]
        v_scr[0:seq, 0:V_DIM] = vl_ref[...]
        v_scr[seq:nk, 0:V_DIM] = vc_ref[...]
        v_scr[:, V_DIM:2 * V_DIM] = jnp.ones((nk, V_DIM), BF16)

    n_chain = len(chain_scr) // 2
    _diff_attention(lam_ref, q_ref, k_scr, v_scr, g_ref, o_ref, chain_scr[:n_chain], chain_scr[n_chain:],
                    0, nk, half)


def _attn_ctx_kernel(lam_ref, q_ref, k_ref, v_ref, g_ref, o_ref, v_scr, *chain_scr, ctx, half):
    v_scr[:, 0:V_DIM] = v_ref[...]
    v_scr[:, V_DIM:2 * V_DIM] = jnp.ones((ctx, V_DIM), BF16)
    n_chain = len(chain_scr) // 2
    _diff_attention(lam_ref, q_ref, k_ref, v_scr, g_ref, o_ref, chain_scr[:n_chain], chain_scr[n_chain:],
                    0, ctx, half)


def _attention(qkv, lam_vec, subln_g, *, tq, seq, ctx, n_lat, nb, aw):
    nh = aw // V_DIM
    kcol = aw // V_DIM
    vcol = 2 * aw // V_DIM
    ctx0 = n_lat // ctx
    qt = seq // tq
    nk = seq + ctx
    g = subln_g.reshape(1, V_DIM)
    smem = pl.BlockSpec(memory_space=pltpu.SMEM)

    def chain_scratch(rows, half, n):
        n_chain = 2 * (rows // half)
        return ([pltpu.VMEM((half, n), F32)] * n_chain) + ([pltpu.VMEM((half, n), BF16)] * n_chain)

    half = _tile(ATTN_HALF, tq)
    lat = pl.pallas_call(
        functools.partial(_attn_lat_kernel, seq=seq, ctx=ctx, half=half),
        grid=(nb, nh, qt),
        in_specs=[
            smem,
            pl.BlockSpec((tq, V_DIM), lambda b, h, t: (b * qt + t, h)),
            pl.BlockSpec((seq, V_DIM), lambda b, h, t: (b, kcol + h)),
            pl.BlockSpec((ctx, V_DIM), lambda b, h, t: (ctx0 + b, kcol + h)),
            pl.BlockSpec((seq, V_DIM), lambda b, h, t: (b, vcol + h)),
            pl.BlockSpec((ctx, V_DIM), lambda b, h, t: (ctx0 + b, vcol + h)),
            pl.BlockSpec((1, V_DIM), lambda b, h, t: (0, 0)),
        ],
        out_specs=pl.BlockSpec((tq, V_DIM), lambda b, h, t: (b * qt + t, h)),
        out_shape=jax.ShapeDtypeStruct((n_lat, aw), BF16),
        scratch_shapes=[pltpu.VMEM((nk, V_DIM), BF16), pltpu.VMEM((nk, 2 * V_DIM), BF16)]
        + chain_scratch(tq, half, nk),
        compiler_params=_params("arbitrary", "arbitrary", "arbitrary"),
        name="attn_latent",
    )(lam_vec, qkv, qkv, qkv, qkv, qkv, g)

    half_c = _tile(ATTN_HALF, ctx)
    cx = pl.pallas_call(
        functools.partial(_attn_ctx_kernel, ctx=ctx, half=half_c),
        grid=(nb, nh),
        in_specs=[
            smem,
            pl.BlockSpec((ctx, V_DIM), lambda b, h: (ctx0 + b, h)),
            pl.BlockSpec((ctx, V_DIM), lambda b, h: (ctx0 + b, kcol + h)),
            pl.BlockSpec((ctx, V_DIM), lambda b, h: (ctx0 + b, vcol + h)),
            pl.BlockSpec((1, V_DIM), lambda b, h: (0, 0)),
        ],
        out_specs=pl.BlockSpec((ctx, V_DIM), lambda b, h: (b, h)),
        out_shape=jax.ShapeDtypeStruct((nb * ctx, aw), BF16),
        scratch_shapes=[pltpu.VMEM((ctx, 2 * V_DIM), BF16)] + chain_scratch(ctx, half_c, ctx),
        compiler_params=_params("arbitrary", "arbitrary"),
        name="attn_context",
    )(lam_vec, qkv, qkv, qkv, g)
    return lat, cx


def _top2_of4(a, b, c, d):
    m01, n01 = jnp.maximum(a, b), jnp.minimum(a, b)
    m23, n23 = jnp.maximum(c, d), jnp.minimum(c, d)
    return jnp.maximum(m01, m23) + jnp.maximum(jnp.minimum(m01, m23), jnp.maximum(n01, n23))


def _route(logits_t, bias_ref):
    n_exp = logits_t.shape[0]
    per = n_exp // N_GROUPS
    s = [jax.nn.sigmoid(logits_t[e:e + 1, :]) for e in range(n_exp)]
    sb = [s[e] + bias_ref[e:e + 1, :] for e in range(n_exp)]
    gscore = [_top2_of4(*sb[g * per:(g + 1) * per]) for g in range(N_GROUPS)]
    best, gidx = gscore[0], jnp.zeros_like(gscore[0], dtype=jnp.int32)
    for g in range(1, N_GROUPS):
        better = gscore[g] > best
        gidx = jnp.where(better, g, gidx)
        best = jnp.where(better, gscore[g], best)
    cand_b, cand_s = [], []
    for jx in range(per):
        vb, vs = sb[jx], s[jx]
        for g in range(1, N_GROUPS):
            sel = gidx == g
            vb = jnp.where(sel, sb[g * per + jx], vb)
            vs = jnp.where(sel, s[g * per + jx], vs)
        cand_b.append(vb)
        cand_s.append(vs)

    def argmax_first(vals, exclude):
        bv = bi = bs = None
        for jx in range(per):
            v = vals[jx] if exclude is None else jnp.where(exclude == jx, -jnp.inf, vals[jx])
            if bv is None:
                bv, bi, bs = v, jnp.zeros_like(gidx), cand_s[jx]
            else:
                better = v > bv
                bi = jnp.where(better, jx, bi)
                bs = jnp.where(better, cand_s[jx], bs)
                bv = jnp.where(better, v, bv)
        return bi, bs

    j1, w1 = argmax_first(cand_b, None)
    j2, w2 = argmax_first(cand_b, j1)
    tot = w1 + w2
    return gidx * per + j1, gidx * per + j2, w1 / tot, w2 / tot


def _merge_kernel(x_ref, attn_lat_ref, attn_ctx_ref, rest_ref, ccp_ref, cxp_ref, ccn_ref, cxn_ref, mod_ref,
                  convw_ref, convb_ref, wab_ref, wcb_ref, wo_ref, g2_ref, wrh_ref, wrl_ref,
                  xo_ref, h2_ref, logit_ref, *, tm, seq, ctx, n_lat, nb, d, cw):
    i = pl.program_id(0)
    r0 = i * tm
    is_lat = r0 < n_lat
    seg = _segment(r0, seq, n_lat, nb)
    pos = jnp.where(is_lat, r0 % seq, (r0 - n_lat) % ctx)
    slen = jnp.where(is_lat, seq, ctx)
    has_prev = (pos > 0).astype(F32)
    has_next = (pos + tm < slen).astype(F32)

    cb = rest_ref[:, 0:cw].astype(F32)
    u = rest_ref[:, cw:2 * cw].astype(F32) * rest_ref[:, 2 * cw:3 * cw].astype(F32)
    ga = rest_ref[:, 3 * cw:3 * cw + d].astype(F32)
    gc = rest_ref[:, 3 * cw + d:3 * cw + 2 * d].astype(F32)
    hl = HALO_ROWS - 1
    halo_prev = ccp_ref[hl:hl + 1, :].astype(F32) * cxp_ref[hl:hl + 1, :].astype(F32) * has_prev
    halo_next = ccn_ref[0:1, :].astype(F32) * cxn_ref[0:1, :].astype(F32) * has_next
    rid = lax.broadcasted_iota(jnp.int32, u.shape, 0)
    u_prev = jnp.where(rid == 0, halo_prev, pltpu.roll(u, 1, 0))
    u_next = jnp.where(rid == tm - 1, halo_next, pltpu.roll(u, tm - 1, 0))
    y = cb * (u_prev * convw_ref[0:1, :] + u * convw_ref[1:2, :] + u_next * convw_ref[2:3, :]
              + convb_ref[...])

    attn = jnp.where(is_lat, attn_lat_ref[...], attn_ctx_ref[...])
    ma = jnp.dot(attn, wab_ref[0], preferred_element_type=F32)
    mc = jnp.dot(y.astype(BF16), wcb_ref[0], preferred_element_type=F32)
    merged = jax.nn.sigmoid(ga) * ma + jax.nn.sigmoid(gc) * mc
    out = jnp.dot(merged.astype(BF16), wo_ref[0], preferred_element_type=F32)

    m = mod_ref[pl.ds(seg, 1), :]
    xn = x_ref[...] + m[:, 2 * d:3 * d] * out
    xo_ref[...] = xn
    h2 = _adaln(xn, g2_ref, mod_ref, seg, 3, d)
    h2_ref[...] = _pack_halves(h2)

    h_hi = h2.astype(BF16)
    h_lo = (h2 - h_hi.astype(F32)).astype(BF16)
    w_hi = wrh_ref[...]
    logits = (jnp.dot(h_hi, w_hi, preferred_element_type=F32)
              + jnp.dot(h_lo, w_hi, preferred_element_type=F32)
              + jnp.dot(h_hi, wrl_ref[...], preferred_element_type=F32))
    logit_ref[...] = logits.T[0:logit_ref.shape[0], :]


def _route_kernel(logit_ref, rbias_ref, eidx_ref, ew_ref):
    e1, e2, w1, w2 = _route(logit_ref[...], rbias_ref)
    eidx_ref[...] = jnp.zeros(eidx_ref.shape, jnp.int32)
    ew_ref[...] = jnp.zeros(ew_ref.shape, F32)
    eidx_ref[0:1, :] = e1
    eidx_ref[1:2, :] = e2
    ew_ref[0:1, :] = w1
    ew_ref[1:2, :] = w2


def _route_tokens(logits_t, router_bias):
    n_exp, n_rows = logits_t.shape
    tr = _tile(2048, n_rows)
    out_spec = pl.BlockSpec((SUBLANES, tr), lambda i: (0, i))
    return pl.pallas_call(
        _route_kernel,
        grid=(n_rows // tr,),
        in_specs=[pl.BlockSpec((n_exp, tr), lambda i: (0, i)), pl.BlockSpec((n_exp, 1), lambda i: (0, 0))],
        out_specs=[out_spec, out_spec],
        out_shape=[jax.ShapeDtypeStruct((SUBLANES, n_rows), jnp.int32),
                   jax.ShapeDtypeStruct((SUBLANES, n_rows), F32)],
        compiler_params=_params("arbitrary"),
        name="route",
    )(logits_t, router_bias.reshape(n_exp, 1))


def _merge(xa, attn_lat, attn_ctx, rest, mod_l, conv_w, conv_b, w_ab, w_cb, w_o, layer, g2, w_router,
           *, n_rows, tm, seq, ctx, n_lat, nb):
    nt, d = xa.shape
    aw = attn_lat.shape[1]
    lat_tiles = n_lat // tm
    ctx_tiles = attn_ctx.shape[0] // tm
    cw = conv_w.shape[1]
    n_exp = w_router.shape[1]
    wr_pad = jnp.pad(w_router, ((0, 0), (0, LANES - n_exp)))
    wr_hi = wr_pad.astype(BF16)
    wr_lo = (wr_pad - wr_hi.astype(F32)).astype(BF16)
    n_rest = rest.shape[1]
    hb = tm // HALO_ROWS
    last_hblk = nt // HALO_ROWS - 1

    def prev_idx(col):
        return lambda i: (jnp.maximum(i * hb - 1, 0), col)

    def next_idx(col):
        return lambda i: (jnp.minimum((i + 1) * hb, last_hblk), col)

    const = lambda i: (0, 0)
    kern = functools.partial(_merge_kernel, tm=tm, seq=seq, ctx=ctx, n_lat=n_lat, nb=nb, d=d, cw=cw)
    return pl.pallas_call(
        kern,
        grid=(n_rows // tm,),
        in_specs=[
            pl.BlockSpec((tm, d), lambda i: (i, 0)),
            pl.BlockSpec((tm, aw), lambda i: (jnp.minimum(i, lat_tiles - 1), 0)),
            pl.BlockSpec((tm, aw), lambda i: (jnp.clip(i - lat_tiles, 0, ctx_tiles - 1), 0)),
            pl.BlockSpec((tm, n_rest), lambda i: (i, 0)),
            pl.BlockSpec((HALO_ROWS, cw), prev_idx(1)),
            pl.BlockSpec((HALO_ROWS, cw), prev_idx(2)),
            pl.BlockSpec((HALO_ROWS, cw), next_idx(1)),
            pl.BlockSpec((HALO_ROWS, cw), next_idx(2)),
            pl.BlockSpec((MOD_ROWS, 6 * d), const),
            pl.BlockSpec((3, cw), const),
            pl.BlockSpec((1, cw), const),
            pl.BlockSpec((1,) + w_ab.shape[1:], lambda i: (layer, 0, 0)),
            pl.BlockSpec((1,) + w_cb.shape[1:], lambda i: (layer, 0, 0)),
            pl.BlockSpec((1,) + w_o.shape[1:], lambda i: (layer, 0, 0)),
            pl.BlockSpec((1, d), const),
            pl.BlockSpec((d, LANES), const),
            pl.BlockSpec((d, LANES), const),
        ],
        out_specs=[
            pl.BlockSpec((tm, d), lambda i: (i, 0)),
            pl.BlockSpec((tm, d // 2), lambda i: (i, 0)),
            pl.BlockSpec((n_exp, tm), lambda i: (0, i)),
        ],
        out_shape=[
            jax.ShapeDtypeStruct((n_rows, d), F32),
            jax.ShapeDtypeStruct((n_rows, d // 2), U32),
            jax.ShapeDtypeStruct((n_exp, n_rows), F32),
        ],
        compiler_params=_params("arbitrary"),
        name="merge",
    )(xa, attn_lat, attn_ctx, rest, rest, rest, rest, rest, mod_l, conv_w, conv_b.reshape(1, cw), w_ab, w_cb, w_o,
      g2.reshape(1, d), wr_hi, wr_lo)


W_CHUNKS = 4
N_STREAM = 3 * W_CHUNKS


def _weight_stream_schedule(tile_expert, n_active):
    n_tiles = tile_expert.shape[0]
    t = jnp.arange(n_tiles, dtype=jnp.int32)
    active = t < n_active
    prev = jnp.concatenate([tile_expert[:1] - 1, tile_expert[:-1]])
    first = active & (tile_expert != prev)
    gid = jnp.cumsum(first.astype(jnp.int32)) - 1
    group_start = lax.cummax(jnp.where(first, t, 0))
    same = (gid[:, None] == gid[None, :]) & active[None, :]
    size = jnp.maximum(jnp.sum(same.astype(jnp.int32), axis=1), 1)
    pos = t - group_start
    nxt = group_start + size
    has_next = active & (nxt < n_active)
    next_expert = jnp.where(has_next, tile_expert[jnp.minimum(nxt, n_tiles - 1)], 0)
    base, rem = N_STREAM // size, N_STREAM % size
    quota = jnp.where(has_next, base + (pos < rem).astype(jnp.int32), 0)
    chunk0 = pos * base + jnp.minimum(pos, rem)
    return tuple(a.astype(jnp.int32) for a in (gid % 2, next_expert, chunk0, quota))


def _dispatch_plan(eidx, n_tok, tg, n_exp):
    ef = eidx[:2].reshape(-1)
    n2 = 2 * n_tok
    onehot = (ef[:, None] == jnp.arange(n_exp)[None, :]).astype(jnp.int32)
    csum = jnp.cumsum(onehot, axis=0)
    rank = jnp.sum(onehot * csum, axis=1) - 1
    counts = csum[-1]
    padded = ((counts + tg - 1) // tg) * tg
    gend = jnp.cumsum(padded)
    gstart = gend - padded
    dest = jnp.sum(onehot * gstart[None, :], axis=1) + rank
    n_slots = -(-n2 // tg) * tg + n_exp * tg
    tile_start = jnp.arange(n_slots // tg, dtype=jnp.int32) * tg
    tile_expert = jnp.sum((gend[None, :] <= tile_start[:, None]).astype(jnp.int32), axis=1)
    tile_expert = jnp.minimum(tile_expert, n_exp - 1).astype(jnp.int32)
    n_active = (gend[-1] // tg).astype(jnp.int32).reshape(1)
    slots = dest.reshape(2, n_tok).astype(jnp.int32)
    pad_start = jnp.concatenate([gstart + counts, gend[-1:]]).astype(jnp.int32)
    pad_count = jnp.concatenate([padded - counts, n_slots - gend[-1:]]).astype(jnp.int32)
    stream = _weight_stream_schedule(tile_expert, n_active[0])
    return slots, pad_start, pad_count, tile_expert, n_active, n_slots, stream


def _dispatch_kernel(d1_ref, d2_ref, ps_ref, pc_ref, h2_ref, xs_hbm, xbuf, zbuf, sem, zsem, *, tm, n_pad):
    i = pl.program_id(0)
    last = pl.num_programs(0) - 1
    slot = i % 2

    def wait_rows(s):
        for _ in range(2):
            pltpu.make_async_copy(xbuf.at[s], xs_hbm.at[pl.ds(0, tm), :], sem.at[s]).wait()

    def zero_copy(row):
        return pltpu.make_async_copy(zbuf.at[pl.ds(0, 1), :], xs_hbm.at[pl.ds(row, 1), :], zsem)

    @pl.when(i == 0)
    def _():
        zbuf[...] = jnp.zeros(zbuf.shape, zbuf.dtype)
        for e in range(n_pad):
            def start_zero(r, carry, e=e):
                zero_copy(ps_ref[e] + r).start()
                return carry
            lax.fori_loop(0, pc_ref[e], start_zero, 0)

    @pl.when(i >= 2)
    def _():
        wait_rows(slot)

    xbuf[slot] = h2_ref[...]

    def scatter(r, carry):
        for d_ref in (d1_ref, d2_ref):
            pltpu.make_async_copy(xbuf.at[slot, pl.ds(r, 1), :], xs_hbm.at[pl.ds(d_ref[i * tm + r], 1), :],
                                  sem.at[slot]).start()
        return carry

    lax.fori_loop(0, tm, scatter, 0, unroll=8)

    @pl.when(i == last)
    def _():
        wait_rows(slot)
        wait_rows(1 - slot)
        for e in range(n_pad):
            def wait_zero(r, carry):
                zero_copy(0).wait()
                return carry
            lax.fori_loop(0, pc_ref[e], wait_zero, 0)


def _dispatch(h2, plan, *, tm):
    slots, pad_start, pad_count, _, _, n_slots = plan[:6]
    n_tok, d = h2.shape
    assert n_tok // tm >= 2
    grid_spec = pltpu.PrefetchScalarGridSpec(
        num_scalar_prefetch=4,
        grid=(n_tok // tm,),
        in_specs=[pl.BlockSpec((tm, d), lambda i, *_: (i, 0))],
        out_specs=pl.BlockSpec(memory_space=pl.ANY),
        scratch_shapes=[pltpu.VMEM((2, tm, d), h2.dtype), pltpu.VMEM((SUBLANES, d), h2.dtype),
                        pltpu.SemaphoreType.DMA((2,)), pltpu.SemaphoreType.DMA(())],
    )
    return pl.pallas_call(
        functools.partial(_dispatch_kernel, tm=tm, n_pad=pad_start.shape[0]),
        grid_spec=grid_spec,
        out_shape=jax.ShapeDtypeStruct((n_slots, d), h2.dtype),
        compiler_params=_params("arbitrary"),
        name="dispatch",
    )(slots[0], slots[1], pad_start, pad_count, h2)


def _expert_kernel(te_ref, na_ref, par_ref, ne_ref, c0_ref, nq_ref, x_ref, wg_hbm, wu_hbm, wd_hbm, y_ref,
                   wg_s, wu_s, wd_s, stg_a, stg_b, sem_a, sem_b, *, layer):
    t = pl.program_id(0)
    d, f = wg_s.shape[1:]
    rows_a, rows_b = d // W_CHUNKS, f // W_CHUNKS

    def by_matrix(c, fn):
        if isinstance(c, int):
            fn(c // W_CHUNKS, c % W_CHUNKS)
            return
        for m in range(3):
            @pl.when((c >= m * W_CHUNKS) & (c < (m + 1) * W_CHUNKS))
            def _(m=m):
                fn(m, c - m * W_CHUNKS)

    def chunk_copy(e, m, j, s):
        if m < 2:
            src = (wg_hbm, wu_hbm)[m].at[layer, e, pl.ds(pl.multiple_of(j * rows_a, rows_a), rows_a), :]
            return pltpu.make_async_copy(src, stg_a.at[s], sem_a.at[s])
        src = wd_hbm.at[layer, e, pl.ds(pl.multiple_of(j * rows_b, rows_b), rows_b), :]
        return pltpu.make_async_copy(src, stg_b.at[s], sem_b.at[s])

    def start(e, c):
        by_matrix(c, lambda m, j: chunk_copy(e, m, j, c % 2).start())

    def convert(e, c, w):
        def fn(m, j):
            s = c % 2
            chunk_copy(e, m, j, s).wait()
            if m < 2:
                dst = (wg_s, wu_s)[m]
                dst[w, pl.ds(pl.multiple_of(j * rows_a, rows_a), rows_a), :] = stg_a[s].astype(BF16)
            else:
                wd_s[w, pl.ds(pl.multiple_of(j * rows_b, rows_b), rows_b), :] = stg_b[s].astype(BF16)
        by_matrix(c, fn)

    @pl.when(t == 0)
    def _():
        e0 = te_ref[0]
        start(e0, 0)
        start(e0, 1)
        for c in range(N_STREAM):
            convert(e0, c, 0)
            if c + 2 < N_STREAM:
                start(e0, c + 2)

    @pl.when(t < na_ref[0])
    def _():
        e_next, c0, n_conv = ne_ref[t], c0_ref[t], nq_ref[t]
        w_cur = par_ref[t]

        @pl.when((n_conv > 0) & (c0 == 0))
        def _():
            start(e_next, 0)
            start(e_next, 1)

        d2 = x_ref.shape[1]
        x_lo, x_hi = (h.astype(BF16) for h in _unpack_halves(x_ref[...]))
        gte = (jnp.dot(x_lo, wg_s[w_cur, 0:d2, :], preferred_element_type=F32)
               + jnp.dot(x_hi, wg_s[w_cur, d2:2 * d2, :], preferred_element_type=F32))
        up = (jnp.dot(x_lo, wu_s[w_cur, 0:d2, :], preferred_element_type=F32)
              + jnp.dot(x_hi, wu_s[w_cur, d2:2 * d2, :], preferred_element_type=F32))
        he = (gte * jax.nn.sigmoid(gte) * up).astype(BF16)
        y_ref[...] = _pack_halves(jnp.dot(he, wd_s[w_cur], preferred_element_type=F32))

        def convert_next(k, carry):
            c = c0 + k
            convert(e_next, c, 1 - w_cur)

            @pl.when(c + 2 < N_STREAM)
            def _():
                start(e_next, c + 2)
            return carry

        lax.fori_loop(0, n_conv, convert_next, 0)

    @pl.when(t >= na_ref[0])
    def _():
        y_ref[...] = jnp.zeros(y_ref.shape, U32)


def _experts(x_sorted, plan, wg, wu, wd, layer, *, tg):
    tile_expert, n_active, n_slots, stream = plan[3:7]
    d2 = x_sorted.shape[1]
    d = 2 * d2
    f = wg.shape[3]
    hbm = pl.BlockSpec(memory_space=pl.ANY)
    grid_spec = pltpu.PrefetchScalarGridSpec(
        num_scalar_prefetch=6,
        grid=(n_slots // tg,),
        in_specs=[pl.BlockSpec((tg, d2), lambda t, *_: (t, 0)), hbm, hbm, hbm],
        out_specs=pl.BlockSpec((tg, d2), lambda t, *_: (t, 0)),
        scratch_shapes=[
            pltpu.VMEM((2, d, f), BF16), pltpu.VMEM((2, d, f), BF16), pltpu.VMEM((2, f, d), BF16),
            pltpu.VMEM((2, d // W_CHUNKS, f), F32), pltpu.VMEM((2, f // W_CHUNKS, d), F32),
            pltpu.SemaphoreType.DMA((2,)), pltpu.SemaphoreType.DMA((2,)),
        ],
    )
    return pl.pallas_call(
        functools.partial(_expert_kernel, layer=layer),
        grid_spec=grid_spec,
        out_shape=jax.ShapeDtypeStruct((n_slots, d2), U32),
        compiler_params=_params("arbitrary"),
        name="experts",
    )(tile_expert, n_active, *stream, x_sorted, wg, wu, wd)


def _combine_kernel(s1_ref, s2_ref, x_ref, wt_ref, y_hbm, mod_ref, g_ref, modn_ref, xo_ref, *refs,
                    tm, seq, n_lat, nb, d, final):
    buf, sem = refs[-2], refs[-1]
    i = pl.program_id(0)
    last = pl.num_programs(0) - 1
    base = i * tm
    slot = i % 2

    def issue(tile, dst_slot):
        def body(r, carry):
            for k, s_ref in enumerate((s1_ref, s2_ref)):
                pltpu.make_async_copy(y_hbm.at[pl.ds(s_ref[tile * tm + r], 1), :],
                                      buf.at[dst_slot, k, pl.ds(r, 1), :], sem.at[dst_slot]).start()
            return carry
        lax.fori_loop(0, tm, body, 0, unroll=8)

    def wait(dst_slot):
        for k in range(2):
            pltpu.make_async_copy(y_hbm.at[pl.ds(0, tm), :], buf.at[dst_slot, k], sem.at[dst_slot]).wait()

    @pl.when(i == 0)
    def _():
        issue(0, 0)

    @pl.when(i < last)
    def _():
        issue(i + 1, 1 - slot)

    wait(slot)
    seg = _segment(base, seq, n_lat, nb)
    m = mod_ref[pl.ds(seg, 1), :]
    y1_lo, y1_hi = _unpack_halves(buf[slot, 0])
    y2_lo, y2_hi = _unpack_halves(buf[slot, 1])
    w1, w2 = wt_ref[:, 0:1], wt_ref[:, 1:2]
    moe = jnp.concatenate([w1 * y1_lo + w2 * y2_lo, w1 * y1_hi + w2 * y2_hi], axis=1)
    xn = x_ref[...] + m[:, 5 * d:6 * d] * moe
    if final:
        xo_ref[...] = _rms(xn) * g_ref[...]
    else:
        xo_ref[...] = xn
        refs[0][...] = _adaln(xn, g_ref, modn_ref, seg, 0, d).astype(BF16)


def _combine(xa, y_sorted, slots, w_tok, mod_l, g_next, mod_next, *, tm, seq, n_lat, nb, final):
    n_rows, d = xa.shape
    row_spec = pl.BlockSpec((tm, d), lambda i, s1, s2: (i, 0))
    mod_spec = pl.BlockSpec((MOD_ROWS, 6 * d), lambda i, s1, s2: (0, 0))
    out_specs = [row_spec]
    out_shape = [jax.ShapeDtypeStruct((n_rows, d), F32)]
    if not final:
        out_specs.append(row_spec)
        out_shape.append(jax.ShapeDtypeStruct((n_rows, d), BF16))
    grid_spec = pltpu.PrefetchScalarGridSpec(
        num_scalar_prefetch=2,
        grid=(n_rows // tm,),
        in_specs=[
            row_spec,
            pl.BlockSpec((tm, 2), lambda i, s1, s2: (i, 0)),
            pl.BlockSpec(memory_space=pl.ANY),
            mod_spec,
            pl.BlockSpec((1, d), lambda i, s1, s2: (0, 0)),
            mod_spec,
        ],
        out_specs=out_specs,
        scratch_shapes=[pltpu.VMEM((2, 2, tm, d // 2), U32), pltpu.SemaphoreType.DMA((2,))],
    )
    kern = functools.partial(_combine_kernel, tm=tm, seq=seq, n_lat=n_lat, nb=nb, d=d, final=final)
    out = pl.pallas_call(
        kern,
        grid_spec=grid_spec,
        out_shape=out_shape,
        compiler_params=_params("arbitrary"),
        name="combine",
    )(slots[0], slots[1], xa, w_tok, y_sorted, mod_l, g_next.reshape(1, d), mod_next)
    return (out[0], None) if final else (out[0], out[1])


def _tile(limit, *sizes):
    t = limit
    while any(s % t for s in sizes):
        t //= 2
    return t


def kernel(x, c, ctx, c_ctx, w_mod, b_mod, norm1_g, norm2_g, w_in, lambda_q1, lambda_k1, lambda_q2,
           lambda_k2, subln_g, conv_w, conv_b, w_attn_branch, w_conv_branch, w_out, w_router,
           router_bias, w_exp_gate, w_exp_up, w_exp_down, final_g):
    nb, seq, d = x.shape
    n_ctx_tok = ctx.shape[1]
    depth = w_mod.shape[0]
    n_exp = w_router.shape[1]
    aw = d // 2
    n_lat = nb * seq
    nt = n_lat + nb * n_ctx_tok
    assert nb < MOD_ROWS and seq % GRID_W == 0 and aw % V_DIM == 0 and n_exp % N_GROUPS == 0

    tm_in = _tile(1024, seq, nb * n_ctx_tok)
    tm_mg = _tile(256, seq, n_ctx_tok)
    tq = _tile(512, seq)
    tg = _tile(512, 2 * tm_mg)

    cvec = jnp.zeros((MOD_ROWS, d), F32).at[:nb].set(c).at[nb].set(c_ctx)
    mod = _modulation(cvec, w_mod, b_mod)
    rope = _rope_tables(seq, tm_in)
    xa, h = _prenorm(x.reshape(n_lat, d), ctx.reshape(nb * n_ctx_tok, d), mod[0], norm1_g[0],
                     tm=tm_mg, seq=seq, nb=nb)
    w_ab, w_cb, w_o = (w.astype(BF16) for w in (w_attn_branch, w_conv_branch, w_out))

    for l in range(depth):
        last = l == depth - 1
        lam_init = 0.8 - 0.6 * math.exp(-0.3 * l)
        lam = (jnp.exp(jnp.sum(lambda_q1[l] * lambda_k1[l])) - jnp.exp(jnp.sum(lambda_q2[l] * lambda_k2[l]))
               + lam_init)
        lam_vec = jnp.stack([lam, jnp.asarray(1.0 - lam_init, F32)]).astype(F32)

        qkv, rest = _in_proj(h, w_in, l, rope, tm=tm_in, seq=seq, n_lat=n_lat)
        attn_lat, attn_ctx = _attention(qkv, lam_vec, subln_g[l], tq=tq, seq=seq, ctx=n_ctx_tok, n_lat=n_lat,
                                        nb=nb, aw=aw)
        n_rows = n_lat if last else nt
        xa, h2, logits_t = _merge(
            xa, attn_lat, attn_ctx, rest, mod[l], conv_w[l], conv_b[l], w_ab, w_cb, w_o, l, norm2_g[l],
            w_router, n_rows=n_rows, tm=tm_mg, seq=seq, ctx=n_ctx_tok, n_lat=n_lat, nb=nb)
        eidx, ew = _route_tokens(logits_t, router_bias)
        plan = _dispatch_plan(eidx, n_rows, tg, n_exp)
        x_sorted = _dispatch(h2, plan, tm=tm_mg)
        y_sorted = _experts(x_sorted, plan, w_exp_gate, w_exp_up, w_exp_down, l, tg=tg)
        g_next, mod_next = (final_g, mod[l]) if last else (norm1_g[l + 1], mod[l + 1])
        xa, h = _combine(xa, y_sorted, plan[0], ew[:2].T, mod[l], g_next, mod_next, tm=tm_mg, seq=seq,
                         n_lat=n_lat, nb=nb, final=last)

    return xa.reshape(nb, seq, d)
```

```python
import functools
import math

import jax
import jax.numpy as jnp
from jax import lax
from jax.experimental import pallas as pl
from jax.experimental.pallas import tpu as pltpu

HEAD_DIM = 64
V_DIM = 2 * HEAD_DIM
GRID_W = 64
ROPE_THETA = 10000.0
ROPE_AXIS_DIM = HEAD_DIM // 2
N_GROUPS = 4
EPS = 1e-6
LANES = 128
SUBLANES = 8
HALO_ROWS = 2 * SUBLANES
VMEM_LIMIT_BYTES = 56 * 1024 * 1024
MOD_ROWS = 8
LOG2E = math.log2(math.e)

F32 = jnp.float32
BF16 = jnp.bfloat16
HIGHEST = lax.Precision.HIGHEST
NT_DIMS = (((1,), (1,)), ((), ()))


def _params(*sem):
    return pltpu.CompilerParams(dimension_semantics=sem, vmem_limit_bytes=VMEM_LIMIT_BYTES)


def _rms(x):
    return x * lax.rsqrt(jnp.mean(x * x, axis=-1, keepdims=True) + EPS)


U32 = jnp.uint32
HIGH_HALF = 0xFFFF0000


def _pack_halves(x):
    c2 = x.shape[1] // 2
    lo = lax.bitcast_convert_type(x[:, 0:c2].astype(BF16).astype(F32), U32) >> 16
    hi = lax.bitcast_convert_type(x[:, c2:2 * c2].astype(BF16).astype(F32), U32) & U32(HIGH_HALF)
    return hi | lo


def _unpack_halves(w):
    return (lax.bitcast_convert_type(w << 16, F32), lax.bitcast_convert_type(w & U32(HIGH_HALF), F32))


def _segment(row0, seq, n_lat, nb):
    return jnp.where(row0 < n_lat, row0 // seq, nb)


def _adaln(x, g_ref, mod_ref, seg, shift_col, d):
    m = mod_ref[pl.ds(seg, 1), :]
    return _rms(x) * g_ref[...] * (1.0 + m[:, (shift_col + 1) * d:(shift_col + 2) * d]) + m[:, shift_col * d:(shift_col + 1) * d]


def _mod_kernel(c_ref, w_ref, b_ref, o_ref):
    c = c_ref[...]
    sc = c * jax.nn.sigmoid(c)
    o_ref[0] = jnp.dot(sc, w_ref[0], precision=HIGHEST, preferred_element_type=F32) + b_ref[0]


def _modulation(cvec, w_mod, b_mod):
    depth, d, n6 = w_mod.shape
    tn = _tile(1024, n6)
    return pl.pallas_call(
        _mod_kernel,
        grid=(depth, n6 // tn),
        in_specs=[
            pl.BlockSpec((MOD_ROWS, d), lambda l, j: (0, 0)),
            pl.BlockSpec((1, d, tn), lambda l, j: (l, 0, j)),
            pl.BlockSpec((1, 1, tn), lambda l, j: (l, 0, j)),
        ],
        out_specs=pl.BlockSpec((1, MOD_ROWS, tn), lambda l, j: (l, 0, j)),
        out_shape=jax.ShapeDtypeStruct((depth, MOD_ROWS, n6), F32),
        compiler_params=_params("arbitrary", "arbitrary"),
        name="modulation",
    )(cvec, w_mod, b_mod.reshape(depth, 1, n6))


def _prenorm_kernel(x_ref, c_ref, mod_ref, g_ref, xa_ref, h_ref, *, tm, seq, n_lat, nb, d):
    r0 = pl.program_id(0) * tm
    x = jnp.where(r0 < n_lat, x_ref[...], c_ref[...])
    xa_ref[...] = x
    h_ref[...] = _adaln(x, g_ref, mod_ref, _segment(r0, seq, n_lat, nb), 0, d).astype(BF16)


def _prenorm(x2, c2, mod_l, g1, *, tm, seq, nb):
    n_lat, d = x2.shape
    nt = n_lat + c2.shape[0]
    lat_tiles = n_lat // tm
    ctx_tiles = c2.shape[0] // tm
    row_spec = pl.BlockSpec((tm, d), lambda i: (i, 0))
    return pl.pallas_call(
        functools.partial(_prenorm_kernel, tm=tm, seq=seq, n_lat=n_lat, nb=nb, d=d),
        grid=(nt // tm,),
        in_specs=[
            pl.BlockSpec((tm, d), lambda i: (jnp.minimum(i, lat_tiles - 1), 0)),
            pl.BlockSpec((tm, d), lambda i: (jnp.clip(i - lat_tiles, 0, ctx_tiles - 1), 0)),
            pl.BlockSpec((MOD_ROWS, 6 * d), lambda i: (0, 0)),
            pl.BlockSpec((1, d), lambda i: (0, 0)),
        ],
        out_specs=[row_spec, row_spec],
        out_shape=[jax.ShapeDtypeStruct((nt, d), F32), jax.ShapeDtypeStruct((nt, d), BF16)],
        compiler_params=_params("arbitrary"),
        name="prenorm",
    )(x2, c2, mod_l, g1.reshape(1, d))


def _in_kernel(h_ref, w_ref, *refs, rope, aw):
    o_ref, w_scr = refs[-2], refs[-1]
    j = pl.program_id(0)

    @pl.when(pl.program_id(1) == 0)
    def _():
        w_scr[...] = w_ref[0].astype(BF16)

    acc = jnp.dot(h_ref[...], w_scr[...], preferred_element_type=F32)
    if not rope:
        o_ref[...] = acc.astype(o_ref.dtype)
        return
    rc_ref, ra_ref, rb_ref = refs[:3]

    @pl.when(j < 2)
    def _():
        scale = jnp.where(j == 0, HEAD_DIM ** -0.5 * LOG2E, 1.0).astype(F32)
        rc, ra, rb = rc_ref[...], ra_ref[...], rb_ref[...]
        for c in range(aw // LANES):
            a = acc[:, c * LANES:(c + 1) * LANES]
            r = (a * rc + pltpu.roll(a, LANES - ROPE_AXIS_DIM // 2, 1) * ra
                 + pltpu.roll(a, ROPE_AXIS_DIM // 2, 1) * rb)
            o_ref[:, c * LANES:(c + 1) * LANES] = (r * scale).astype(BF16)

    @pl.when(j == 2)
    def _():
        o_ref[...] = acc.astype(BF16)


def _rope_tables(seq, tm):
    pos = jnp.arange(seq)
    row = (pos // GRID_W).astype(F32)
    col = (pos % GRID_W).astype(F32)
    inv = ROPE_THETA ** (-jnp.arange(0, ROPE_AXIS_DIM, 2, dtype=F32) / ROPE_AXIS_DIM)
    lane = jnp.arange(LANES)
    jj = lane % HEAD_DIM
    axis = jj // ROPE_AXIS_DIM
    r = jj % ROPE_AXIS_DIM
    f = r % (ROPE_AXIS_DIM // 2)
    half = r // (ROPE_AXIS_DIM // 2)
    posv = jnp.where(axis[None, :] == 0, row[:, None], col[:, None])
    ang = posv * inv[f][None, :]
    cos, sin = jnp.cos(ang), jnp.sin(ang)
    rc = jnp.concatenate([cos, jnp.ones((tm, LANES), F32)], axis=0)
    ra = jnp.concatenate([jnp.where(half[None, :] == 0, -sin, 0.0), jnp.zeros((tm, LANES), F32)], axis=0)
    rb = jnp.concatenate([jnp.where(half[None, :] == 1, sin, 0.0), jnp.zeros((tm, LANES), F32)], axis=0)
    return rc, ra, rb


def _in_proj(h, w_in, layer, rope, *, tm, seq, n_lat):
    nt, d = h.shape
    in_cols = w_in.shape[2]
    aw = d // 2
    n_qkv = 3
    n_rest = in_cols // aw - n_qkv
    n_lat_tiles = n_lat // tm
    seq_tiles = seq // tm

    def rope_idx(j, i):
        return (jnp.where(i < n_lat_tiles, i % seq_tiles, seq_tiles), 0)

    def call(col0, n_col, out_dtype, with_rope):
        in_specs = [
            pl.BlockSpec((tm, d), lambda j, i: (i, 0)),
            pl.BlockSpec((1, d, aw), lambda j, i: (layer, 0, col0 + j)),
        ]
        args = [h, w_in]
        if with_rope:
            in_specs += [pl.BlockSpec((tm, LANES), rope_idx)] * 3
            args += list(rope)
        return pl.pallas_call(
            functools.partial(_in_kernel, rope=with_rope, aw=aw),
            grid=(n_col, nt // tm),
            in_specs=in_specs,
            out_specs=pl.BlockSpec((tm, aw), lambda j, i: (i, j)),
            out_shape=jax.ShapeDtypeStruct((nt, n_col * aw), out_dtype),
            scratch_shapes=[pltpu.VMEM((d, aw), BF16)],
            compiler_params=_params("arbitrary", "arbitrary"),
            name="in_proj_qkv" if with_rope else "in_proj_rest",
        )(*args)

    return call(0, n_qkv, BF16, True), call(n_qkv, n_rest, BF16, False)


EXP_CHUNK = 128
ATTN_HALF = 512


def _split_maps(q_ref, half):
    q_maps = []
    for hh in range(q_ref.shape[0] // half):
        q = q_ref[hh * half:(hh + 1) * half, :]
        lane = lax.broadcasted_iota(jnp.int32, q.shape, 1)
        zero = jnp.zeros_like(q)
        q_maps += [jnp.where(lane < HEAD_DIM, q, zero), jnp.where(lane >= HEAD_DIM, q, zero)]
    return q_maps


def _finish_heads(outs, lam_ref, g_ref, o_ref, half):
    for hh in range(len(outs) // 2):
        a, b = outs[2 * hh], outs[2 * hh + 1]
        o1 = a[:, 0:V_DIM] / a[:, V_DIM:V_DIM + 1]
        o2 = b[:, 0:V_DIM] * (lam_ref[0] / b[:, V_DIM:V_DIM + 1])
        o = _rms(o1 - o2) * g_ref[...] * lam_ref[1]
        o_ref[hh * half:(hh + 1) * half, :] = o.astype(BF16)


def _diff_attention(lam_ref, q_ref, k_ref, v_ref, g_ref, o_ref, s_refs, e_refs, k0, n, half):
    q_maps = _split_maps(q_ref, half)
    n_chain = len(q_maps)
    outs = [None] * n_chain

    def scores(i):
        s_refs[i][:, 0:n] = lax.dot_general(q_maps[i], k_ref[k0:k0 + n, :], NT_DIMS,
                                            preferred_element_type=F32)

    def exps(i):
        mx = jnp.max(s_refs[i][:, 0:n], axis=-1, keepdims=True)
        for c in range(n // EXP_CHUNK):
            sl = slice(c * EXP_CHUNK, (c + 1) * EXP_CHUNK)
            e_refs[i][:, sl] = jnp.exp2(s_refs[i][:, sl] - mx).astype(BF16)

    def values(i):
        outs[i] = jnp.dot(e_refs[i][:, 0:n], v_ref[k0:k0 + n, :], preferred_element_type=F32)

    scores(0)
    for i in range(n_chain):
        if i + 1 < n_chain:
            scores(i + 1)
        exps(i)
        if i >= 1:
            values(i - 1)
    values(n_chain - 1)
    _finish_heads(outs, lam_ref, g_ref, o_ref, half)


def _attn_lat_kernel(lam_ref, q_ref, kl_ref, kc_ref, vl_ref, vc_ref, g_ref, o_ref, k_scr, v_scr,
                     *chain_scr, seq, ctx, half):
    nk = seq + ctx

    @pl.when(pl.program_id(2) == 0)
    def _():
        k_scr[0:seq, :] = kl_ref[...]
        k_scr[seq:nk, :] = kc_ref[...]
        v_scr[0:seq, 0:V_DIM] = vl_ref[...]
        v_scr[seq:nk, 0:V_DIM] = vc_ref[...]
        v_scr[:, V_DIM:2 * V_DIM] = jnp.ones((nk, V_DIM), BF16)

    n_chain = len(chain_scr) // 2
    _diff_attention(lam_ref, q_ref, k_scr, v_scr, g_ref, o_ref, chain_scr[:n_chain], chain_scr[n_chain:],
                    0, nk, half)


def _attn_ctx_kernel(lam_ref, q_ref, k_ref, v_ref, g_ref, o_ref, v_scr, *chain_scr, ctx, half):
    v_scr[:, 0:V_DIM] = v_ref[...]
    v_scr[:, V_DIM:2 * V_DIM] = jnp.ones((ctx, V_DIM), BF16)
    n_chain = len(chain_scr) // 2
    _diff_attention(lam_ref, q_ref, k_ref, v_scr, g_ref, o_ref, chain_scr[:n_chain], chain_scr[n_chain:],
                    0, ctx, half)


def _attention(qkv, lam_vec, subln_g, *, tq, seq, ctx, n_lat, nb, aw):
    nh = aw // V_DIM
    kcol = aw // V_DIM
    vcol = 2 * aw // V_DIM
    ctx0 = n_lat // ctx
    qt = seq // tq
    nk = seq + ctx
    g = subln_g.reshape(1, V_DIM)
    smem = pl.BlockSpec(memory_space=pltpu.SMEM)

    def chain_scratch(rows, half, n):
        n_chain = 2 * (rows // half)
        return ([pltpu.VMEM((half, n), F32)] * n_chain) + ([pltpu.VMEM((half, n), BF16)] * n_chain)

    half = _tile(ATTN_HALF, tq)
    lat = pl.pallas_call(
        functools.partial(_attn_lat_kernel, seq=seq, ctx=ctx, half=half),
        grid=(nb, nh, qt),
        in_specs=[
            smem,
            pl.BlockSpec((tq, V_DIM), lambda b, h, t: (b * qt + t, h)),
            pl.BlockSpec((seq, V_DIM), lambda b, h, t: (b, kcol + h)),
            pl.BlockSpec((ctx, V_DIM), lambda b, h, t: (ctx0 + b, kcol + h)),
            pl.BlockSpec((seq, V_DIM), lambda b, h, t: (b, vcol + h)),
            pl.BlockSpec((ctx, V_DIM), lambda b, h, t: (ctx0 + b, vcol + h)),
            pl.BlockSpec((1, V_DIM), lambda b, h, t: (0, 0)),
        ],
        out_specs=pl.BlockSpec((tq, V_DIM), lambda b, h, t: (b * qt + t, h)),
        out_shape=jax.ShapeDtypeStruct((n_lat, aw), BF16),
        scratch_shapes=[pltpu.VMEM((nk, V_DIM), BF16), pltpu.VMEM((nk, 2 * V_DIM), BF16)]
        + chain_scratch(tq, half, nk),
        compiler_params=_params("arbitrary", "arbitrary", "arbitrary"),
        name="attn_latent",
    )(lam_vec, qkv, qkv, qkv, qkv, qkv, g)

    half_c = _tile(ATTN_HALF, ctx)
    cx = pl.pallas_call(
        functools.partial(_attn_ctx_kernel, ctx=ctx, half=half_c),
        grid=(nb, nh),
        in_specs=[
            smem,
            pl.BlockSpec((ctx, V_DIM), lambda b, h: (ctx0 + b, h)),
            pl.BlockSpec((ctx, V_DIM), lambda b, h: (ctx0 + b, kcol + h)),
            pl.BlockSpec((ctx, V_DIM), lambda b, h: (ctx0 + b, vcol + h)),
            pl.BlockSpec((1, V_DIM), lambda b, h: (0, 0)),
        ],
        out_specs=pl.BlockSpec((ctx, V_DIM), lambda b, h: (b, h)),
        out_shape=jax.ShapeDtypeStruct((nb * ctx, aw), BF16),
        scratch_shapes=[pltpu.VMEM((ctx, 2 * V_DIM), BF16)] + chain_scratch(ctx, half_c, ctx),
        compiler_params=_params("arbitrary", "arbitrary"),
        name="attn_context",
    )(lam_vec, qkv, qkv, qkv, g)
    return lat, cx


def _top2_of4(a, b, c, d):
    m01, n01 = jnp.maximum(a, b), jnp.minimum(a, b)
    m23, n23 = jnp.maximum(c, d), jnp.minimum(c, d)
    return jnp.maximum(m01, m23) + jnp.maximum(jnp.minimum(m01, m23), jnp.maximum(n01, n23))


def _route(logits_t, bias_ref):
    n_exp = logits_t.shape[0]
    per = n_exp // N_GROUPS
    s = [jax.nn.sigmoid(logits_t[e:e + 1, :]) for e in range(n_exp)]
    sb = [s[e] + bias_ref[e:e + 1, :] for e in range(n_exp)]
    gscore = [_top2_of4(*sb[g * per:(g + 1) * per]) for g in range(N_GROUPS)]
    best, gidx = gscore[0], jnp.zeros_like(gscore[0], dtype=jnp.int32)
    for g in range(1, N_GROUPS):
        better = gscore[g] > best
        gidx = jnp.where(better, g, gidx)
        best = jnp.where(better, gscore[g], best)
    cand_b, cand_s = [], []
    for jx in range(per):
        vb, vs = sb[jx], s[jx]
        for g in range(1, N_GROUPS):
            sel = gidx == g
            vb = jnp.where(sel, sb[g * per + jx], vb)
            vs = jnp.where(sel, s[g * per + jx], vs)
        cand_b.append(vb)
        cand_s.append(vs)

    def argmax_first(vals, exclude):
        bv = bi = bs = None
        for jx in range(per):
            v = vals[jx] if exclude is None else jnp.where(exclude == jx, -jnp.inf, vals[jx])
            if bv is None:
                bv, bi, bs = v, jnp.zeros_like(gidx), cand_s[jx]
            else:
                better = v > bv
                bi = jnp.where(better, jx, bi)
                bs = jnp.where(better, cand_s[jx], bs)
                bv = jnp.where(better, v, bv)
        return bi, bs

    j1, w1 = argmax_first(cand_b, None)
    j2, w2 = argmax_first(cand_b, j1)
    tot = w1 + w2
    return gidx * per + j1, gidx * per + j2, w1 / tot, w2 / tot


def _merge_kernel(x_ref, attn_lat_ref, attn_ctx_ref, rest_ref, ccp_ref, cxp_ref, ccn_ref, cxn_ref, mod_ref,
                  convw_ref, convb_ref, wab_ref, wcb_ref, wo_ref, g2_ref, wrh_ref, wrl_ref,
                  xo_ref, h2_ref, logit_ref, *, tm, seq, ctx, n_lat, nb, d, cw):
    i = pl.program_id(0)
    r0 = i * tm
    is_lat = r0 < n_lat
    seg = _segment(r0, seq, n_lat, nb)
    pos = jnp.where(is_lat, r0 % seq, (r0 - n_lat) % ctx)
    slen = jnp.where(is_lat, seq, ctx)
    has_prev = (pos > 0).astype(F32)
    has_next = (pos + tm < slen).astype(F32)

    cb = rest_ref[:, 0:cw].astype(F32)
    u = rest_ref[:, cw:2 * cw].astype(F32) * rest_ref[:, 2 * cw:3 * cw].astype(F32)
    ga = rest_ref[:, 3 * cw:3 * cw + d].astype(F32)
    gc = rest_ref[:, 3 * cw + d:3 * cw + 2 * d].astype(F32)
    hl = HALO_ROWS - 1
    halo_prev = ccp_ref[hl:hl + 1, :].astype(F32) * cxp_ref[hl:hl + 1, :].astype(F32) * has_prev
    halo_next = ccn_ref[0:1, :].astype(F32) * cxn_ref[0:1, :].astype(F32) * has_next
    rid = lax.broadcasted_iota(jnp.int32, u.shape, 0)
    u_prev = jnp.where(rid == 0, halo_prev, pltpu.roll(u, 1, 0))
    u_next = jnp.where(rid == tm - 1, halo_next, pltpu.roll(u, tm - 1, 0))
    y = cb * (u_prev * convw_ref[0:1, :] + u * convw_ref[1:2, :] + u_next * convw_ref[2:3, :]
              + convb_ref[...])

    attn = jnp.where(is_lat, attn_lat_ref[...], attn_ctx_ref[...])
    ma = jnp.dot(attn, wab_ref[0], preferred_element_type=F32)
    mc = jnp.dot(y.astype(BF16), wcb_ref[0], preferred_element_type=F32)
    merged = jax.nn.sigmoid(ga) * ma + jax.nn.sigmoid(gc) * mc
    out = jnp.dot(merged.astype(BF16), wo_ref[0], preferred_element_type=F32)

    m = mod_ref[pl.ds(seg, 1), :]
    xn = x_ref[...] + m[:, 2 * d:3 * d] * out
    xo_ref[...] = xn
    h2 = _adaln(xn, g2_ref, mod_ref, seg, 3, d)
    h2_ref[...] = _pack_halves(h2)

    h_hi = h2.astype(BF16)
    h_lo = (h2 - h_hi.astype(F32)).astype(BF16)
    w_hi = wrh_ref[...]
    logits = (jnp.dot(h_hi, w_hi, preferred_element_type=F32)
              + jnp.dot(h_lo, w_hi, preferred_element_type=F32)
              + jnp.dot(h_hi, wrl_ref[...], preferred_element_type=F32))
    logit_ref[...] = logits.T[0:logit_ref.shape[0], :]


def _route_kernel(logit_ref, rbias_ref, eidx_ref, ew_ref):
    e1, e2, w1, w2 = _route(logit_ref[...], rbias_ref)
    eidx_ref[...] = jnp.zeros(eidx_ref.shape, jnp.int32)
    ew_ref[...] = jnp.zeros(ew_ref.shape, F32)
    eidx_ref[0:1, :] = e1
    eidx_ref[1:2, :] = e2
    ew_ref[0:1, :] = w1
    ew_ref[1:2, :] = w2


def _route_tokens(logits_t, router_bias):
    n_exp, n_rows = logits_t.shape
    tr = _tile(2048, n_rows)
    out_spec = pl.BlockSpec((SUBLANES, tr), lambda i: (0, i))
    return pl.pallas_call(
        _route_kernel,
        grid=(n_rows // tr,),
        in_specs=[pl.BlockSpec((n_exp, tr), lambda i: (0, i)), pl.BlockSpec((n_exp, 1), lambda i: (0, 0))],
        out_specs=[out_spec, out_spec],
        out_shape=[jax.ShapeDtypeStruct((SUBLANES, n_rows), jnp.int32),
                   jax.ShapeDtypeStruct((SUBLANES, n_rows), F32)],
        compiler_params=_params("arbitrary"),
        name="route",
    )(logits_t, router_bias.reshape(n_exp, 1))


def _merge(xa, attn_lat, attn_ctx, rest, mod_l, conv_w, conv_b, w_ab, w_cb, w_o, layer, g2, w_router,
           *, n_rows, tm, seq, ctx, n_lat, nb):
    nt, d = xa.shape
    aw = attn_lat.shape[1]
    lat_tiles = n_lat // tm
    ctx_tiles = attn_ctx.shape[0] // tm
    cw = conv_w.shape[1]
    n_exp = w_router.shape[1]
    wr_pad = jnp.pad(w_router, ((0, 0), (0, LANES - n_exp)))
    wr_hi = wr_pad.astype(BF16)
    wr_lo = (wr_pad - wr_hi.astype(F32)).astype(BF16)
    n_rest = rest.shape[1]
    hb = tm // HALO_ROWS
    last_hblk = nt // HALO_ROWS - 1

    def prev_idx(col):
        return lambda i: (jnp.maximum(i * hb - 1, 0), col)

    def next_idx(col):
        return lambda i: (jnp.minimum((i + 1) * hb, last_hblk), col)

    const = lambda i: (0, 0)
    kern = functools.partial(_merge_kernel, tm=tm, seq=seq, ctx=ctx, n_lat=n_lat, nb=nb, d=d, cw=cw)
    return pl.pallas_call(
        kern,
        grid=(n_rows // tm,),
        in_specs=[
            pl.BlockSpec((tm, d), lambda i: (i, 0)),
            pl.BlockSpec((tm, aw), lambda i: (jnp.minimum(i, lat_tiles - 1), 0)),
            pl.BlockSpec((tm, aw), lambda i: (jnp.clip(i - lat_tiles, 0, ctx_tiles - 1), 0)),
            pl.BlockSpec((tm, n_rest), lambda i: (i, 0)),
            pl.BlockSpec((HALO_ROWS, cw), prev_idx(1)),
            pl.BlockSpec((HALO_ROWS, cw), prev_idx(2)),
            pl.BlockSpec((HALO_ROWS, cw), next_idx(1)),
            pl.BlockSpec((HALO_ROWS, cw), next_idx(2)),
            pl.BlockSpec((MOD_ROWS, 6 * d), const),
            pl.BlockSpec((3, cw), const),
            pl.BlockSpec((1, cw), const),
            pl.BlockSpec((1,) + w_ab.shape[1:], lambda i: (layer, 0, 0)),
            pl.BlockSpec((1,) + w_cb.shape[1:], lambda i: (layer, 0, 0)),
            pl.BlockSpec((1,) + w_o.shape[1:], lambda i: (layer, 0, 0)),
            pl.BlockSpec((1, d), const),
            pl.BlockSpec((d, LANES), const),
            pl.BlockSpec((d, LANES), const),
        ],
        out_specs=[
            pl.BlockSpec((tm, d), lambda i: (i, 0)),
            pl.BlockSpec((tm, d // 2), lambda i: (i, 0)),
            pl.BlockSpec((n_exp, tm), lambda i: (0, i)),
        ],
        out_shape=[
            jax.ShapeDtypeStruct((n_rows, d), F32),
            jax.ShapeDtypeStruct((n_rows, d // 2), U32),
            jax.ShapeDtypeStruct((n_exp, n_rows), F32),
        ],
        compiler_params=_params("arbitrary"),
        name="merge",
    )(xa, attn_lat, attn_ctx, rest, rest, rest, rest, rest, mod_l, conv_w, conv_b.reshape(1, cw), w_ab, w_cb, w_o,
      g2.reshape(1, d), wr_hi, wr_lo)


W_CHUNKS = 4
N_STREAM = 3 * W_CHUNKS


def _weight_stream_schedule(tile_expert, n_active):
    n_tiles = tile_expert.shape[0]
    t = jnp.arange(n_tiles, dtype=jnp.int32)
    active = t < n_active
    prev = jnp.concatenate([tile_expert[:1] - 1, tile_expert[:-1]])
    first = active & (tile_expert != prev)
    gid = jnp.cumsum(first.astype(jnp.int32)) - 1
    group_start = lax.cummax(jnp.where(first, t, 0))
    same = (gid[:, None] == gid[None, :]) & active[None, :]
    size = jnp.maximum(jnp.sum(same.astype(jnp.int32), axis=1), 1)
    pos = t - group_start
    nxt = group_start + size
    has_next = active & (nxt < n_active)
    next_expert = jnp.where(has_next, tile_expert[jnp.minimum(nxt, n_tiles - 1)], 0)
    base, rem = N_STREAM // size, N_STREAM % size
    quota = jnp.where(has_next, base + (pos < rem).astype(jnp.int32), 0)
    chunk0 = pos * base + jnp.minimum(pos, rem)
    return tuple(a.astype(jnp.int32) for a in (gid % 2, next_expert, chunk0, quota))


def _dispatch_plan(eidx, n_tok, tg, n_exp):
    ef = eidx[:2].reshape(-1)
    n2 = 2 * n_tok
    onehot = (ef[:, None] == jnp.arange(n_exp)[None, :]).astype(jnp.int32)
    csum = jnp.cumsum(onehot, axis=0)
    rank = jnp.sum(onehot * csum, axis=1) - 1
    counts = csum[-1]
    padded = ((counts + tg - 1) // tg) * tg
    gend = jnp.cumsum(padded)
    gstart = gend - padded
    dest = jnp.sum(onehot * gstart[None, :], axis=1) + rank
    n_slots = -(-n2 // tg) * tg + n_exp * tg
    tile_start = jnp.arange(n_slots // tg, dtype=jnp.int32) * tg
    tile_expert = jnp.sum((gend[None, :] <= tile_start[:, None]).astype(jnp.int32), axis=1)
    tile_expert = jnp.minimum(tile_expert, n_exp - 1).astype(jnp.int32)
    n_active = (gend[-1] // tg).astype(jnp.int32).reshape(1)
    slots = dest.reshape(2, n_tok).astype(jnp.int32)
    pad_start = jnp.concatenate([gstart + counts, gend[-1:]]).astype(jnp.int32)
    pad_count = jnp.concatenate([padded - counts, n_slots - gend[-1:]]).astype(jnp.int32)
    stream = _weight_stream_schedule(tile_expert, n_active[0])
    return slots, pad_start, pad_count, tile_expert, n_active, n_slots, stream


def _dispatch_kernel(d1_ref, d2_ref, ps_ref, pc_ref, h2_ref, xs_hbm, xbuf, zbuf, sem, zsem, *, tm, n_pad):
    i = pl.program_id(0)
    last = pl.num_programs(0) - 1
    slot = i % 2

    def wait_rows(s):
        for _ in range(2):
            pltpu.make_async_copy(xbuf.at[s], xs_hbm.at[pl.ds(0, tm), :], sem.at[s]).wait()

    def zero_copy(row):
        return pltpu.make_async_copy(zbuf.at[pl.ds(0, 1), :], xs_hbm.at[pl.ds(row, 1), :], zsem)

    @pl.when(i == 0)
    def _():
        zbuf[...] = jnp.zeros(zbuf.shape, zbuf.dtype)
        for e in range(n_pad):
            def start_zero(r, carry, e=e):
                zero_copy(ps_ref[e] + r).start()
                return carry
            lax.fori_loop(0, pc_ref[e], start_zero, 0)

    @pl.when(i >= 2)
    def _():
        wait_rows(slot)

    xbuf[slot] = h2_ref[...]

    def scatter(r, carry):
        for queue, d_ref in enumerate((d1_ref, d2_ref)):
            pltpu.make_async_copy(xbuf.at[slot, pl.ds(r, 1), :], xs_hbm.at[pl.ds(d_ref[i * tm + r], 1), :],
                                  sem.at[slot]).start(priority=queue)
        return carry

    lax.fori_loop(0, tm, scatter, 0, unroll=8)

    @pl.when(i == last)
    def _():
        wait_rows(slot)
        wait_rows(1 - slot)
        for e in range(n_pad):
            def wait_zero(r, carry):
                zero_copy(0).wait()
                return carry
            lax.fori_loop(0, pc_ref[e], wait_zero, 0)


def _dispatch(h2, plan, *, tm):
    slots, pad_start, pad_count, _, _, n_slots = plan[:6]
    n_tok, d = h2.shape
    assert n_tok // tm >= 2
    grid_spec = pltpu.PrefetchScalarGridSpec(
        num_scalar_prefetch=4,
        grid=(n_tok // tm,),
        in_specs=[pl.BlockSpec((tm, d), lambda i, *_: (i, 0))],
        out_specs=pl.BlockSpec(memory_space=pl.ANY),
        scratch_shapes=[pltpu.VMEM((2, tm, d), h2.dtype), pltpu.VMEM((SUBLANES, d), h2.dtype),
                        pltpu.SemaphoreType.DMA((2,)), pltpu.SemaphoreType.DMA(())],
    )
    return pl.pallas_call(
        functools.partial(_dispatch_kernel, tm=tm, n_pad=pad_start.shape[0]),
        grid_spec=grid_spec,
        out_shape=jax.ShapeDtypeStruct((n_slots, d), h2.dtype),
        compiler_params=_params("arbitrary"),
        name="dispatch",
    )(slots[0], slots[1], pad_start, pad_count, h2)


def _expert_kernel(te_ref, na_ref, par_ref, ne_ref, c0_ref, nq_ref, x_ref, wg_hbm, wu_hbm, wd_hbm, y_ref,
                   wg_s, wu_s, wd_s, stg_a, stg_b, sem_a, sem_b, *, layer):
    t = pl.program_id(0)
    d, f = wg_s.shape[1:]
    rows_a, rows_b = d // W_CHUNKS, f // W_CHUNKS

    def by_matrix(c, fn):
        if isinstance(c, int):
            fn(c // W_CHUNKS, c % W_CHUNKS)
            return
        for m in range(3):
            @pl.when((c >= m * W_CHUNKS) & (c < (m + 1) * W_CHUNKS))
            def _(m=m):
                fn(m, c - m * W_CHUNKS)

    def chunk_copy(e, m, j, s):
        if m < 2:
            src = (wg_hbm, wu_hbm)[m].at[layer, e, pl.ds(pl.multiple_of(j * rows_a, rows_a), rows_a), :]
            return pltpu.make_async_copy(src, stg_a.at[s], sem_a.at[s])
        src = wd_hbm.at[layer, e, pl.ds(pl.multiple_of(j * rows_b, rows_b), rows_b), :]
        return pltpu.make_async_copy(src, stg_b.at[s], sem_b.at[s])

    def start(e, c):
        by_matrix(c, lambda m, j: chunk_copy(e, m, j, c % 2).start())

    def convert(e, c, w):
        def fn(m, j):
            s = c % 2
            chunk_copy(e, m, j, s).wait()
            if m < 2:
                dst = (wg_s, wu_s)[m]
                dst[w, pl.ds(pl.multiple_of(j * rows_a, rows_a), rows_a), :] = stg_a[s].astype(BF16)
            else:
                wd_s[w, pl.ds(pl.multiple_of(j * rows_b, rows_b), rows_b), :] = stg_b[s].astype(BF16)
        by_matrix(c, fn)

    @pl.when(t == 0)
    def _():
        e0 = te_ref[0]
        start(e0, 0)
        start(e0, 1)
        for c in range(N_STREAM):
            convert(e0, c, 0)
            if c + 2 < N_STREAM:
                start(e0, c + 2)

    @pl.when(t < na_ref[0])
    def _():
        e_next, c0, n_conv = ne_ref[t], c0_ref[t], nq_ref[t]
        w_cur = par_ref[t]

        @pl.when((n_conv > 0) & (c0 == 0))
        def _():
            start(e_next, 0)
            start(e_next, 1)

        d2 = x_ref.shape[1]
        x_lo, x_hi = (h.astype(BF16) for h in _unpack_halves(x_ref[...]))
        gte = (jnp.dot(x_lo, wg_s[w_cur, 0:d2, :], preferred_element_type=F32)
               + jnp.dot(x_hi, wg_s[w_cur, d2:2 * d2, :], preferred_element_type=F32))
        up = (jnp.dot(x_lo, wu_s[w_cur, 0:d2, :], preferred_element_type=F32)
              + jnp.dot(x_hi, wu_s[w_cur, d2:2 * d2, :], preferred_element_type=F32))
        he = (gte * jax.nn.sigmoid(gte) * up).astype(BF16)
        y_ref[...] = _pack_halves(jnp.dot(he, wd_s[w_cur], preferred_element_type=F32))

        def convert_next(k, carry):
            c = c0 + k
            convert(e_next, c, 1 - w_cur)

            @pl.when(c + 2 < N_STREAM)
            def _():
                start(e_next, c + 2)
            return carry

        lax.fori_loop(0, n_conv, convert_next, 0)

    @pl.when(t >= na_ref[0])
    def _():
        y_ref[...] = jnp.zeros(y_ref.shape, U32)


def _experts(x_sorted, plan, wg, wu, wd, layer, *, tg):
    tile_expert, n_active, n_slots, stream = plan[3:7]
    d2 = x_sorted.shape[1]
    d = 2 * d2
    f = wg.shape[3]
    hbm = pl.BlockSpec(memory_space=pl.ANY)
    grid_spec = pltpu.PrefetchScalarGridSpec(
        num_scalar_prefetch=6,
        grid=(n_slots // tg,),
        in_specs=[pl.BlockSpec((tg, d2), lambda t, *_: (t, 0)), hbm, hbm, hbm],
        out_specs=pl.BlockSpec((tg, d2), lambda t, *_: (t, 0)),
        scratch_shapes=[
            pltpu.VMEM((2, d, f), BF16), pltpu.VMEM((2, d, f), BF16), pltpu.VMEM((2, f, d), BF16),
            pltpu.VMEM((2, d // W_CHUNKS, f), F32), pltpu.VMEM((2, f // W_CHUNKS, d), F32),
            pltpu.SemaphoreType.DMA((2,)), pltpu.SemaphoreType.DMA((2,)),
        ],
    )
    return pl.pallas_call(
        functools.partial(_expert_kernel, layer=layer),
        grid_spec=grid_spec,
        out_shape=jax.ShapeDtypeStruct((n_slots, d2), U32),
        compiler_params=_params("arbitrary"),
        name="experts",
    )(tile_expert, n_active, *stream, x_sorted, wg, wu, wd)


def _combine_kernel(s1_ref, s2_ref, x_ref, wt_ref, y_hbm, mod_ref, g_ref, modn_ref, xo_ref, *refs,
                    tm, seq, n_lat, nb, d, final):
    buf, sem = refs[-2], refs[-1]
    i = pl.program_id(0)
    last = pl.num_programs(0) - 1
    base = i * tm
    slot = i % 2

    def issue(tile, dst_slot):
        def body(r, carry):
            for k, s_ref in enumerate((s1_ref, s2_ref)):
                pltpu.make_async_copy(y_hbm.at[pl.ds(s_ref[tile * tm + r], 1), :],
                                      buf.at[dst_slot, k, pl.ds(r, 1), :], sem.at[dst_slot]).start(priority=k)
            return carry
        lax.fori_loop(0, tm, body, 0, unroll=8)

    def wait(dst_slot):
        for k in range(2):
            pltpu.make_async_copy(y_hbm.at[pl.ds(0, tm), :], buf.at[dst_slot, k], sem.at[dst_slot]).wait()

    @pl.when(i == 0)
    def _():
        issue(0, 0)

    @pl.when(i < last)
    def _():
        issue(i + 1, 1 - slot)

    wait(slot)
    seg = _segment(base, seq, n_lat, nb)
    m = mod_ref[pl.ds(seg, 1), :]
    y1_lo, y1_hi = _unpack_halves(buf[slot, 0])
    y2_lo, y2_hi = _unpack_halves(buf[slot, 1])
    w1, w2 = wt_ref[:, 0:1], wt_ref[:, 1:2]
    moe = jnp.concatenate([w1 * y1_lo + w2 * y2_lo, w1 * y1_hi + w2 * y2_hi], axis=1)
    xn = x_ref[...] + m[:, 5 * d:6 * d] * moe
    if final:
        xo_ref[...] = _rms(xn) * g_ref[...]
    else:
        xo_ref[...] = xn
        refs[0][...] = _adaln(xn, g_ref, modn_ref, seg, 0, d).astype(BF16)


def _combine(xa, y_sorted, slots, w_tok, mod_l, g_next, mod_next, *, tm, seq, n_lat, nb, final):
    n_rows, d = xa.shape
    row_spec = pl.BlockSpec((tm, d), lambda i, s1, s2: (i, 0))
    mod_spec = pl.BlockSpec((MOD_ROWS, 6 * d), lambda i, s1, s2: (0, 0))
    out_specs = [row_spec]
    out_shape = [jax.ShapeDtypeStruct((n_rows, d), F32)]
    if not final:
        out_specs.append(row_spec)
        out_shape.append(jax.ShapeDtypeStruct((n_rows, d), BF16))
    grid_spec = pltpu.PrefetchScalarGridSpec(
        num_scalar_prefetch=2,
        grid=(n_rows // tm,),
        in_specs=[
            row_spec,
            pl.BlockSpec((tm, 2), lambda i, s1, s2: (i, 0)),
            pl.BlockSpec(memory_space=pl.ANY),
            mod_spec,
            pl.BlockSpec((1, d), lambda i, s1, s2: (0, 0)),
            mod_spec,
        ],
        out_specs=out_specs,
        scratch_shapes=[pltpu.VMEM((2, 2, tm, d // 2), U32), pltpu.SemaphoreType.DMA((2,))],
    )
    kern = functools.partial(_combine_kernel, tm=tm, seq=seq, n_lat=n_lat, nb=nb, d=d, final=final)
    out = pl.pallas_call(
        kern,
        grid_spec=grid_spec,
        out_shape=out_shape,
        compiler_params=_params("arbitrary"),
        name="combine",
    )(slots[0], slots[1], xa, w_tok, y_sorted, mod_l, g_next.reshape(1, d), mod_next)
    return (out[0], None) if final else (out[0], out[1])


def _tile(limit, *sizes):
    t = limit
    while any(s % t for s in sizes):
        t //= 2
    return t


def kernel(x, c, ctx, c_ctx, w_mod, b_mod, norm1_g, norm2_g, w_in, lambda_q1, lambda_k1, lambda_q2,
           lambda_k2, subln_g, conv_w, conv_b, w_attn_branch, w_conv_branch, w_out, w_router,
           router_bias, w_exp_gate, w_exp_up, w_exp_down, final_g):
    nb, seq, d = x.shape
    n_ctx_tok = ctx.shape[1]
    depth = w_mod.shape[0]
    n_exp = w_router.shape[1]
    aw = d // 2
    n_lat = nb * seq
    nt = n_lat + nb * n_ctx_tok
    assert nb < MOD_ROWS and seq % GRID_W == 0 and aw % V_DIM == 0 and n_exp % N_GROUPS == 0

    tm_in = _tile(1024, seq, nb * n_ctx_tok)
    tm_mg = _tile(256, seq, n_ctx_tok)
    tq = _tile(512, seq)
    tg = _tile(256, tm_mg)

    cvec = jnp.zeros((MOD_ROWS, d), F32).at[:nb].set(c).at[nb].set(c_ctx)
    mod = _modulation(cvec, w_mod, b_mod)
    rope = _rope_tables(seq, tm_in)
    xa, h = _prenorm(x.reshape(n_lat, d), ctx.reshape(nb * n_ctx_tok, d), mod[0], norm1_g[0],
                     tm=tm_mg, seq=seq, nb=nb)
    w_ab, w_cb, w_o = (w.astype(BF16) for w in (w_attn_branch, w_conv_branch, w_out))

    for l in range(depth):
        last = l == depth - 1
        lam_init = 0.8 - 0.6 * math.exp(-0.3 * l)
        lam = (jnp.exp(jnp.sum(lambda_q1[l] * lambda_k1[l])) - jnp.exp(jnp.sum(lambda_q2[l] * lambda_k2[l]))
               + lam_init)
        lam_vec = jnp.stack([lam, jnp.asarray(1.0 - lam_init, F32)]).astype(F32)

        qkv, rest = _in_proj(h, w_in, l, rope, tm=tm_in, seq=seq, n_lat=n_lat)
        attn_lat, attn_ctx = _attention(qkv, lam_vec, subln_g[l], tq=tq, seq=seq, ctx=n_ctx_tok, n_lat=n_lat,
                                        nb=nb, aw=aw)
        n_rows = n_lat if last else nt
        xa, h2, logits_t = _merge(
            xa, attn_lat, attn_ctx, rest, mod[l], conv_w[l], conv_b[l], w_ab, w_cb, w_o, l, norm2_g[l],
            w_router, n_rows=n_rows, tm=tm_mg, seq=seq, ctx=n_ctx_tok, n_lat=n_lat, nb=nb)
        eidx, ew = _route_tokens(logits_t, router_bias)
        plan = _dispatch_plan(eidx, n_rows, tg, n_exp)
        x_sorted = _dispatch(h2, plan, tm=tm_mg)
        y_sorted = _experts(x_sorted, plan, w_exp_gate, w_exp_up, w_exp_down, l, tg=tg)
        g_next, mod_next = (final_g, mod[l]) if last else (norm1_g[l + 1], mod[l + 1])
        xa, h = _combine(xa, y_sorted, plan[0], ew[:2].T, mod[l], g_next, mod_next, tm=tm_mg, seq=seq,
                         n_lat=n_lat, nb=nb, final=last)

    return xa.reshape(nb, seq, d)
```

```python
import functools
import math

import jax
import jax.numpy as jnp
from jax import lax
from jax.experimental import pallas as pl
from jax.experimental.pallas import tpu as pltpu

HEAD_DIM = 64
V_DIM = 2 * HEAD_DIM
GRID_W = 64
ROPE_THETA = 10000.0
ROPE_AXIS_DIM = HEAD_DIM // 2
N_GROUPS = 4
EPS = 1e-6
LANES = 128
SUBLANES = 8
HALO_ROWS = 2 * SUBLANES
VMEM_LIMIT_BYTES = 56 * 1024 * 1024
MOD_ROWS = 8
LOG2E = math.log2(math.e)

F32 = jnp.float32
BF16 = jnp.bfloat16
HIGHEST = lax.Precision.HIGHEST
NT_DIMS = (((1,), (1,)), ((), ()))


def _params(*sem):
    return pltpu.CompilerParams(dimension_semantics=sem, vmem_limit_bytes=VMEM_LIMIT_BYTES)


def _rms(x):
    return x * lax.rsqrt(jnp.mean(x * x, axis=-1, keepdims=True) + EPS)


U32 = jnp.uint32
HIGH_HALF = 0xFFFF0000


def _pack_halves(x):
    c2 = x.shape[1] // 2
    lo = lax.bitcast_convert_type(x[:, 0:c2].astype(BF16).astype(F32), U32) >> 16
    hi = lax.bitcast_convert_type(x[:, c2:2 * c2].astype(BF16).astype(F32), U32) & U32(HIGH_HALF)
    return hi | lo


def _unpack_halves(w):
    return (lax.bitcast_convert_type(w << 16, F32), lax.bitcast_convert_type(w & U32(HIGH_HALF), F32))


def _segment(row0, seq, n_lat, nb):
    return jnp.where(row0 < n_lat, row0 // seq, nb)


def _adaln(x, g_ref, mod_ref, seg, shift_col, d):
    m = mod_ref[pl.ds(seg, 1), :]
    return _rms(x) * g_ref[...] * (1.0 + m[:, (shift_col + 1) * d:(shift_col + 2) * d]) + m[:, shift_col * d:(shift_col + 1) * d]


def _mod_kernel(c_ref, w_ref, b_ref, o_ref):
    c = c_ref[...]
    sc = c * jax.nn.sigmoid(c)
    o_ref[0] = jnp.dot(sc, w_ref[0], precision=HIGHEST, preferred_element_type=F32) + b_ref[0]


def _modulation(cvec, w_mod, b_mod):
    depth, d, n6 = w_mod.shape
    tn = _tile(1024, n6)
    return pl.pallas_call(
        _mod_kernel,
        grid=(depth, n6 // tn),
        in_specs=[
            pl.BlockSpec((MOD_ROWS, d), lambda l, j: (0, 0)),
            pl.BlockSpec((1, d, tn), lambda l, j: (l, 0, j)),
            pl.BlockSpec((1, 1, tn), lambda l, j: (l, 0, j)),
        ],
        out_specs=pl.BlockSpec((1, MOD_ROWS, tn), lambda l, j: (l, 0, j)),
        out_shape=jax.ShapeDtypeStruct((depth, MOD_ROWS, n6), F32),
        compiler_params=_params("arbitrary", "arbitrary"),
        name="modulation",
    )(cvec, w_mod, b_mod.reshape(depth, 1, n6))


def _prenorm_kernel(x_ref, c_ref, mod_ref, g_ref, xa_ref, h_ref, *, tm, seq, n_lat, nb, d):
    r0 = pl.program_id(0) * tm
    x = jnp.where(r0 < n_lat, x_ref[...], c_ref[...])
    xa_ref[...] = x
    h_ref[...] = _adaln(x, g_ref, mod_ref, _segment(r0, seq, n_lat, nb), 0, d).astype(BF16)


def _prenorm(x2, c2, mod_l, g1, *, tm, seq, nb):
    n_lat, d = x2.shape
    nt = n_lat + c2.shape[0]
    lat_tiles = n_lat // tm
    ctx_tiles = c2.shape[0] // tm
    row_spec = pl.BlockSpec((tm, d), lambda i: (i, 0))
    return pl.pallas_call(
        functools.partial(_prenorm_kernel, tm=tm, seq=seq, n_lat=n_lat, nb=nb, d=d),
        grid=(nt // tm,),
        in_specs=[
            pl.BlockSpec((tm, d), lambda i: (jnp.minimum(i, lat_tiles - 1), 0)),
            pl.BlockSpec((tm, d), lambda i: (jnp.clip(i - lat_tiles, 0, ctx_tiles - 1), 0)),
            pl.BlockSpec((MOD_ROWS, 6 * d), lambda i: (0, 0)),
            pl.BlockSpec((1, d), lambda i: (0, 0)),
        ],
        out_specs=[row_spec, row_spec],
        out_shape=[jax.ShapeDtypeStruct((nt, d), F32), jax.ShapeDtypeStruct((nt, d), BF16)],
        compiler_params=_params("arbitrary"),
        name="prenorm",
    )(x2, c2, mod_l, g1.reshape(1, d))


def _in_kernel(h_ref, w_ref, *refs, rope, aw):
    o_ref, w_scr = refs[-2], refs[-1]
    j = pl.program_id(0)

    @pl.when(pl.program_id(1) == 0)
    def _():
        w_scr[...] = w_ref[0].astype(BF16)

    acc = jnp.dot(h_ref[...], w_scr[...], preferred_element_type=F32)
    if not rope:
        o_ref[...] = acc.astype(o_ref.dtype)
        return
    rc_ref, rs_ref = refs[:2]

    @pl.when(j < 2)
    def _():
        scale = jnp.where(j == 0, HEAD_DIM ** -0.5 * LOG2E, 1.0).astype(F32)
        rc, rs = rc_ref[...], rs_ref[...]
        for c in range(aw // LANES):
            a = acc[:, c * LANES:(c + 1) * LANES]
            r = a * rc + pltpu.roll(a, HEAD_DIM, 1) * rs
            o_ref[:, c * LANES:(c + 1) * LANES] = (r * scale).astype(BF16)

    @pl.when(j == 2)
    def _():
        o_ref[...] = acc.astype(BF16)


ROPE_FREQS = ROPE_AXIS_DIM // 2


def _rotary_layout(w_in, aw):
    depth, d, _ = w_in.shape
    qk = w_in[:, :, :2 * aw].reshape(depth, d, 2 * aw // V_DIM, 2, 2, 2, ROPE_FREQS)
    qk = qk.transpose(0, 1, 2, 5, 3, 4, 6).reshape(depth, d, 2 * aw)
    return jnp.concatenate([qk, w_in[:, :, 2 * aw:3 * aw]], axis=2)


def _rope_tables(seq, tm):
    pos = jnp.arange(seq)
    row = (pos // GRID_W).astype(F32)
    col = (pos % GRID_W).astype(F32)
    inv = ROPE_THETA ** (-jnp.arange(0, ROPE_AXIS_DIM, 2, dtype=F32) / ROPE_AXIS_DIM)
    lane = jnp.arange(LANES)
    half = lane // HEAD_DIM
    axis = (lane % ROPE_AXIS_DIM) // ROPE_FREQS
    f = lane % ROPE_FREQS
    posv = jnp.where(axis[None, :] == 0, row[:, None], col[:, None])
    ang = posv * inv[f][None, :]
    cos, sin = jnp.cos(ang), jnp.sin(ang)
    rc = jnp.concatenate([cos, jnp.ones((tm, LANES), F32)], axis=0)
    rs = jnp.concatenate([jnp.where(half[None, :] == 0, -sin, sin), jnp.zeros((tm, LANES), F32)], axis=0)
    return rc, rs


def _in_proj(h, w_qkv, w_in, layer, rope, *, tm, seq, n_lat):
    nt, d = h.shape
    in_cols = w_in.shape[2]
    aw = d // 2
    n_qkv = 3
    n_rest = in_cols // aw - n_qkv
    n_lat_tiles = n_lat // tm
    seq_tiles = seq // tm

    def rope_idx(j, i):
        return (jnp.where(i < n_lat_tiles, i % seq_tiles, seq_tiles), 0)

    def call(w, col0, n_col, out_dtype, with_rope):
        in_specs = [
            pl.BlockSpec((tm, d), lambda j, i: (i, 0)),
            pl.BlockSpec((1, d, aw), lambda j, i: (layer, 0, col0 + j)),
        ]
        args = [h, w]
        if with_rope:
            in_specs += [pl.BlockSpec((tm, LANES), rope_idx)] * len(rope)
            args += list(rope)
        return pl.pallas_call(
            functools.partial(_in_kernel, rope=with_rope, aw=aw),
            grid=(n_col, nt // tm),
            in_specs=in_specs,
            out_specs=pl.BlockSpec((tm, aw), lambda j, i: (i, j)),
            out_shape=jax.ShapeDtypeStruct((nt, n_col * aw), out_dtype),
            scratch_shapes=[pltpu.VMEM((d, aw), BF16)],
            compiler_params=_params("arbitrary", "arbitrary"),
            name="in_proj_qkv" if with_rope else "in_proj_rest",
        )(*args)

    return call(w_qkv, 0, n_qkv, BF16, True), call(w_in, n_qkv, n_rest, BF16, False)


EXP_CHUNK = 128
ATTN_HALF = 512


def _split_maps(q_ref, half):
    q_maps = []
    for hh in range(q_ref.shape[0] // half):
        q = q_ref[hh * half:(hh + 1) * half, :]
        lane = lax.broadcasted_iota(jnp.int32, q.shape, 1)
        zero = jnp.zeros_like(q)
        first = (lane % HEAD_DIM) < HEAD_DIM // 2
        q_maps += [jnp.where(first, q, zero), jnp.where(first, zero, q)]
    return q_maps


def _finish_heads(outs, lam_ref, g_ref, o_ref, half):
    for hh in range(len(outs) // 2):
        a, b = outs[2 * hh], outs[2 * hh + 1]
        o1 = a[:, 0:V_DIM] / a[:, V_DIM:V_DIM + 1]
        o2 = b[:, 0:V_DIM] * (lam_ref[0] / b[:, V_DIM:V_DIM + 1])
        o = _rms(o1 - o2) * g_ref[...] * lam_ref[1]
        o_ref[hh * half:(hh + 1) * half, :] = o.astype(BF16)


def _diff_attention(lam_ref, q_ref, k_ref, v_ref, g_ref, o_ref, s_refs, e_refs, k0, n, half):
    q_maps = _split_maps(q_ref, half)
    n_chain = len(q_maps)
    outs = [None] * n_chain

    def scores(i):
        s_refs[i][:, 0:n] = lax.dot_general(q_maps[i], k_ref[k0:k0 + n, :], NT_DIMS,
                                            preferred_element_type=F32)

    def exps(i):
        mx = jnp.max(s_refs[i][:, 0:n], axis=-1, keepdims=True)
        for c in range(n // EXP_CHUNK):
            sl = slice(c * EXP_CHUNK, (c + 1) * EXP_CHUNK)
            e_refs[i][:, sl] = jnp.exp2(s_refs[i][:, sl] - mx).astype(BF16)

    def values(i):
        outs[i] = jnp.dot(e_refs[i][:, 0:n], v_ref[k0:k0 + n, :], preferred_element_type=F32)

    scores(0)
    for i in range(n_chain):
        if i + 1 < n_chain:
            scores(i + 1)
        exps(i)
        if i >= 1:
            values(i - 1)
    values(n_chain - 1)
    _finish_heads(outs, lam_ref, g_ref, o_ref, half)


def _attn_lat_kernel(lam_ref, q_ref, kl_ref, kc_ref, vl_ref, vc_ref, g_ref, o_ref, k_scr, v_scr,
                     *chain_scr, seq, ctx, half):
    nk = seq + ctx

    @pl.when(pl.program_id(2) == 0)
    def _():
        k_scr[0:seq, :] = kl_ref[...]
        k_scr[seq:nk, :] = kc_ref[...]
        v_scr[0:seq, 0:V_DIM] = vl_ref[...]
        v_scr[seq:nk, 0:V_DIM] = vc_ref[...]
        v_scr[:, V_DIM:2 * V_DIM] = jnp.ones((nk, V_DIM), BF16)

    n_chain = len(chain_scr) // 2
    _diff_attention(lam_ref, q_ref, k_scr, v_scr, g_ref, o_ref, chain_scr[:n_chain], chain_scr[n_chain:],
                    0, nk, half)


def _attn_ctx_kernel(lam_ref, q_ref, k_ref, v_ref, g_ref, o_ref, v_scr, *chain_scr, ctx, half):
    v_scr[:, 0:V_DIM] = v_ref[...]
    v_scr[:, V_DIM:2 * V_DIM] = jnp.ones((ctx, V_DIM), BF16)
    n_chain = len(chain_scr) // 2
    _diff_attention(lam_ref, q_ref, k_ref, v_scr, g_ref, o_ref, chain_scr[:n_chain], chain_scr[n_chain:],
                    0, ctx, half)


def _attention(qkv, lam_vec, subln_g, *, tq, seq, ctx, n_lat, nb, aw):
    nh = aw // V_DIM
    kcol = aw // V_DIM
    vcol = 2 * aw // V_DIM
    ctx0 = n_lat // ctx
    qt = seq // tq
    nk = seq + ctx
    g = subln_g.reshape(1, V_DIM)
    smem = pl.BlockSpec(memory_space=pltpu.SMEM)

    def chain_scratch(rows, half, n):
        n_chain = 2 * (rows // half)
        return ([pltpu.VMEM((half, n), F32)] * n_chain) + ([pltpu.VMEM((half, n), BF16)] * n_chain)

    half = _tile(ATTN_HALF, tq)
    lat = pl.pallas_call(
        functools.partial(_attn_lat_kernel, seq=seq, ctx=ctx, half=half),
        grid=(nb, nh, qt),
        in_specs=[
            smem,
            pl.BlockSpec((tq, V_DIM), lambda b, h, t: (b * qt + t, h)),
            pl.BlockSpec((seq, V_DIM), lambda b, h, t: (b, kcol + h)),
            pl.BlockSpec((ctx, V_DIM), lambda b, h, t: (ctx0 + b, kcol + h)),
            pl.BlockSpec((seq, V_DIM), lambda b, h, t: (b, vcol + h)),
            pl.BlockSpec((ctx, V_DIM), lambda b, h, t: (ctx0 + b, vcol + h)),
            pl.BlockSpec((1, V_DIM), lambda b, h, t: (0, 0)),
        ],
        out_specs=pl.BlockSpec((tq, V_DIM), lambda b, h, t: (b * qt + t, h)),
        out_shape=jax.ShapeDtypeStruct((n_lat, aw), BF16),
        scratch_shapes=[pltpu.VMEM((nk, V_DIM), BF16), pltpu.VMEM((nk, 2 * V_DIM), BF16)]
        + chain_scratch(tq, half, nk),
        compiler_params=_params("arbitrary", "arbitrary", "arbitrary"),
        name="attn_latent",
    )(lam_vec, qkv, qkv, qkv, qkv, qkv, g)

    half_c = _tile(ATTN_HALF, ctx)
    cx = pl.pallas_call(
        functools.partial(_attn_ctx_kernel, ctx=ctx, half=half_c),
        grid=(nb, nh),
        in_specs=[
            smem,
            pl.BlockSpec((ctx, V_DIM), lambda b, h: (ctx0 + b, h)),
            pl.BlockSpec((ctx, V_DIM), lambda b, h: (ctx0 + b, kcol + h)),
            pl.BlockSpec((ctx, V_DIM), lambda b, h: (ctx0 + b, vcol + h)),
            pl.BlockSpec((1, V_DIM), lambda b, h: (0, 0)),
        ],
        out_specs=pl.BlockSpec((ctx, V_DIM), lambda b, h: (b, h)),
        out_shape=jax.ShapeDtypeStruct((nb * ctx, aw), BF16),
        scratch_shapes=[pltpu.VMEM((ctx, 2 * V_DIM), BF16)] + chain_scratch(ctx, half_c, ctx),
        compiler_params=_params("arbitrary", "arbitrary"),
        name="attn_context",
    )(lam_vec, qkv, qkv, qkv, g)
    return lat, cx


def _top2_of4(a, b, c, d):
    m01, n01 = jnp.maximum(a, b), jnp.minimum(a, b)
    m23, n23 = jnp.maximum(c, d), jnp.minimum(c, d)
    return jnp.maximum(m01, m23) + jnp.maximum(jnp.minimum(m01, m23), jnp.maximum(n01, n23))


def _route(logits_t, bias_ref):
    n_exp = logits_t.shape[0]
    per = n_exp // N_GROUPS
    s = [jax.nn.sigmoid(logits_t[e:e + 1, :]) for e in range(n_exp)]
    sb = [s[e] + bias_ref[e:e + 1, :] for e in range(n_exp)]
    gscore = [_top2_of4(*sb[g * per:(g + 1) * per]) for g in range(N_GROUPS)]
    best, gidx = gscore[0], jnp.zeros_like(gscore[0], dtype=jnp.int32)
    for g in range(1, N_GROUPS):
        better = gscore[g] > best
        gidx = jnp.where(better, g, gidx)
        best = jnp.where(better, gscore[g], best)
    cand_b, cand_s = [], []
    for jx in range(per):
        vb, vs = sb[jx], s[jx]
        for g in range(1, N_GROUPS):
            sel = gidx == g
            vb = jnp.where(sel, sb[g * per + jx], vb)
            vs = jnp.where(sel, s[g * per + jx], vs)
        cand_b.append(vb)
        cand_s.append(vs)

    def argmax_first(vals, exclude):
        bv = bi = bs = None
        for jx in range(per):
            v = vals[jx] if exclude is None else jnp.where(exclude == jx, -jnp.inf, vals[jx])
            if bv is None:
                bv, bi, bs = v, jnp.zeros_like(gidx), cand_s[jx]
            else:
                better = v > bv
                bi = jnp.where(better, jx, bi)
                bs = jnp.where(better, cand_s[jx], bs)
                bv = jnp.where(better, v, bv)
        return bi, bs

    j1, w1 = argmax_first(cand_b, None)
    j2, w2 = argmax_first(cand_b, j1)
    tot = w1 + w2
    return gidx * per + j1, gidx * per + j2, w1 / tot, w2 / tot


def _merge_kernel(x_ref, attn_lat_ref, attn_ctx_ref, rest_ref, ccp_ref, cxp_ref, ccn_ref, cxn_ref, mod_ref,
                  convw_ref, convb_ref, wab_ref, wcb_ref, wo_ref, g2_ref, wrh_ref, wrl_ref,
                  xo_ref, h2_ref, logit_ref, *, tm, seq, ctx, n_lat, nb, d, cw):
    i = pl.program_id(0)
    r0 = i * tm
    is_lat = r0 < n_lat
    seg = _segment(r0, seq, n_lat, nb)
    pos = jnp.where(is_lat, r0 % seq, (r0 - n_lat) % ctx)
    slen = jnp.where(is_lat, seq, ctx)
    has_prev = (pos > 0).astype(F32)
    has_next = (pos + tm < slen).astype(F32)

    cb = rest_ref[:, 0:cw].astype(F32)
    u = rest_ref[:, cw:2 * cw].astype(F32) * rest_ref[:, 2 * cw:3 * cw].astype(F32)
    ga = rest_ref[:, 3 * cw:3 * cw + d].astype(F32)
    gc = rest_ref[:, 3 * cw + d:3 * cw + 2 * d].astype(F32)
    hl = HALO_ROWS - 1
    halo_prev = ccp_ref[hl:hl + 1, :].astype(F32) * cxp_ref[hl:hl + 1, :].astype(F32) * has_prev
    halo_next = ccn_ref[0:1, :].astype(F32) * cxn_ref[0:1, :].astype(F32) * has_next
    rid = lax.broadcasted_iota(jnp.int32, u.shape, 0)
    u_prev = jnp.where(rid == 0, halo_prev, pltpu.roll(u, 1, 0))
    u_next = jnp.where(rid == tm - 1, halo_next, pltpu.roll(u, tm - 1, 0))
    y = cb * (u_prev * convw_ref[0:1, :] + u * convw_ref[1:2, :] + u_next * convw_ref[2:3, :]
              + convb_ref[...])

    attn = jnp.where(is_lat, attn_lat_ref[...], attn_ctx_ref[...])
    ma = jnp.dot(attn, wab_ref[0], preferred_element_type=F32)
    mc = jnp.dot(y.astype(BF16), wcb_ref[0], preferred_element_type=F32)
    merged = jax.nn.sigmoid(ga) * ma + jax.nn.sigmoid(gc) * mc
    out = jnp.dot(merged.astype(BF16), wo_ref[0], preferred_element_type=F32)

    m = mod_ref[pl.ds(seg, 1), :]
    xn = x_ref[...] + m[:, 2 * d:3 * d] * out
    xo_ref[...] = xn
    h2 = _adaln(xn, g2_ref, mod_ref, seg, 3, d)
    h2_ref[...] = _pack_halves(h2)

    h_hi = h2.astype(BF16)
    h_lo = (h2 - h_hi.astype(F32)).astype(BF16)
    w_hi = wrh_ref[...]
    logits = (jnp.dot(h_hi, w_hi, preferred_element_type=F32)
              + jnp.dot(h_lo, w_hi, preferred_element_type=F32)
              + jnp.dot(h_hi, wrl_ref[...], preferred_element_type=F32))
    logit_ref[...] = logits.T[0:logit_ref.shape[0], :]


def _route_kernel(logit_ref, rbias_ref, eidx_ref, ew_ref):
    e1, e2, w1, w2 = _route(logit_ref[...], rbias_ref)
    eidx_ref[...] = jnp.zeros(eidx_ref.shape, jnp.int32)
    ew_ref[...] = jnp.zeros(ew_ref.shape, F32)
    eidx_ref[0:1, :] = e1
    eidx_ref[1:2, :] = e2
    ew_ref[0:1, :] = w1
    ew_ref[1:2, :] = w2


def _route_tokens(logits_t, router_bias):
    n_exp, n_rows = logits_t.shape
    tr = _tile(2048, n_rows)
    out_spec = pl.BlockSpec((SUBLANES, tr), lambda i: (0, i))
    return pl.pallas_call(
        _route_kernel,
        grid=(n_rows // tr,),
        in_specs=[pl.BlockSpec((n_exp, tr), lambda i: (0, i)), pl.BlockSpec((n_exp, 1), lambda i: (0, 0))],
        out_specs=[out_spec, out_spec],
        out_shape=[jax.ShapeDtypeStruct((SUBLANES, n_rows), jnp.int32),
                   jax.ShapeDtypeStruct((SUBLANES, n_rows), F32)],
        compiler_params=_params("arbitrary"),
        name="route",
    )(logits_t, router_bias.reshape(n_exp, 1))


def _merge(xa, attn_lat, attn_ctx, rest, mod_l, conv_w, conv_b, w_ab, w_cb, w_o, layer, g2, w_router,
           *, n_rows, tm, seq, ctx, n_lat, nb):
    nt, d = xa.shape
    aw = attn_lat.shape[1]
    lat_tiles = n_lat // tm
    ctx_tiles = attn_ctx.shape[0] // tm
    cw = conv_w.shape[1]
    n_exp = w_router.shape[1]
    wr_pad = jnp.pad(w_router, ((0, 0), (0, LANES - n_exp)))
    wr_hi = wr_pad.astype(BF16)
    wr_lo = (wr_pad - wr_hi.astype(F32)).astype(BF16)
    n_rest = rest.shape[1]
    hb = tm // HALO_ROWS
    last_hblk = nt // HALO_ROWS - 1

    def prev_idx(col):
        return lambda i: (jnp.maximum(i * hb - 1, 0), col)

    def next_idx(col):
        return lambda i: (jnp.minimum((i + 1) * hb, last_hblk), col)

    const = lambda i: (0, 0)
    kern = functools.partial(_merge_kernel, tm=tm, seq=seq, ctx=ctx, n_lat=n_lat, nb=nb, d=d, cw=cw)
    return pl.pallas_call(
        kern,
        grid=(n_rows // tm,),
        in_specs=[
            pl.BlockSpec((tm, d), lambda i: (i, 0)),
            pl.BlockSpec((tm, aw), lambda i: (jnp.minimum(i, lat_tiles - 1), 0)),
            pl.BlockSpec((tm, aw), lambda i: (jnp.clip(i - lat_tiles, 0, ctx_tiles - 1), 0)),
            pl.BlockSpec((tm, n_rest), lambda i: (i, 0)),
            pl.BlockSpec((HALO_ROWS, cw), prev_idx(1)),
            pl.BlockSpec((HALO_ROWS, cw), prev_idx(2)),
            pl.BlockSpec((HALO_ROWS, cw), next_idx(1)),
            pl.BlockSpec((HALO_ROWS, cw), next_idx(2)),
            pl.BlockSpec((MOD_ROWS, 6 * d), const),
            pl.BlockSpec((3, cw), const),
            pl.BlockSpec((1, cw), const),
            pl.BlockSpec((1,) + w_ab.shape[1:], lambda i: (layer, 0, 0)),
            pl.BlockSpec((1,) + w_cb.shape[1:], lambda i: (layer, 0, 0)),
            pl.BlockSpec((1,) + w_o.shape[1:], lambda i: (layer, 0, 0)),
            pl.BlockSpec((1, d), const),
            pl.BlockSpec((d, LANES), const),
            pl.BlockSpec((d, LANES), const),
        ],
        out_specs=[
            pl.BlockSpec((tm, d), lambda i: (i, 0)),
            pl.BlockSpec((tm, d // 2), lambda i: (i, 0)),
            pl.BlockSpec((n_exp, tm), lambda i: (0, i)),
        ],
        out_shape=[
            jax.ShapeDtypeStruct((n_rows, d), F32),
            jax.ShapeDtypeStruct((n_rows, d // 2), U32),
            jax.ShapeDtypeStruct((n_exp, n_rows), F32),
        ],
        compiler_params=_params("arbitrary"),
        name="merge",
    )(xa, attn_lat, attn_ctx, rest, rest, rest, rest, rest, mod_l, conv_w, conv_b.reshape(1, cw), w_ab, w_cb, w_o,
      g2.reshape(1, d), wr_hi, wr_lo)


W_CHUNKS = 4
N_STREAM = 3 * W_CHUNKS


def _weight_stream_schedule(tile_expert, n_active):
    n_tiles = tile_expert.shape[0]
    t = jnp.arange(n_tiles, dtype=jnp.int32)
    active = t < n_active
    prev = jnp.concatenate([tile_expert[:1] - 1, tile_expert[:-1]])
    first = active & (tile_expert != prev)
    gid = jnp.cumsum(first.astype(jnp.int32)) - 1
    group_start = lax.cummax(jnp.where(first, t, 0))
    same = (gid[:, None] == gid[None, :]) & active[None, :]
    size = jnp.maximum(jnp.sum(same.astype(jnp.int32), axis=1), 1)
    pos = t - group_start
    nxt = group_start + size
    has_next = active & (nxt < n_active)
    next_expert = jnp.where(has_next, tile_expert[jnp.minimum(nxt, n_tiles - 1)], 0)
    base, rem = N_STREAM // size, N_STREAM % size
    quota = jnp.where(has_next, base + (pos < rem).astype(jnp.int32), 0)
    chunk0 = pos * base + jnp.minimum(pos, rem)
    return tuple(a.astype(jnp.int32) for a in (gid % 2, next_expert, chunk0, quota))


def _dispatch_plan(eidx, n_tok, tg, n_exp):
    ef = eidx[:2].reshape(-1)
    n2 = 2 * n_tok
    onehot = (ef[:, None] == jnp.arange(n_exp)[None, :]).astype(jnp.int32)
    csum = jnp.cumsum(onehot, axis=0)
    rank = jnp.sum(onehot * csum, axis=1) - 1
    counts = csum[-1]
    padded = ((counts + tg - 1) // tg) * tg
    gend = jnp.cumsum(padded)
    gstart = gend - padded
    dest = jnp.sum(onehot * gstart[None, :], axis=1) + rank
    n_slots = -(-n2 // tg) * tg + n_exp * tg
    tile_start = jnp.arange(n_slots // tg, dtype=jnp.int32) * tg
    tile_expert = jnp.sum((gend[None, :] <= tile_start[:, None]).astype(jnp.int32), axis=1)
    tile_expert = jnp.minimum(tile_expert, n_exp - 1).astype(jnp.int32)
    n_active = (gend[-1] // tg).astype(jnp.int32).reshape(1)
    slots = dest.reshape(2, n_tok).astype(jnp.int32)
    pad_start = jnp.concatenate([gstart + counts, gend[-1:]]).astype(jnp.int32)
    pad_count = jnp.concatenate([padded - counts, n_slots - gend[-1:]]).astype(jnp.int32)
    stream = _weight_stream_schedule(tile_expert, n_active[0])
    return slots, pad_start, pad_count, tile_expert, n_active, n_slots, stream


def _dispatch_kernel(d1_ref, d2_ref, ps_ref, pc_ref, h2_ref, xs_hbm, xbuf, zbuf, sem, zsem, *, tm, n_pad):
    i = pl.program_id(0)
    last = pl.num_programs(0) - 1
    slot = i % 2

    def wait_rows(s):
        for _ in range(2):
            pltpu.make_async_copy(xbuf.at[s], xs_hbm.at[pl.ds(0, tm), :], sem.at[s]).wait()

    def zero_copy(row):
        return pltpu.make_async_copy(zbuf.at[pl.ds(0, 1), :], xs_hbm.at[pl.ds(row, 1), :], zsem)

    @pl.when(i == 0)
    def _():
        zbuf[...] = jnp.zeros(zbuf.shape, zbuf.dtype)
        for e in range(n_pad):
            def start_zero(r, carry, e=e):
                zero_copy(ps_ref[e] + r).start()
                return carry
            lax.fori_loop(0, pc_ref[e], start_zero, 0)

    @pl.when(i >= 2)
    def _():
        wait_rows(slot)

    xbuf[slot] = h2_ref[...]

    def scatter(r, carry):
        for d_ref in (d1_ref, d2_ref):
            pltpu.make_async_copy(xbuf.at[slot, pl.ds(r, 1), :], xs_hbm.at[pl.ds(d_ref[i * tm + r], 1), :],
                                  sem.at[slot]).start()
        return carry

    lax.fori_loop(0, tm, scatter, 0, unroll=8)

    @pl.when(i == last)
    def _():
        wait_rows(slot)
        wait_rows(1 - slot)
        for e in range(n_pad):
            def wait_zero(r, carry):
                zero_copy(0).wait()
                return carry
            lax.fori_loop(0, pc_ref[e], wait_zero, 0)


def _dispatch(h2, plan, *, tm):
    slots, pad_start, pad_count, _, _, n_slots = plan[:6]
    n_tok, d = h2.shape
    assert n_tok // tm >= 2
    grid_spec = pltpu.PrefetchScalarGridSpec(
        num_scalar_prefetch=4,
        grid=(n_tok // tm,),
        in_specs=[pl.BlockSpec((tm, d), lambda i, *_: (i, 0))],
        out_specs=pl.BlockSpec(memory_space=pl.ANY),
        scratch_shapes=[pltpu.VMEM((2, tm, d), h2.dtype), pltpu.VMEM((SUBLANES, d), h2.dtype),
                        pltpu.SemaphoreType.DMA((2,)), pltpu.SemaphoreType.DMA(())],
    )
    return pl.pallas_call(
        functools.partial(_dispatch_kernel, tm=tm, n_pad=pad_start.shape[0]),
        grid_spec=grid_spec,
        out_shape=jax.ShapeDtypeStruct((n_slots, d), h2.dtype),
        compiler_params=_params("arbitrary"),
        name="dispatch",
    )(slots[0], slots[1], pad_start, pad_count, h2)


def _expert_kernel(te_ref, na_ref, par_ref, ne_ref, c0_ref, nq_ref, x_ref, wg_hbm, wu_hbm, wd_hbm, y_ref,
                   wg_s, wu_s, wd_s, stg_a, stg_b, sem_a, sem_b, *, layer):
    t = pl.program_id(0)
    d, f = wg_s.shape[1:]
    rows_a, rows_b = d // W_CHUNKS, f // W_CHUNKS

    def by_matrix(c, fn):
        if isinstance(c, int):
            fn(c // W_CHUNKS, c % W_CHUNKS)
            return
        for m in range(3):
            @pl.when((c >= m * W_CHUNKS) & (c < (m + 1) * W_CHUNKS))
            def _(m=m):
                fn(m, c - m * W_CHUNKS)

    def chunk_copy(e, m, j, s):
        if m < 2:
            src = (wg_hbm, wu_hbm)[m].at[layer, e, pl.ds(pl.multiple_of(j * rows_a, rows_a), rows_a), :]
            return pltpu.make_async_copy(src, stg_a.at[s], sem_a.at[s])
        src = wd_hbm.at[layer, e, pl.ds(pl.multiple_of(j * rows_b, rows_b), rows_b), :]
        return pltpu.make_async_copy(src, stg_b.at[s], sem_b.at[s])

    def start(e, c):
        by_matrix(c, lambda m, j: chunk_copy(e, m, j, c % 2).start())

    def convert(e, c, w):
        def fn(m, j):
            s = c % 2
            chunk_copy(e, m, j, s).wait()
            if m < 2:
                dst = (wg_s, wu_s)[m]
                dst[w, pl.ds(pl.multiple_of(j * rows_a, rows_a), rows_a), :] = stg_a[s].astype(BF16)
            else:
                wd_s[w, pl.ds(pl.multiple_of(j * rows_b, rows_b), rows_b), :] = stg_b[s].astype(BF16)
        by_matrix(c, fn)

    @pl.when(t == 0)
    def _():
        e0 = te_ref[0]
        start(e0, 0)
        start(e0, 1)
        for c in range(N_STREAM):
            convert(e0, c, 0)
            if c + 2 < N_STREAM:
                start(e0, c + 2)

    @pl.when(t < na_ref[0])
    def _():
        e_next, c0, n_conv = ne_ref[t], c0_ref[t], nq_ref[t]
        w_cur = par_ref[t]

        @pl.when((n_conv > 0) & (c0 == 0))
        def _():
            start(e_next, 0)
            start(e_next, 1)

        d2 = x_ref.shape[1]
        x_lo, x_hi = (h.astype(BF16) for h in _unpack_halves(x_ref[...]))
        gte = (jnp.dot(x_lo, wg_s[w_cur, 0:d2, :], preferred_element_type=F32)
               + jnp.dot(x_hi, wg_s[w_cur, d2:2 * d2, :], preferred_element_type=F32))
        up = (jnp.dot(x_lo, wu_s[w_cur, 0:d2, :], preferred_element_type=F32)
              + jnp.dot(x_hi, wu_s[w_cur, d2:2 * d2, :], preferred_element_type=F32))
        he = (gte * jax.nn.sigmoid(gte) * up).astype(BF16)
        y_ref[...] = _pack_halves(jnp.dot(he, wd_s[w_cur], preferred_element_type=F32))

        def convert_next(k, carry):
            c = c0 + k
            convert(e_next, c, 1 - w_cur)

            @pl.when(c + 2 < N_STREAM)
            def _():
                start(e_next, c + 2)
            return carry

        lax.fori_loop(0, n_conv, convert_next, 0)

    @pl.when(t >= na_ref[0])
    def _():
        y_ref[...] = jnp.zeros(y_ref.shape, U32)


def _experts(x_sorted, plan, wg, wu, wd, layer, *, tg):
    tile_expert, n_active, n_slots, stream = plan[3:7]
    d2 = x_sorted.shape[1]
    d = 2 * d2
    f = wg.shape[3]
    hbm = pl.BlockSpec(memory_space=pl.ANY)
    grid_spec = pltpu.PrefetchScalarGridSpec(
        num_scalar_prefetch=6,
        grid=(n_slots // tg,),
        in_specs=[pl.BlockSpec((tg, d2), lambda t, *_: (t, 0)), hbm, hbm, hbm],
        out_specs=pl.BlockSpec((tg, d2), lambda t, *_: (t, 0)),
        scratch_shapes=[
            pltpu.VMEM((2, d, f), BF16), pltpu.VMEM((2, d, f), BF16), pltpu.VMEM((2, f, d), BF16),
            pltpu.VMEM((2, d // W_CHUNKS, f), F32), pltpu.VMEM((2, f // W_CHUNKS, d), F32),
            pltpu.SemaphoreType.DMA((2,)), pltpu.SemaphoreType.DMA((2,)),
        ],
    )
    return pl.pallas_call(
        functools.partial(_expert_kernel, layer=layer),
        grid_spec=grid_spec,
        out_shape=jax.ShapeDtypeStruct((n_slots, d2), U32),
        compiler_params=_params("arbitrary"),
        name="experts",
    )(tile_expert, n_active, *stream, x_sorted, wg, wu, wd)


def _combine_kernel(s1_ref, s2_ref, x_ref, wt_ref, y_hbm, mod_ref, g_ref, modn_ref, xo_ref, *refs,
                    tm, seq, n_lat, nb, d, final):
    buf, sem = refs[-2], refs[-1]
    i = pl.program_id(0)
    last = pl.num_programs(0) - 1
    base = i * tm
    slot = i % 2

    def issue(tile, dst_slot):
        def body(r, carry):
            for k, s_ref in enumerate((s1_ref, s2_ref)):
                pltpu.make_async_copy(y_hbm.at[pl.ds(s_ref[tile * tm + r], 1), :],
                                      buf.at[dst_slot, k, pl.ds(r, 1), :], sem.at[dst_slot]).start()
            return carry
        lax.fori_loop(0, tm, body, 0, unroll=8)

    def wait(dst_slot):
        for k in range(2):
            pltpu.make_async_copy(y_hbm.at[pl.ds(0, tm), :], buf.at[dst_slot, k], sem.at[dst_slot]).wait()

    @pl.when(i == 0)
    def _():
        issue(0, 0)

    @pl.when(i < last)
    def _():
        issue(i + 1, 1 - slot)

    wait(slot)
    seg = _segment(base, seq, n_lat, nb)
    m = mod_ref[pl.ds(seg, 1), :]
    y1_lo, y1_hi = _unpack_halves(buf[slot, 0])
    y2_lo, y2_hi = _unpack_halves(buf[slot, 1])
    w1, w2 = wt_ref[:, 0:1], wt_ref[:, 1:2]
    moe = jnp.concatenate([w1 * y1_lo + w2 * y2_lo, w1 * y1_hi + w2 * y2_hi], axis=1)
    xn = x_ref[...] + m[:, 5 * d:6 * d] * moe
    if final:
        xo_ref[...] = _rms(xn) * g_ref[...]
    else:
        xo_ref[...] = xn
        refs[0][...] = _adaln(xn, g_ref, modn_ref, seg, 0, d).astype(BF16)


def _combine(xa, y_sorted, slots, w_tok, mod_l, g_next, mod_next, *, tm, seq, n_lat, nb, final):
    n_rows, d = xa.shape
    row_spec = pl.BlockSpec((tm, d), lambda i, s1, s2: (i, 0))
    mod_spec = pl.BlockSpec((MOD_ROWS, 6 * d), lambda i, s1, s2: (0, 0))
    out_specs = [row_spec]
    out_shape = [jax.ShapeDtypeStruct((n_rows, d), F32)]
    if not final:
        out_specs.append(row_spec)
        out_shape.append(jax.ShapeDtypeStruct((n_rows, d), BF16))
    grid_spec = pltpu.PrefetchScalarGridSpec(
        num_scalar_prefetch=2,
        grid=(n_rows // tm,),
        in_specs=[
            row_spec,
            pl.BlockSpec((tm, 2), lambda i, s1, s2: (i, 0)),
            pl.BlockSpec(memory_space=pl.ANY),
            mod_spec,
            pl.BlockSpec((1, d), lambda i, s1, s2: (0, 0)),
            mod_spec,
        ],
        out_specs=out_specs,
        scratch_shapes=[pltpu.VMEM((2, 2, tm, d // 2), U32), pltpu.SemaphoreType.DMA((2,))],
    )
    kern = functools.partial(_combine_kernel, tm=tm, seq=seq, n_lat=n_lat, nb=nb, d=d, final=final)
    out = pl.pallas_call(
        kern,
        grid_spec=grid_spec,
        out_shape=out_shape,
        compiler_params=_params("arbitrary"),
        name="combine",
    )(slots[0], slots[1], xa, w_tok, y_sorted, mod_l, g_next.reshape(1, d), mod_next)
    return (out[0], None) if final else (out[0], out[1])


def _tile(limit, *sizes):
    t = limit
    while any(s % t for s in sizes):
        t //= 2
    return t


def kernel(x, c, ctx, c_ctx, w_mod, b_mod, norm1_g, norm2_g, w_in, lambda_q1, lambda_k1, lambda_q2,
           lambda_k2, subln_g, conv_w, conv_b, w_attn_branch, w_conv_branch, w_out, w_router,
           router_bias, w_exp_gate, w_exp_up, w_exp_down, final_g):
    nb, seq, d = x.shape
    n_ctx_tok = ctx.shape[1]
    depth = w_mod.shape[0]
    n_exp = w_router.shape[1]
    aw = d // 2
    n_lat = nb * seq
    nt = n_lat + nb * n_ctx_tok
    assert nb < MOD_ROWS and seq % GRID_W == 0 and aw % V_DIM == 0 and n_exp % N_GROUPS == 0

    tm_in = _tile(1024, seq, nb * n_ctx_tok)
    tm_mg = _tile(256, seq, n_ctx_tok)
    tq = _tile(512, seq)
    tg = _tile(256, tm_mg)

    cvec = jnp.zeros((MOD_ROWS, d), F32).at[:nb].set(c).at[nb].set(c_ctx)
    mod = _modulation(cvec, w_mod, b_mod)
    rope = _rope_tables(seq, tm_in)
    xa, h = _prenorm(x.reshape(n_lat, d), ctx.reshape(nb * n_ctx_tok, d), mod[0], norm1_g[0],
                     tm=tm_mg, seq=seq, nb=nb)
    w_ab, w_cb, w_o = (w.astype(BF16) for w in (w_attn_branch, w_conv_branch, w_out))
    w_qkv = _rotary_layout(w_in, aw)

    for l in range(depth):
        last = l == depth - 1
        lam_init = 0.8 - 0.6 * math.exp(-0.3 * l)
        lam = (jnp.exp(jnp.sum(lambda_q1[l] * lambda_k1[l])) - jnp.exp(jnp.sum(lambda_q2[l] * lambda_k2[l]))
               + lam_init)
        lam_vec = jnp.stack([lam, jnp.asarray(1.0 - lam_init, F32)]).astype(F32)

        qkv, rest = _in_proj(h, w_qkv, w_in, l, rope, tm=tm_in, seq=seq, n_lat=n_lat)
        attn_lat, attn_ctx = _attention(qkv, lam_vec, subln_g[l], tq=tq, seq=seq, ctx=n_ctx_tok, n_lat=n_lat,
                                        nb=nb, aw=aw)
        n_rows = n_lat if last else nt
        xa, h2, logits_t = _merge(
            xa, attn_lat, attn_ctx, rest, mod[l], conv_w[l], conv_b[l], w_ab, w_cb, w_o, l, norm2_g[l],
            w_router, n_rows=n_rows, tm=tm_mg, seq=seq, ctx=n_ctx_tok, n_lat=n_lat, nb=nb)
        eidx, ew = _route_tokens(logits_t, router_bias)
        plan = _dispatch_plan(eidx, n_rows, tg, n_exp)
        x_sorted = _dispatch(h2, plan, tm=tm_mg)
        y_sorted = _experts(x_sorted, plan, w_exp_gate, w_exp_up, w_exp_down, l, tg=tg)
        g_next, mod_next = (final_g, mod[l]) if last else (norm1_g[l + 1], mod[l + 1])
        xa, h = _combine(xa, y_sorted, plan[0], ew[:2].T, mod[l], g_next, mod_next, tm=tm_mg, seq=seq,
                         n_lat=n_lat, nb=nb, final=last)

    return xa.reshape(nb, seq, d)
```

```python
import functools
import math

import jax
import jax.numpy as jnp
from jax import lax
from jax.experimental import pallas as pl
from jax.experimental.pallas import tpu as pltpu

HEAD_DIM = 64
V_DIM = 2 * HEAD_DIM
GRID_W = 64
ROPE_THETA = 10000.0
ROPE_AXIS_DIM = HEAD_DIM // 2
N_GROUPS = 4
EPS = 1e-6
LANES = 128
SUBLANES = 8
HALO_ROWS = 2 * SUBLANES
VMEM_LIMIT_BYTES = 56 * 1024 * 1024
MOD_ROWS = 8
LOG2E = math.log2(math.e)

F32 = jnp.float32
BF16 = jnp.bfloat16
HIGHEST = lax.Precision.HIGHEST
NT_DIMS = (((1,), (1,)), ((), ()))


def _params(*sem):
    return pltpu.CompilerParams(dimension_semantics=sem, vmem_limit_bytes=VMEM_LIMIT_BYTES)


def _rms(x):
    return x * lax.rsqrt(jnp.mean(x * x, axis=-1, keepdims=True) + EPS)


U32 = jnp.uint32
HIGH_HALF = 0xFFFF0000


def _pack_halves(x):
    c2 = x.shape[1] // 2
    lo = lax.bitcast_convert_type(x[:, 0:c2].astype(BF16).astype(F32), U32) >> 16
    hi = lax.bitcast_convert_type(x[:, c2:2 * c2].astype(BF16).astype(F32), U32) & U32(HIGH_HALF)
    return hi | lo


def _unpack_halves(w):
    return (lax.bitcast_convert_type(w << 16, F32), lax.bitcast_convert_type(w & U32(HIGH_HALF), F32))


def _segment(row0, seq, n_lat, nb):
    return jnp.where(row0 < n_lat, row0 // seq, nb)


def _adaln(x, g_ref, mod_ref, seg, shift_col, d):
    m = mod_ref[pl.ds(seg, 1), :]
    return _rms(x) * g_ref[...] * (1.0 + m[:, (shift_col + 1) * d:(shift_col + 2) * d]) + m[:, shift_col * d:(shift_col + 1) * d]


def _mod_kernel(c_ref, w_ref, b_ref, o_ref):
    c = c_ref[...]
    sc = c * jax.nn.sigmoid(c)
    o_ref[0] = jnp.dot(sc, w_ref[0], precision=HIGHEST, preferred_element_type=F32) + b_ref[0]


def _modulation(cvec, w_mod, b_mod):
    depth, d, n6 = w_mod.shape
    tn = _tile(1024, n6)
    return pl.pallas_call(
        _mod_kernel,
        grid=(depth, n6 // tn),
        in_specs=[
            pl.BlockSpec((MOD_ROWS, d), lambda l, j: (0, 0)),
            pl.BlockSpec((1, d, tn), lambda l, j: (l, 0, j)),
            pl.BlockSpec((1, 1, tn), lambda l, j: (l, 0, j)),
        ],
        out_specs=pl.BlockSpec((1, MOD_ROWS, tn), lambda l, j: (l, 0, j)),
        out_shape=jax.ShapeDtypeStruct((depth, MOD_ROWS, n6), F32),
        compiler_params=_params("arbitrary", "arbitrary"),
        name="modulation",
    )(cvec, w_mod, b_mod.reshape(depth, 1, n6))


def _prenorm_kernel(x_ref, c_ref, mod_ref, g_ref, xa_ref, h_ref, *, tm, seq, n_lat, nb, d):
    r0 = pl.program_id(0) * tm
    x = jnp.where(r0 < n_lat, x_ref[...], c_ref[...])
    xa_ref[...] = x
    h_ref[...] = _adaln(x, g_ref, mod_ref, _segment(r0, seq, n_lat, nb), 0, d).astype(BF16)


def _prenorm(x2, c2, mod_l, g1, *, tm, seq, nb):
    n_lat, d = x2.shape
    nt = n_lat + c2.shape[0]
    lat_tiles = n_lat // tm
    ctx_tiles = c2.shape[0] // tm
    row_spec = pl.BlockSpec((tm, d), lambda i: (i, 0))
    return pl.pallas_call(
        functools.partial(_prenorm_kernel, tm=tm, seq=seq, n_lat=n_lat, nb=nb, d=d),
        grid=(nt // tm,),
        in_specs=[
            pl.BlockSpec((tm, d), lambda i: (jnp.minimum(i, lat_tiles - 1), 0)),
            pl.BlockSpec((tm, d), lambda i: (jnp.clip(i - lat_tiles, 0, ctx_tiles - 1), 0)),
            pl.BlockSpec((MOD_ROWS, 6 * d), lambda i: (0, 0)),
            pl.BlockSpec((1, d), lambda i: (0, 0)),
        ],
        out_specs=[row_spec, row_spec],
        out_shape=[jax.ShapeDtypeStruct((nt, d), F32), jax.ShapeDtypeStruct((nt, d), BF16)],
        compiler_params=_params("arbitrary"),
        name="prenorm",
    )(x2, c2, mod_l, g1.reshape(1, d))


def _in_kernel(h_ref, w_ref, *refs, rope, aw):
    o_ref, w_scr = refs[-2], refs[-1]
    j = pl.program_id(0)

    @pl.when(pl.program_id(1) == 0)
    def _():
        w_scr[...] = w_ref[0].astype(BF16)

    acc = jnp.dot(h_ref[...], w_scr[...], preferred_element_type=F32)
    if not rope:
        o_ref[...] = acc.astype(o_ref.dtype)
        return
    rc_ref, ra_ref, rb_ref = refs[:3]

    @pl.when(j < 2)
    def _():
        scale = jnp.where(j == 0, HEAD_DIM ** -0.5 * LOG2E, 1.0).astype(F32)
        rc, ra, rb = rc_ref[...], ra_ref[...], rb_ref[...]
        for c in range(aw // LANES):
            a = acc[:, c * LANES:(c + 1) * LANES]
            r = (a * rc + pltpu.roll(a, LANES - ROPE_AXIS_DIM // 2, 1) * ra
                 + pltpu.roll(a, ROPE_AXIS_DIM // 2, 1) * rb)
            o_ref[:, c * LANES:(c + 1) * LANES] = (r * scale).astype(BF16)

    @pl.when(j == 2)
    def _():
        o_ref[...] = acc.astype(BF16)


def _rope_tables(seq, tm):
    pos = jnp.arange(seq)
    row = (pos // GRID_W).astype(F32)
    col = (pos % GRID_W).astype(F32)
    inv = ROPE_THETA ** (-jnp.arange(0, ROPE_AXIS_DIM, 2, dtype=F32) / ROPE_AXIS_DIM)
    lane = jnp.arange(LANES)
    jj = lane % HEAD_DIM
    axis = jj // ROPE_AXIS_DIM
    r = jj % ROPE_AXIS_DIM
    f = r % (ROPE_AXIS_DIM // 2)
    half = r // (ROPE_AXIS_DIM // 2)
    posv = jnp.where(axis[None, :] == 0, row[:, None], col[:, None])
    ang = posv * inv[f][None, :]
    cos, sin = jnp.cos(ang), jnp.sin(ang)
    rc = jnp.concatenate([cos, jnp.ones((tm, LANES), F32)], axis=0)
    ra = jnp.concatenate([jnp.where(half[None, :] == 0, -sin, 0.0), jnp.zeros((tm, LANES), F32)], axis=0)
    rb = jnp.concatenate([jnp.where(half[None, :] == 1, sin, 0.0), jnp.zeros((tm, LANES), F32)], axis=0)
    return rc, ra, rb


def _in_proj(h, w_in, layer, rope, *, tm, seq, n_lat):
    nt, d = h.shape
    in_cols = w_in.shape[2]
    aw = d // 2
    n_qkv = 3
    n_rest = in_cols // aw - n_qkv
    n_lat_tiles = n_lat // tm
    seq_tiles = seq // tm

    def rope_idx(j, i):
        return (jnp.where(i < n_lat_tiles, i % seq_tiles, seq_tiles), 0)

    def call(col0, n_col, out_dtype, with_rope):
        in_specs = [
            pl.BlockSpec((tm, d), lambda j, i: (i, 0)),
            pl.BlockSpec((1, d, aw), lambda j, i: (layer, 0, col0 + j)),
        ]
        args = [h, w_in]
        if with_rope:
            in_specs += [pl.BlockSpec((tm, LANES), rope_idx)] * 3
            args += list(rope)
        return pl.pallas_call(
            functools.partial(_in_kernel, rope=with_rope, aw=aw),
            grid=(n_col, nt // tm),
            in_specs=in_specs,
            out_specs=pl.BlockSpec((tm, aw), lambda j, i: (i, j)),
            out_shape=jax.ShapeDtypeStruct((nt, n_col * aw), out_dtype),
            scratch_shapes=[pltpu.VMEM((d, aw), BF16)],
            compiler_params=_params("arbitrary", "arbitrary"),
            name="in_proj_qkv" if with_rope else "in_proj_rest",
        )(*args)

    return call(0, n_qkv, BF16, True), call(n_qkv, n_rest, BF16, False)


EXP_CHUNK = 128
ATTN_HALF = 512


def _split_maps(q_ref, half):
    q_maps = []
    for hh in range(q_ref.shape[0] // half):
        q = q_ref[hh * half:(hh + 1) * half, :]
        lane = lax.broadcasted_iota(jnp.int32, q.shape, 1)
        zero = jnp.zeros_like(q)
        q_maps += [jnp.where(lane < HEAD_DIM, q, zero), jnp.where(lane >= HEAD_DIM, q, zero)]
    return q_maps


def _finish_heads(outs, lam_ref, g_ref, o_ref, half):
    for hh in range(len(outs) // 2):
        a, b = outs[2 * hh], outs[2 * hh + 1]
        o1 = a[:, 0:V_DIM] / a[:, V_DIM:V_DIM + 1]
        o2 = b[:, 0:V_DIM] * (lam_ref[0] / b[:, V_DIM:V_DIM + 1])
        o = _rms(o1 - o2) * g_ref[...] * lam_ref[1]
        o_ref[hh * half:(hh + 1) * half, :] = o.astype(BF16)


def _diff_attention(lam_ref, q_ref, k_ref, v_ref, g_ref, o_ref, s_refs, e_refs, k0, n, half):
    q_maps = _split_maps(q_ref, half)
    n_chain = len(q_maps)
    outs = [None] * n_chain

    def scores(i):
        s_refs[i][:, 0:n] = lax.dot_general(q_maps[i], k_ref[k0:k0 + n, :], NT_DIMS,
                                            preferred_element_type=F32)

    row_blocks = [slice(0, half // 2), slice(half // 2, half)] if half >= 2 * HALO_ROWS else [slice(0, half)]

    def exps(i):
        for rows in row_blocks:
            mx = jnp.max(s_refs[i][rows, 0:n], axis=-1, keepdims=True)
            for c in range(n // EXP_CHUNK):
                sl = slice(c * EXP_CHUNK, (c + 1) * EXP_CHUNK)
                e_refs[i][rows, sl] = jnp.exp2(s_refs[i][rows, sl] - mx).astype(BF16)

    def values(i):
        outs[i] = jnp.concatenate(
            [jnp.dot(e_refs[i][rows, 0:n], v_ref[k0:k0 + n, :], preferred_element_type=F32) for rows in row_blocks],
            axis=0)

    scores(0)
    for i in range(n_chain):
        if i + 1 < n_chain:
            scores(i + 1)
        exps(i)
        if i >= 1:
            values(i - 1)
    values(n_chain - 1)
    _finish_heads(outs, lam_ref, g_ref, o_ref, half)


def _attn_lat_kernel(lam_ref, q_ref, kl_ref, kc_ref, vl_ref, vc_ref, g_ref, o_ref, k_scr, v_scr,
                     *chain_scr, seq, ctx, half):
    nk = seq + ctx

    @pl.when(pl.program_id(2) == 0)
    def _():
        k_scr[0:seq, :] = kl_ref[...]
        k_scr[seq:nk, :] = kc_ref[...]
        v_scr[0:seq, 0:V_DIM] = vl_ref[...]
        v_scr[seq:nk, 0:V_DIM] = vc_ref[...]
        v_scr[:, V_DIM:2 * V_DIM] = jnp.ones((nk, V_DIM), BF16)

    n_chain = len(chain_scr) // 2
    _diff_attention(lam_ref, q_ref, k_scr, v_scr, g_ref, o_ref, chain_scr[:n_chain], chain_scr[n_chain:],
                    0, nk, half)


def _attn_ctx_kernel(lam_ref, q_ref, k_ref, v_ref, g_ref, o_ref, v_scr, *chain_scr, ctx, half):
    v_scr[:, 0:V_DIM] = v_ref[...]
    v_scr[:, V_DIM:2 * V_DIM] = jnp.ones((ctx, V_DIM), BF16)
    n_chain = len(chain_scr) // 2
    _diff_attention(lam_ref, q_ref, k_ref, v_scr, g_ref, o_ref, chain_scr[:n_chain], chain_scr[n_chain:],
                    0, ctx, half)


def _attention(qkv, lam_vec, subln_g, *, tq, seq, ctx, n_lat, nb, aw):
    nh = aw // V_DIM
    kcol = aw // V_DIM
    vcol = 2 * aw // V_DIM
    ctx0 = n_lat // ctx
    qt = seq // tq
    nk = seq + ctx
    g = subln_g.reshape(1, V_DIM)
    smem = pl.BlockSpec(memory_space=pltpu.SMEM)

    def chain_scratch(rows, half, n):
        n_chain = 2 * (rows // half)
        return ([pltpu.VMEM((half, n), F32)] * n_chain) + ([pltpu.VMEM((half, n), BF16)] * n_chain)

    half = _tile(ATTN_HALF, tq)
    lat = pl.pallas_call(
        functools.partial(_attn_lat_kernel, seq=seq, ctx=ctx, half=half),
        grid=(nb, nh, qt),
        in_specs=[
            smem,
            pl.BlockSpec((tq, V_DIM), lambda b, h, t: (b * qt + t, h)),
            pl.BlockSpec((seq, V_DIM), lambda b, h, t: (b, kcol + h)),
            pl.BlockSpec((ctx, V_DIM), lambda b, h, t: (ctx0 + b, kcol + h)),
            pl.BlockSpec((seq, V_DIM), lambda b, h, t: (b, vcol + h)),
            pl.BlockSpec((ctx, V_DIM), lambda b, h, t: (ctx0 + b, vcol + h)),
            pl.BlockSpec((1, V_DIM), lambda b, h, t: (0, 0)),
        ],
        out_specs=pl.BlockSpec((tq, V_DIM), lambda b, h, t: (b * qt + t, h)),
        out_shape=jax.ShapeDtypeStruct((n_lat, aw), BF16),
        scratch_shapes=[pltpu.VMEM((nk, V_DIM), BF16), pltpu.VMEM((nk, 2 * V_DIM), BF16)]
        + chain_scratch(tq, half, nk),
        compiler_params=_params("arbitrary", "arbitrary", "arbitrary"),
        name="attn_latent",
    )(lam_vec, qkv, qkv, qkv, qkv, qkv, g)

    half_c = _tile(ATTN_HALF, ctx)
    cx = pl.pallas_call(
        functools.partial(_attn_ctx_kernel, ctx=ctx, half=half_c),
        grid=(nb, nh),
        in_specs=[
            smem,
            pl.BlockSpec((ctx, V_DIM), lambda b, h: (ctx0 + b, h)),
            pl.BlockSpec((ctx, V_DIM), lambda b, h: (ctx0 + b, kcol + h)),
            pl.BlockSpec((ctx, V_DIM), lambda b, h: (ctx0 + b, vcol + h)),
            pl.BlockSpec((1, V_DIM), lambda b, h: (0, 0)),
        ],
        out_specs=pl.BlockSpec((ctx, V_DIM), lambda b, h: (b, h)),
        out_shape=jax.ShapeDtypeStruct((nb * ctx, aw), BF16),
        scratch_shapes=[pltpu.VMEM((ctx, 2 * V_DIM), BF16)] + chain_scratch(ctx, half_c, ctx),
        compiler_params=_params("arbitrary", "arbitrary"),
        name="attn_context",
    )(lam_vec, qkv, qkv, qkv, g)
    return lat, cx


def _top2_of4(a, b, c, d):
    m01, n01 = jnp.maximum(a, b), jnp.minimum(a, b)
    m23, n23 = jnp.maximum(c, d), jnp.minimum(c, d)
    return jnp.maximum(m01, m23) + jnp.maximum(jnp.minimum(m01, m23), jnp.maximum(n01, n23))


def _route(logits_t, bias_ref):
    n_exp = logits_t.shape[0]
    per = n_exp // N_GROUPS
    s = [jax.nn.sigmoid(logits_t[e:e + 1, :]) for e in range(n_exp)]
    sb = [s[e] + bias_ref[e:e + 1, :] for e in range(n_exp)]
    gscore = [_top2_of4(*sb[g * per:(g + 1) * per]) for g in range(N_GROUPS)]
    best, gidx = gscore[0], jnp.zeros_like(gscore[0], dtype=jnp.int32)
    for g in range(1, N_GROUPS):
        better = gscore[g] > best
        gidx = jnp.where(better, g, gidx)
        best = jnp.where(better, gscore[g], best)
    cand_b, cand_s = [], []
    for jx in range(per):
        vb, vs = sb[jx], s[jx]
        for g in range(1, N_GROUPS):
            sel = gidx == g
            vb = jnp.where(sel, sb[g * per + jx], vb)
            vs = jnp.where(sel, s[g * per + jx], vs)
        cand_b.append(vb)
        cand_s.append(vs)

    def argmax_first(vals, exclude):
        bv = bi = bs = None
        for jx in range(per):
            v = vals[jx] if exclude is None else jnp.where(exclude == jx, -jnp.inf, vals[jx])
            if bv is None:
                bv, bi, bs = v, jnp.zeros_like(gidx), cand_s[jx]
            else:
                better = v > bv
                bi = jnp.where(better, jx, bi)
                bs = jnp.where(better, cand_s[jx], bs)
                bv = jnp.where(better, v, bv)
        return bi, bs

    j1, w1 = argmax_first(cand_b, None)
    j2, w2 = argmax_first(cand_b, j1)
    tot = w1 + w2
    return gidx * per + j1, gidx * per + j2, w1 / tot, w2 / tot


def _merge_kernel(x_ref, attn_lat_ref, attn_ctx_ref, rest_ref, ccp_ref, cxp_ref, ccn_ref, cxn_ref, mod_ref,
                  convw_ref, convb_ref, wab_ref, wcb_ref, wo_ref, g2_ref, wrh_ref, wrl_ref,
                  xo_ref, h2_ref, logit_ref, *, tm, seq, ctx, n_lat, nb, d, cw):
    i = pl.program_id(0)
    r0 = i * tm
    is_lat = r0 < n_lat
    seg = _segment(r0, seq, n_lat, nb)
    pos = jnp.where(is_lat, r0 % seq, (r0 - n_lat) % ctx)
    slen = jnp.where(is_lat, seq, ctx)
    has_prev = (pos > 0).astype(F32)
    has_next = (pos + tm < slen).astype(F32)

    cb = rest_ref[:, 0:cw].astype(F32)
    u = rest_ref[:, cw:2 * cw].astype(F32) * rest_ref[:, 2 * cw:3 * cw].astype(F32)
    ga = rest_ref[:, 3 * cw:3 * cw + d].astype(F32)
    gc = rest_ref[:, 3 * cw + d:3 * cw + 2 * d].astype(F32)
    hl = HALO_ROWS - 1
    halo_prev = ccp_ref[hl:hl + 1, :].astype(F32) * cxp_ref[hl:hl + 1, :].astype(F32) * has_prev
    halo_next = ccn_ref[0:1, :].astype(F32) * cxn_ref[0:1, :].astype(F32) * has_next
    rid = lax.broadcasted_iota(jnp.int32, u.shape, 0)
    u_prev = jnp.where(rid == 0, halo_prev, pltpu.roll(u, 1, 0))
    u_next = jnp.where(rid == tm - 1, halo_next, pltpu.roll(u, tm - 1, 0))
    y = cb * (u_prev * convw_ref[0:1, :] + u * convw_ref[1:2, :] + u_next * convw_ref[2:3, :]
              + convb_ref[...])

    attn = jnp.where(is_lat, attn_lat_ref[...], attn_ctx_ref[...])
    ma = jnp.dot(attn, wab_ref[0], preferred_element_type=F32)
    mc = jnp.dot(y.astype(BF16), wcb_ref[0], preferred_element_type=F32)
    merged = jax.nn.sigmoid(ga) * ma + jax.nn.sigmoid(gc) * mc
    out = jnp.dot(merged.astype(BF16), wo_ref[0], preferred_element_type=F32)

    m = mod_ref[pl.ds(seg, 1), :]
    xn = x_ref[...] + m[:, 2 * d:3 * d] * out
    xo_ref[...] = xn
    h2 = _adaln(xn, g2_ref, mod_ref, seg, 3, d)
    h2_ref[...] = _pack_halves(h2)

    h_hi = h2.astype(BF16)
    h_lo = (h2 - h_hi.astype(F32)).astype(BF16)
    w_hi = wrh_ref[...]
    logits = (jnp.dot(h_hi, w_hi, preferred_element_type=F32)
              + jnp.dot(h_lo, w_hi, preferred_element_type=F32)
              + jnp.dot(h_hi, wrl_ref[...], preferred_element_type=F32))
    logit_ref[...] = logits.T[0:logit_ref.shape[0], :]


def _route_kernel(logit_ref, rbias_ref, eidx_ref, ew_ref):
    e1, e2, w1, w2 = _route(logit_ref[...], rbias_ref)
    eidx_ref[...] = jnp.zeros(eidx_ref.shape, jnp.int32)
    ew_ref[...] = jnp.zeros(ew_ref.shape, F32)
    eidx_ref[0:1, :] = e1
    eidx_ref[1:2, :] = e2
    ew_ref[0:1, :] = w1
    ew_ref[1:2, :] = w2


def _route_tokens(logits_t, router_bias):
    n_exp, n_rows = logits_t.shape
    tr = _tile(2048, n_rows)
    out_spec = pl.BlockSpec((SUBLANES, tr), lambda i: (0, i))
    return pl.pallas_call(
        _route_kernel,
        grid=(n_rows // tr,),
        in_specs=[pl.BlockSpec((n_exp, tr), lambda i: (0, i)), pl.BlockSpec((n_exp, 1), lambda i: (0, 0))],
        out_specs=[out_spec, out_spec],
        out_shape=[jax.ShapeDtypeStruct((SUBLANES, n_rows), jnp.int32),
                   jax.ShapeDtypeStruct((SUBLANES, n_rows), F32)],
        compiler_params=_params("arbitrary"),
        name="route",
    )(logits_t, router_bias.reshape(n_exp, 1))


def _merge(xa, attn_lat, attn_ctx, rest, mod_l, conv_w, conv_b, w_ab, w_cb, w_o, layer, g2, w_router,
           *, n_rows, tm, seq, ctx, n_lat, nb):
    nt, d = xa.shape
    aw = attn_lat.shape[1]
    lat_tiles = n_lat // tm
    ctx_tiles = attn_ctx.shape[0] // tm
    cw = conv_w.shape[1]
    n_exp = w_router.shape[1]
    wr_pad = jnp.pad(w_router, ((0, 0), (0, LANES - n_exp)))
    wr_hi = wr_pad.astype(BF16)
    wr_lo = (wr_pad - wr_hi.astype(F32)).astype(BF16)
    n_rest = rest.shape[1]
    hb = tm // HALO_ROWS
    last_hblk = nt // HALO_ROWS - 1

    def prev_idx(col):
        return lambda i: (jnp.maximum(i * hb - 1, 0), col)

    def next_idx(col):
        return lambda i: (jnp.minimum((i + 1) * hb, last_hblk), col)

    const = lambda i: (0, 0)
    kern = functools.partial(_merge_kernel, tm=tm, seq=seq, ctx=ctx, n_lat=n_lat, nb=nb, d=d, cw=cw)
    return pl.pallas_call(
        kern,
        grid=(n_rows // tm,),
        in_specs=[
            pl.BlockSpec((tm, d), lambda i: (i, 0)),
            pl.BlockSpec((tm, aw), lambda i: (jnp.minimum(i, lat_tiles - 1), 0)),
            pl.BlockSpec((tm, aw), lambda i: (jnp.clip(i - lat_tiles, 0, ctx_tiles - 1), 0)),
            pl.BlockSpec((tm, n_rest), lambda i: (i, 0)),
            pl.BlockSpec((HALO_ROWS, cw), prev_idx(1)),
            pl.BlockSpec((HALO_ROWS, cw), prev_idx(2)),
            pl.BlockSpec((HALO_ROWS, cw), next_idx(1)),
            pl.BlockSpec((HALO_ROWS, cw), next_idx(2)),
            pl.BlockSpec((MOD_ROWS, 6 * d), const),
            pl.BlockSpec((3, cw), const),
            pl.BlockSpec((1, cw), const),
            pl.BlockSpec((1,) + w_ab.shape[1:], lambda i: (layer, 0, 0)),
            pl.BlockSpec((1,) + w_cb.shape[1:], lambda i: (layer, 0, 0)),
            pl.BlockSpec((1,) + w_o.shape[1:], lambda i: (layer, 0, 0)),
            pl.BlockSpec((1, d), const),
            pl.BlockSpec((d, LANES), const),
            pl.BlockSpec((d, LANES), const),
        ],
        out_specs=[
            pl.BlockSpec((tm, d), lambda i: (i, 0)),
            pl.BlockSpec((tm, d // 2), lambda i: (i, 0)),
            pl.BlockSpec((n_exp, tm), lambda i: (0, i)),
        ],
        out_shape=[
            jax.ShapeDtypeStruct((n_rows, d), F32),
            jax.ShapeDtypeStruct((n_rows, d // 2), U32),
            jax.ShapeDtypeStruct((n_exp, n_rows), F32),
        ],
        compiler_params=_params("arbitrary"),
        name="merge",
    )(xa, attn_lat, attn_ctx, rest, rest, rest, rest, rest, mod_l, conv_w, conv_b.reshape(1, cw), w_ab, w_cb, w_o,
      g2.reshape(1, d), wr_hi, wr_lo)


W_CHUNKS = 4
N_STREAM = 3 * W_CHUNKS


def _weight_stream_schedule(tile_expert, n_active):
    n_tiles = tile_expert.shape[0]
    t = jnp.arange(n_tiles, dtype=jnp.int32)
    active = t < n_active
    prev = jnp.concatenate([tile_expert[:1] - 1, tile_expert[:-1]])
    first = active & (tile_expert != prev)
    gid = jnp.cumsum(first.astype(jnp.int32)) - 1
    group_start = lax.cummax(jnp.where(first, t, 0))
    same = (gid[:, None] == gid[None, :]) & active[None, :]
    size = jnp.maximum(jnp.sum(same.astype(jnp.int32), axis=1), 1)
    pos = t - group_start
    nxt = group_start + size
    has_next = active & (nxt < n_active)
    next_expert = jnp.where(has_next, tile_expert[jnp.minimum(nxt, n_tiles - 1)], 0)
    base, rem = N_STREAM // size, N_STREAM % size
    quota = jnp.where(has_next, base + (pos < rem).astype(jnp.int32), 0)
    chunk0 = pos * base + jnp.minimum(pos, rem)
    return tuple(a.astype(jnp.int32) for a in (gid % 2, next_expert, chunk0, quota))


def _dispatch_plan(eidx, n_tok, tg, n_exp):
    ef = eidx[:2].reshape(-1)
    n2 = 2 * n_tok
    onehot = (ef[:, None] == jnp.arange(n_exp)[None, :]).astype(jnp.int32)
    csum = jnp.cumsum(onehot, axis=0)
    rank = jnp.sum(onehot * csum, axis=1) - 1
    counts = csum[-1]
    padded = ((counts + tg - 1) // tg) * tg
    gend = jnp.cumsum(padded)
    gstart = gend - padded
    dest = jnp.sum(onehot * gstart[None, :], axis=1) + rank
    n_slots = -(-n2 // tg) * tg + n_exp * tg
    tile_start = jnp.arange(n_slots // tg, dtype=jnp.int32) * tg
    tile_expert = jnp.sum((gend[None, :] <= tile_start[:, None]).astype(jnp.int32), axis=1)
    tile_expert = jnp.minimum(tile_expert, n_exp - 1).astype(jnp.int32)
    n_active = (gend[-1] // tg).astype(jnp.int32).reshape(1)
    slots = dest.reshape(2, n_tok).astype(jnp.int32)
    pad_start = jnp.concatenate([gstart + counts, gend[-1:]]).astype(jnp.int32)
    pad_count = jnp.concatenate([padded - counts, n_slots - gend[-1:]]).astype(jnp.int32)
    stream = _weight_stream_schedule(tile_expert, n_active[0])
    return slots, pad_start, pad_count, tile_expert, n_active, n_slots, stream


def _dispatch_kernel(d1_ref, d2_ref, ps_ref, pc_ref, h2_ref, xs_hbm, xbuf, zbuf, sem, zsem, *, tm, n_pad):
    i = pl.program_id(0)
    last = pl.num_programs(0) - 1
    slot = i % 2

    def wait_rows(s):
        for _ in range(2):
            pltpu.make_async_copy(xbuf.at[s], xs_hbm.at[pl.ds(0, tm), :], sem.at[s]).wait()

    def zero_copy(row):
        return pltpu.make_async_copy(zbuf.at[pl.ds(0, 1), :], xs_hbm.at[pl.ds(row, 1), :], zsem)

    @pl.when(i == 0)
    def _():
        zbuf[...] = jnp.zeros(zbuf.shape, zbuf.dtype)
        for e in range(n_pad):
            def start_zero(r, carry, e=e):
                zero_copy(ps_ref[e] + r).start()
                return carry
            lax.fori_loop(0, pc_ref[e], start_zero, 0)

    @pl.when(i >= 2)
    def _():
        wait_rows(slot)

    xbuf[slot] = h2_ref[...]

    def scatter(r, carry):
        for d_ref in (d1_ref, d2_ref):
            pltpu.make_async_copy(xbuf.at[slot, pl.ds(r, 1), :], xs_hbm.at[pl.ds(d_ref[i * tm + r], 1), :],
                                  sem.at[slot]).start()
        return carry

    lax.fori_loop(0, tm, scatter, 0, unroll=8)

    @pl.when(i == last)
    def _():
        wait_rows(slot)
        wait_rows(1 - slot)
        for e in range(n_pad):
            def wait_zero(r, carry):
                zero_copy(0).wait()
                return carry
            lax.fori_loop(0, pc_ref[e], wait_zero, 0)


def _dispatch(h2, plan, *, tm):
    slots, pad_start, pad_count, _, _, n_slots = plan[:6]
    n_tok, d = h2.shape
    assert n_tok // tm >= 2
    grid_spec = pltpu.PrefetchScalarGridSpec(
        num_scalar_prefetch=4,
        grid=(n_tok // tm,),
        in_specs=[pl.BlockSpec((tm, d), lambda i, *_: (i, 0))],
        out_specs=pl.BlockSpec(memory_space=pl.ANY),
        scratch_shapes=[pltpu.VMEM((2, tm, d), h2.dtype), pltpu.VMEM((SUBLANES, d), h2.dtype),
                        pltpu.SemaphoreType.DMA((2,)), pltpu.SemaphoreType.DMA(())],
    )
    return pl.pallas_call(
        functools.partial(_dispatch_kernel, tm=tm, n_pad=pad_start.shape[0]),
        grid_spec=grid_spec,
        out_shape=jax.ShapeDtypeStruct((n_slots, d), h2.dtype),
        compiler_params=_params("arbitrary"),
        name="dispatch",
    )(slots[0], slots[1], pad_start, pad_count, h2)


def _expert_kernel(te_ref, na_ref, par_ref, ne_ref, c0_ref, nq_ref, x_ref, wg_hbm, wu_hbm, wd_hbm, y_ref,
                   wg_s, wu_s, wd_s, stg_a, stg_b, sem_a, sem_b, *, layer):
    t = pl.program_id(0)
    d, f = wg_s.shape[1:]
    rows_a, rows_b = d // W_CHUNKS, f // W_CHUNKS

    def by_matrix(c, fn):
        if isinstance(c, int):
            fn(c // W_CHUNKS, c % W_CHUNKS)
            return
        for m in range(3):
            @pl.when((c >= m * W_CHUNKS) & (c < (m + 1) * W_CHUNKS))
            def _(m=m):
                fn(m, c - m * W_CHUNKS)

    def chunk_copy(e, m, j, s):
        if m < 2:
            src = (wg_hbm, wu_hbm)[m].at[layer, e, pl.ds(pl.multiple_of(j * rows_a, rows_a), rows_a), :]
            return pltpu.make_async_copy(src, stg_a.at[s], sem_a.at[s])
        src = wd_hbm.at[layer, e, pl.ds(pl.multiple_of(j * rows_b, rows_b), rows_b), :]
        return pltpu.make_async_copy(src, stg_b.at[s], sem_b.at[s])

    def start(e, c):
        by_matrix(c, lambda m, j: chunk_copy(e, m, j, c % 2).start())

    def convert(e, c, w):
        def fn(m, j):
            s = c % 2
            chunk_copy(e, m, j, s).wait()
            if m < 2:
                dst = (wg_s, wu_s)[m]
                dst[w, pl.ds(pl.multiple_of(j * rows_a, rows_a), rows_a), :] = stg_a[s].astype(BF16)
            else:
                wd_s[w, pl.ds(pl.multiple_of(j * rows_b, rows_b), rows_b), :] = stg_b[s].astype(BF16)
        by_matrix(c, fn)

    @pl.when(t == 0)
    def _():
        e0 = te_ref[0]
        start(e0, 0)
        start(e0, 1)
        for c in range(N_STREAM):
            convert(e0, c, 0)
            if c + 2 < N_STREAM:
                start(e0, c + 2)

    @pl.when(t < na_ref[0])
    def _():
        e_next, c0, n_conv = ne_ref[t], c0_ref[t], nq_ref[t]
        w_cur = par_ref[t]

        @pl.when((n_conv > 0) & (c0 == 0))
        def _():
            start(e_next, 0)
            start(e_next, 1)

        d2 = x_ref.shape[1]
        x_lo, x_hi = (h.astype(BF16) for h in _unpack_halves(x_ref[...]))
        gte = (jnp.dot(x_lo, wg_s[w_cur, 0:d2, :], preferred_element_type=F32)
               + jnp.dot(x_hi, wg_s[w_cur, d2:2 * d2, :], preferred_element_type=F32))
        up = (jnp.dot(x_lo, wu_s[w_cur, 0:d2, :], preferred_element_type=F32)
              + jnp.dot(x_hi, wu_s[w_cur, d2:2 * d2, :], preferred_element_type=F32))
        he = (gte * jax.nn.sigmoid(gte) * up).astype(BF16)
        y_ref[...] = _pack_halves(jnp.dot(he, wd_s[w_cur], preferred_element_type=F32))

        def convert_next(k, carry):
            c = c0 + k
            convert(e_next, c, 1 - w_cur)

            @pl.when(c + 2 < N_STREAM)
            def _():
                start(e_next, c + 2)
            return carry

        lax.fori_loop(0, n_conv, convert_next, 0)

    @pl.when(t >= na_ref[0])
    def _():
        y_ref[...] = jnp.zeros(y_ref.shape, U32)


def _experts(x_sorted, plan, wg, wu, wd, layer, *, tg):
    tile_expert, n_active, n_slots, stream = plan[3:7]
    d2 = x_sorted.shape[1]
    d = 2 * d2
    f = wg.shape[3]
    hbm = pl.BlockSpec(memory_space=pl.ANY)
    grid_spec = pltpu.PrefetchScalarGridSpec(
        num_scalar_prefetch=6,
        grid=(n_slots // tg,),
        in_specs=[pl.BlockSpec((tg, d2), lambda t, *_: (t, 0)), hbm, hbm, hbm],
        out_specs=pl.BlockSpec((tg, d2), lambda t, *_: (t, 0)),
        scratch_shapes=[
            pltpu.VMEM((2, d, f), BF16), pltpu.VMEM((2, d, f), BF16), pltpu.VMEM((2, f, d), BF16),
            pltpu.VMEM((2, d // W_CHUNKS, f), F32), pltpu.VMEM((2, f // W_CHUNKS, d), F32),
            pltpu.SemaphoreType.DMA((2,)), pltpu.SemaphoreType.DMA((2,)),
        ],
    )
    return pl.pallas_call(
        functools.partial(_expert_kernel, layer=layer),
        grid_spec=grid_spec,
        out_shape=jax.ShapeDtypeStruct((n_slots, d2), U32),
        compiler_params=_params("arbitrary"),
        name="experts",
    )(tile_expert, n_active, *stream, x_sorted, wg, wu, wd)


def _combine_kernel(s1_ref, s2_ref, x_ref, wt_ref, y_hbm, mod_ref, g_ref, modn_ref, xo_ref, *refs,
                    tm, seq, n_lat, nb, d, final):
    buf, sem = refs[-2], refs[-1]
    i = pl.program_id(0)
    last = pl.num_programs(0) - 1
    base = i * tm
    slot = i % 2

    def issue(tile, dst_slot):
        def body(r, carry):
            for k, s_ref in enumerate((s1_ref, s2_ref)):
                pltpu.make_async_copy(y_hbm.at[pl.ds(s_ref[tile * tm + r], 1), :],
                                      buf.at[dst_slot, k, pl.ds(r, 1), :], sem.at[dst_slot]).start()
            return carry
        lax.fori_loop(0, tm, body, 0, unroll=8)

    def wait(dst_slot):
        for k in range(2):
            pltpu.make_async_copy(y_hbm.at[pl.ds(0, tm), :], buf.at[dst_slot, k], sem.at[dst_slot]).wait()

    @pl.when(i == 0)
    def _():
        issue(0, 0)

    @pl.when(i < last)
    def _():
        issue(i + 1, 1 - slot)

    wait(slot)
    seg = _segment(base, seq, n_lat, nb)
    m = mod_ref[pl.ds(seg, 1), :]
    y1_lo, y1_hi = _unpack_halves(buf[slot, 0])
    y2_lo, y2_hi = _unpack_halves(buf[slot, 1])
    w1, w2 = wt_ref[:, 0:1], wt_ref[:, 1:2]
    moe = jnp.concatenate([w1 * y1_lo + w2 * y2_lo, w1 * y1_hi + w2 * y2_hi], axis=1)
    xn = x_ref[...] + m[:, 5 * d:6 * d] * moe
    if final:
        xo_ref[...] = _rms(xn) * g_ref[...]
    else:
        xo_ref[...] = xn
        refs[0][...] = _adaln(xn, g_ref, modn_ref, seg, 0, d).astype(BF16)


def _combine(xa, y_sorted, slots, w_tok, mod_l, g_next, mod_next, *, tm, seq, n_lat, nb, final):
    n_rows, d = xa.shape
    row_spec = pl.BlockSpec((tm, d), lambda i, s1, s2: (i, 0))
    mod_spec = pl.BlockSpec((MOD_ROWS, 6 * d), lambda i, s1, s2: (0, 0))
    out_specs = [row_spec]
    out_shape = [jax.ShapeDtypeStruct((n_rows, d), F32)]
    if not final:
        out_specs.append(row_spec)
        out_shape.append(jax.ShapeDtypeStruct((n_rows, d), BF16))
    grid_spec = pltpu.PrefetchScalarGridSpec(
        num_scalar_prefetch=2,
        grid=(n_rows // tm,),
        in_specs=[
            row_spec,
            pl.BlockSpec((tm, 2), lambda i, s1, s2: (i, 0)),
            pl.BlockSpec(memory_space=pl.ANY),
            mod_spec,
            pl.BlockSpec((1, d), lambda i, s1, s2: (0, 0)),
            mod_spec,
        ],
        out_specs=out_specs,
        scratch_shapes=[pltpu.VMEM((2, 2, tm, d // 2), U32), pltpu.SemaphoreType.DMA((2,))],
    )
    kern = functools.partial(_combine_kernel, tm=tm, seq=seq, n_lat=n_lat, nb=nb, d=d, final=final)
    out = pl.pallas_call(
        kern,
        grid_spec=grid_spec,
        out_shape=out_shape,
        compiler_params=_params("arbitrary"),
        name="combine",
    )(slots[0], slots[1], xa, w_tok, y_sorted, mod_l, g_next.reshape(1, d), mod_next)
    return (out[0], None) if final else (out[0], out[1])


def _tile(limit, *sizes):
    t = limit
    while any(s % t for s in sizes):
        t //= 2
    return t


def kernel(x, c, ctx, c_ctx, w_mod, b_mod, norm1_g, norm2_g, w_in, lambda_q1, lambda_k1, lambda_q2,
           lambda_k2, subln_g, conv_w, conv_b, w_attn_branch, w_conv_branch, w_out, w_router,
           router_bias, w_exp_gate, w_exp_up, w_exp_down, final_g):
    nb, seq, d = x.shape
    n_ctx_tok = ctx.shape[1]
    depth = w_mod.shape[0]
    n_exp = w_router.shape[1]
    aw = d // 2
    n_lat = nb * seq
    nt = n_lat + nb * n_ctx_tok
    assert nb < MOD_ROWS and seq % GRID_W == 0 and aw % V_DIM == 0 and n_exp % N_GROUPS == 0

    tm_in = _tile(1024, seq, nb * n_ctx_tok)
    tm_mg = _tile(256, seq, n_ctx_tok)
    tq = _tile(512, seq)
    tg = _tile(256, tm_mg)

    cvec = jnp.zeros((MOD_ROWS, d), F32).at[:nb].set(c).at[nb].set(c_ctx)
    mod = _modulation(cvec, w_mod, b_mod)
    rope = _rope_tables(seq, tm_in)
    xa, h = _prenorm(x.reshape(n_lat, d), ctx.reshape(nb * n_ctx_tok, d), mod[0], norm1_g[0],
                     tm=tm_mg, seq=seq, nb=nb)
    w_ab, w_cb, w_o = (w.astype(BF16) for w in (w_attn_branch, w_conv_branch, w_out))

    for l in range(depth):
        last = l == depth - 1
        lam_init = 0.8 - 0.6 * math.exp(-0.3 * l)
        lam = (jnp.exp(jnp.sum(lambda_q1[l] * lambda_k1[l])) - jnp.exp(jnp.sum(lambda_q2[l] * lambda_k2[l]))
               + lam_init)
        lam_vec = jnp.stack([lam, jnp.asarray(1.0 - lam_init, F32)]).astype(F32)

        qkv, rest = _in_proj(h, w_in, l, rope, tm=tm_in, seq=seq, n_lat=n_lat)
        attn_lat, attn_ctx = _attention(qkv, lam_vec, subln_g[l], tq=tq, seq=seq, ctx=n_ctx_tok, n_lat=n_lat,
                                        nb=nb, aw=aw)
        n_rows = n_lat if last else nt
        xa, h2, logits_t = _merge(
            xa, attn_lat, attn_ctx, rest, mod[l], conv_w[l], conv_b[l], w_ab, w_cb, w_o, l, norm2_g[l],
            w_router, n_rows=n_rows, tm=tm_mg, seq=seq, ctx=n_ctx_tok, n_lat=n_lat, nb=nb)
        eidx, ew = _route_tokens(logits_t, router_bias)
        plan = _dispatch_plan(eidx, n_rows, tg, n_exp)
        x_sorted = _dispatch(h2, plan, tm=tm_mg)
        y_sorted = _experts(x_sorted, plan, w_exp_gate, w_exp_up, w_exp_down, l, tg=tg)
        g_next, mod_next = (final_g, mod[l]) if last else (norm1_g[l + 1], mod[l + 1])
        xa, h = _combine(xa, y_sorted, plan[0], ew[:2].T, mod[l], g_next, mod_next, tm=tm_mg, seq=seq,
                         n_lat=n_lat, nb=nb, final=last)

    return xa.reshape(nb, seq, d)
```
